```python
import math
import jax
import jax.numpy as jnp
from jax import lax
import numpy as np

D_MODEL = 1024
BATCH = 8
SEQ = 2048
DEPTH = 2
DEC_BATCH = 128
DEC_SEQ = 1
PAST_LEN = 2048
PAGE_SIZE = 128

N_EVEN = (DEPTH + 1) // 2
N_ODD = DEPTH // 2
CONV_W = 4
LRU_W = D_MODEL // 2
LRU_BLOCKS = 8
LRU_BW = LRU_W // LRU_BLOCKS
RG_C = 8.0
H_B = 4
DH_B = D_MODEL // (4 * H_B)
ATT_W = H_B * 2 * DH_B
Q_BLOCK = 128
IN_E = 2 * LRU_W + 3 * ATT_W
MIX_E = LRU_W + ATT_W
DK_C = 128
DV_C = 128
H_C = D_MODEL // DK_C
QKV_C = H_C * (2 * DK_C + DV_C)
IN_O = QKV_C + H_C * DV_C + 2 * H_C
CHUNK = 64
D_FF = 2816
N_EXPERTS = 8
TOP_K = 2
D_FF_E = 3584
EPS = 1e-6
NEG_INF = -1e30

kernel_name = "hybrid_rglru_diffattn_gdn_moe_step"


def rmsnorm(x, g):
    xf = x.astype(jnp.float32)
    y = xf * lax.rsqrt(jnp.mean(xf * xf, axis=-1, keepdims=True) + EPS)
    return (y * g.astype(jnp.float32)).astype(x.dtype)


def l2norm(x):
    xf = x.astype(jnp.float32)
    return xf * lax.rsqrt(jnp.sum(xf * xf, axis=-1, keepdims=True) + EPS)


def swiglu(h, wg, wu, wd):
    return (jax.nn.silu(h @ wg) * (h @ wu)) @ wd


def causal_conv(x, buf, w, b=None):
    T = x.shape[1]
    xp = jnp.concatenate([buf.astype(x.dtype), x], axis=1)
    y = xp[:, 0:T] * w[0]
    for j in range(1, CONV_W):
        y = y + xp[:, j:j + T] * w[j]
    if b is not None:
        y = y + b
    return y, xp[:, T:]


def rg_lru(x, h0, wx, bx, wa, ba, lam):
    B, T, _ = x.shape
    xf = x.astype(jnp.float32)
    xb = xf.reshape(B, T, LRU_BLOCKS, LRU_BW)
    gate_x = jax.nn.sigmoid(jnp.einsum('btnc,ncd->btnd', xb, wx.astype(jnp.float32)).reshape(B, T, LRU_W) + bx.astype(jnp.float32))
    gate_a = jax.nn.sigmoid(jnp.einsum('btnc,ncd->btnd', xb, wa.astype(jnp.float32)).reshape(B, T, LRU_W) + ba.astype(jnp.float32))
    log_a = -RG_C * gate_a * jax.nn.softplus(-lam.astype(jnp.float32))
    a = jnp.exp(log_a)
    b = jnp.sqrt(-jnp.expm1(2.0 * log_a)) * (gate_x * xf)
    b = b.at[:, 0].add(a[:, 0] * h0.astype(jnp.float32))
    _, h = lax.associative_scan(lambda l, r: (l[0] * r[0], r[0] * l[1] + r[1]), (a, b), axis=1)
    return h, h[:, -1]


def diff_softmax_mix(q1, q2, k1, k2, v, mask, lam):
    scale = DH_B ** -0.5
    s1 = jnp.einsum('bqhd,bkhd->bhqk', q1, k1).astype(jnp.float32) * scale
    s2 = jnp.einsum('bqhd,bkhd->bhqk', q2, k2).astype(jnp.float32) * scale
    s1 = jnp.where(mask, s1, NEG_INF)
    s2 = jnp.where(mask, s2, NEG_INF)
    p = jax.nn.softmax(s1, axis=-1) - lam * jax.nn.softmax(s2, axis=-1)
    return jnp.einsum('bhqk,bkhe->bqhe', p.astype(v.dtype), v)


def diff_attention_prompt(q1, q2, k1, k2, v, lam):
    B, T, H, d = q1.shape
    nb = T // Q_BLOCK

    def blk(a):
        return a.reshape((B, nb, Q_BLOCK) + a.shape[2:]).swapaxes(0, 1)

    qpos = jnp.arange(T).reshape(nb, Q_BLOCK)
    kpos = jnp.arange(T)

    def body(args):
        q1b, q2b, qp = args
        return diff_softmax_mix(q1b, q2b, k1, k2, v, kpos[None, :] <= qp[:, None], lam)

    o = lax.map(body, (blk(q1), blk(q2), qpos))
    return o.swapaxes(0, 1).reshape(B, T, H, 2 * d)


def even_mixer(h, conv_buf, h0, past_k, past_v, lam_init, w_in, conv_w, conv_b, lru_wx, lru_bx, lru_wa, lru_ba, lru_lambda,
               g_q, g_k, lam_q1, lam_k1, lam_q2, lam_k2, subln_g, w_out):
    B, T, _ = h.shape
    proj = h @ w_in
    gate = proj[..., :LRU_W]
    xr = proj[..., LRU_W:2 * LRU_W]
    q = proj[..., 2 * LRU_W:2 * LRU_W + ATT_W]
    k = proj[..., 2 * LRU_W + ATT_W:2 * LRU_W + 2 * ATT_W]
    v = proj[..., 2 * LRU_W + 2 * ATT_W:]
    xc, conv_new = causal_conv(xr, conv_buf, conv_w, conv_b)
    hs, h_last = rg_lru(xc, h0, lru_wx, lru_bx, lru_wa, lru_ba, lru_lambda)
    y_a = jax.nn.gelu(gate) * hs.astype(h.dtype)
    q = rmsnorm(q.reshape(B, T, H_B, 2, DH_B), g_q)
    k = rmsnorm(k.reshape(B, T, H_B, 2, DH_B), g_k)
    k_rows = k.reshape(B, T, H_B, 2 * DH_B)
    v_rows = v.reshape(B, T, H_B, 2 * DH_B)
    f32 = jnp.float32
    lam = (jnp.exp(jnp.sum(lam_q1.astype(f32) * lam_k1.astype(f32)))
           - jnp.exp(jnp.sum(lam_q2.astype(f32) * lam_k2.astype(f32))) + lam_init)
    if past_k is None:
        o = diff_attention_prompt(q[..., 0, :], q[..., 1, :], k[..., 0, :], k[..., 1, :], v_rows, lam)
    else:
        P = past_k.shape[1]
        k_all = jnp.concatenate([past_k.astype(k_rows.dtype), k_rows], axis=1)
        v_all = jnp.concatenate([past_v.astype(v_rows.dtype), v_rows], axis=1)
        qpos = P + jnp.arange(T)
        kpos = jnp.arange(P + T)
        o = diff_softmax_mix(q[..., 0, :], q[..., 1, :], k_all[..., :DH_B], k_all[..., DH_B:], v_all,
                             kpos[None, :] <= qpos[:, None], lam)
    o = rmsnorm(o, subln_g) * (1.0 - lam_init)
    y = jnp.concatenate([y_a, o.reshape(B, T, ATT_W)], axis=-1) @ w_out
    return y, k_rows, v_rows, conv_new, h_last


def gated_delta_chunked(q, k, v, g, beta, s0):
    B, T, H, _ = q.shape
    n = -(-T // CHUNK)
    pad = n * CHUNK - T

    def to_chunks(a):
        a = jnp.pad(a.astype(jnp.float32), [(0, 0), (0, pad)] + [(0, 0)] * (a.ndim - 2))
        a = a.reshape((B, n, CHUNK) + a.shape[2:])
        return jnp.moveaxis(a, 3, 2)

    q, k, v, g, beta = (to_chunks(a) for a in (q, k, v, g, beta))
    DV = v.shape[-1]
    G = jnp.cumsum(g, axis=-1)
    idx = jnp.arange(CHUNK)
    incl = idx[:, None] >= idx[None, :]
    strict = idx[:, None] > idx[None, :]
    diff = G[..., :, None] - G[..., None, :]
    decay = jnp.where(incl, jnp.exp(jnp.where(incl, diff, 0.0)), 0.0)
    kb = k * beta[..., None]
    A = jnp.where(strict, jnp.einsum('bnhid,bnhjd->bnhij', kb, k) * decay, 0.0)
    rhs = jnp.concatenate([v * beta[..., None], kb * jnp.exp(G)[..., None]], axis=-1)
    sol = lax.linalg.triangular_solve(A + jnp.eye(CHUNK, dtype=jnp.float32), rhs,
                                      left_side=True, lower=True, unit_diagonal=True)
    u, w = sol[..., :DV], sol[..., DV:]
    qk = jnp.einsum('bnhid,bnhjd->bnhij', q, k) * decay
    q_dec = q * jnp.exp(G)[..., None]
    k_dec = k * jnp.exp(G[..., -1:] - G)[..., None]
    g_end = jnp.exp(G[..., -1])
    xs = tuple(jnp.moveaxis(a, 1, 0) for a in (u, w, qk, q_dec, k_dec, g_end))

    def step(S, xs_i):
        u_i, w_i, qk_i, qd_i, kd_i, ge_i = xs_i
        v_new = u_i - jnp.einsum('bhcd,bhde->bhce', w_i, S)
        o_i = jnp.einsum('bhcd,bhde->bhce', qd_i, S) + jnp.einsum('bhij,bhje->bhie', qk_i, v_new)
        S = S * ge_i[..., None, None] + jnp.einsum('bhcd,bhce->bhde', kd_i, v_new)
        return S, o_i

    s_fin, o = lax.scan(step, s0.astype(jnp.float32), xs)
    o = jnp.transpose(o, (1, 0, 3, 2, 4)).reshape(B, n * CHUNK, H, DV)[:, :T]
    return o, s_fin


def odd_mixer(h, conv_buf, s0, w_in, conv_w, a_log, dt_bias, onorm_g, w_out):
    B, T, _ = h.shape
    proj = h @ w_in
    qkv = proj[..., :QKV_C]
    z = proj[..., QKV_C:QKV_C + H_C * DV_C]
    b_pre = proj[..., QKV_C + H_C * DV_C:QKV_C + H_C * DV_C + H_C]
    a_pre = proj[..., QKV_C + H_C * DV_C + H_C:]
    qkv, conv_new = causal_conv(qkv, conv_buf, conv_w)
    qkv = jax.nn.silu(qkv)
    q = qkv[..., :H_C * DK_C].reshape(B, T, H_C, DK_C)
    k = qkv[..., H_C * DK_C:2 * H_C * DK_C].reshape(B, T, H_C, DK_C)
    v = qkv[..., 2 * H_C * DK_C:].reshape(B, T, H_C, DV_C)
    q = l2norm(q) * (DK_C ** -0.5)
    k = l2norm(k)
    beta = jax.nn.sigmoid(b_pre.astype(jnp.float32))
    g = -jnp.exp(a_log.astype(jnp.float32)) * jax.nn.softplus(a_pre.astype(jnp.float32) + dt_bias.astype(jnp.float32))
    o, s_new = gated_delta_chunked(q, k, v, g, beta, s0)
    o = rmsnorm(o.astype(h.dtype), onorm_g) * jax.nn.silu(z.reshape(B, T, H_C, DV_C))
    return o.reshape(B, T, H_C * DV_C) @ w_out, conv_new, s_new


def moe_swiglu(h, router_w, wg, wu, wd):
    logits = (h @ router_w).astype(jnp.float32)
    top_v, top_i = lax.top_k(logits, TOP_K)
    gates = jax.nn.softmax(top_v, axis=-1)
    comb = jnp.sum(jax.nn.one_hot(top_i, N_EXPERTS, dtype=jnp.float32) * gates[..., None], axis=-2)
    out = jnp.zeros_like(h)
    for e in range(N_EXPERTS):
        out = out + comb[..., e:e + 1].astype(h.dtype) * swiglu(h, wg[e], wu[e], wd[e])
    return out


def setup_inputs(seed: int = 0) -> dict:
    key = jax.random.key(seed)
    keys = jax.random.split(key, 64)
    cnt = [0]
    f32 = jnp.float32

    def nk():
        cnt[0] += 1
        return keys[cnt[0] - 1]

    def nrm(shape, scale):
        return scale * jax.random.normal(nk(), shape, f32)

    def gain(shape):
        return 1.0 + 0.05 * jax.random.normal(nk(), shape, f32)

    n_pages = PAST_LEN // PAGE_SIZE
    n_pool = (DEC_BATCH * n_pages * 5) // 4
    perm = jax.random.permutation(nk(), n_pool)
    page_table = perm[:DEC_BATCH * n_pages].reshape(DEC_BATCH, n_pages).astype(jnp.int32)
    u = jax.random.uniform(nk(), (N_EVEN, LRU_W), f32, 0.9, 0.999)
    s = u ** (1.0 / RG_C)
    lru_lambda = jnp.log(s) - jnp.log1p(-s)
    a_log = jnp.log(jax.random.uniform(nk(), (N_ODD, H_C), f32, 1.0, 16.0))
    dt = jnp.exp(jax.random.uniform(nk(), (N_ODD, H_C), f32, math.log(1e-3), math.log(1e-1)))
    dt_bias = dt + jnp.log(-jnp.expm1(-dt))
    return {
        "x_prompt": nrm((BATCH, SEQ, D_MODEL), 1.0),
        "x_sample": nrm((DEC_BATCH, DEC_SEQ, D_MODEL), 1.0),
        "cache_k": nrm((N_EVEN, n_pool, PAGE_SIZE, H_B, 2 * DH_B), 1.0),
        "cache_v": nrm((N_EVEN, n_pool, PAGE_SIZE, H_B, 2 * DH_B), 1.0),
        "state_lru_conv": nrm((N_EVEN, DEC_BATCH, CONV_W - 1, LRU_W), 1.0),
        "state_lru_h": nrm((N_EVEN, DEC_BATCH, LRU_W), 0.5),
        "state_delta_conv": nrm((N_ODD, DEC_BATCH, CONV_W - 1, QKV_C), 1.0),
        "state_delta_s": nrm((N_ODD, DEC_BATCH, H_C, DK_C, DV_C), 0.1),
        "page_table": page_table,
        "ln_mix_e": gain((N_EVEN, D_MODEL)),
        "w_in_e": nrm((N_EVEN, D_MODEL, IN_E), D_MODEL ** -0.5),
        "conv_lru_w": nrm((N_EVEN, CONV_W, LRU_W), 0.5),
        "conv_lru_b": nrm((N_EVEN, LRU_W), 0.01),
        "lru_wx": nrm((N_EVEN, LRU_BLOCKS, LRU_BW, LRU_BW), LRU_BW ** -0.5),
        "lru_bx": nrm((N_EVEN, LRU_W), 0.01),
        "lru_wa": nrm((N_EVEN, LRU_BLOCKS, LRU_BW, LRU_BW), LRU_BW ** -0.5),
        "lru_ba": nrm((N_EVEN, LRU_W), 0.01),
        "lru_lambda": lru_lambda,
        "qk_gain_q": gain((N_EVEN, DH_B)),
        "qk_gain_k": gain((N_EVEN, DH_B)),
        "lam_q1": nrm((N_EVEN, DH_B), 0.1),
        "lam_k1": nrm((N_EVEN, DH_B), 0.1),
        "lam_q2": nrm((N_EVEN, DH_B), 0.1),
        "lam_k2": nrm((N_EVEN, DH_B), 0.1),
        "subln_g": gain((N_EVEN, 2 * DH_B)),
        "w_out_e": nrm((N_EVEN, MIX_E, D_MODEL), MIX_E ** -0.5),
        "ln_ffn_e": gain((N_EVEN, D_MODEL)),
        "ffn_wg": nrm((N_EVEN, D_MODEL, D_FF), D_MODEL ** -0.5),
        "ffn_wu": nrm((N_EVEN, D_MODEL, D_FF), D_MODEL ** -0.5),
        "ffn_wd": nrm((N_EVEN, D_FF, D_MODEL), D_FF ** -0.5),
        "ln_mix_o": gain((N_ODD, D_MODEL)),
        "w_in_o": nrm((N_ODD, D_MODEL, IN_O), D_MODEL ** -0.5),
        "conv_delta_w": nrm((N_ODD, CONV_W, QKV_C), 0.5),
        "a_log": a_log,
        "dt_bias": dt_bias,
        "onorm_g": gain((N_ODD, DV_C)),
        "w_out_o": nrm((N_ODD, H_C * DV_C, D_MODEL), (H_C * DV_C) ** -0.5),
        "ln_ffn_o": gain((N_ODD, D_MODEL)),
        "router_w": nrm((N_ODD, D_MODEL, N_EXPERTS), D_MODEL ** -0.5),
        "moe_wg": nrm((N_ODD, N_EXPERTS, D_MODEL, D_FF_E), D_MODEL ** -0.5),
        "moe_wu": nrm((N_ODD, N_EXPERTS, D_MODEL, D_FF_E), D_MODEL ** -0.5),
        "moe_wd": nrm((N_ODD, N_EXPERTS, D_FF_E, D_MODEL), D_FF_E ** -0.5),
    }


def reference(x_prompt, x_sample, cache_k, cache_v, state_lru_conv, state_lru_h, state_delta_conv, state_delta_s,
              page_table, ln_mix_e, w_in_e, conv_lru_w, conv_lru_b, lru_wx, lru_bx, lru_wa, lru_ba, lru_lambda,
              qk_gain_q, qk_gain_k, lam_q1, lam_k1, lam_q2, lam_k2, subln_g, w_out_e, ln_ffn_e, ffn_wg, ffn_wu, ffn_wd,
              ln_mix_o, w_in_o, conv_delta_w, a_log, dt_bias, onorm_g, w_out_o, ln_ffn_o, router_w, moe_wg, moe_wu, moe_wd):
    bp = x_prompt.shape[0]
    bs = x_sample.shape[0]
    past_len = page_table.shape[1] * PAGE_SIZE
    xp, xs = x_prompt, x_sample
    kp_l, vp_l, cp_l, hp_l, dcp_l, dsp_l = [], [], [], [], [], []
    ks_l, vs_l, cs_l, hs_l, dcs_l, dss_l = [], [], [], [], [], []
    for layer in range(DEPTH):
        i = layer // 2
        if layer % 2 == 0:
            lam_init = 0.8 - 0.6 * math.exp(-0.3 * layer)
            we = (w_in_e[i], conv_lru_w[i], conv_lru_b[i], lru_wx[i], lru_bx[i], lru_wa[i], lru_ba[i], lru_lambda[i],
                  qk_gain_q[i], qk_gain_k[i], lam_q1[i], lam_k1[i], lam_q2[i], lam_k2[i], subln_g[i], w_out_e[i])
            past_k = cache_k[i, page_table].reshape(bs, past_len, H_B, 2 * DH_B)
            past_v = cache_v[i, page_table].reshape(bs, past_len, H_B, 2 * DH_B)
            yp, kp, vp, cp, hp = even_mixer(rmsnorm(xp, ln_mix_e[i]), jnp.zeros((bp, CONV_W - 1, LRU_W), xp.dtype),
                                            jnp.zeros((bp, LRU_W), jnp.float32), None, None, lam_init, *we)
            ys, ks, vs, cs, hs = even_mixer(rmsnorm(xs, ln_mix_e[i]), state_lru_conv[i], state_lru_h[i],
                                            past_k, past_v, lam_init, *we)
            xp = xp + yp
            xs = xs + ys
            kp_l.append(kp); vp_l.append(vp); cp_l.append(cp); hp_l.append(hp)
            ks_l.append(ks); vs_l.append(vs); cs_l.append(cs); hs_l.append(hs)
            xp = xp + swiglu(rmsnorm(xp, ln_ffn_e[i]), ffn_wg[i], ffn_wu[i], ffn_wd[i])
            xs = xs + swiglu(rmsnorm(xs, ln_ffn_e[i]), ffn_wg[i], ffn_wu[i], ffn_wd[i])
        else:
            wo = (w_in_o[i], conv_delta_w[i], a_log[i], dt_bias[i], onorm_g[i], w_out_o[i])
            yp, dcp, dsp = odd_mixer(rmsnorm(xp, ln_mix_o[i]), jnp.zeros((bp, CONV_W - 1, QKV_C), xp.dtype),
                                     jnp.zeros((bp, H_C, DK_C, DV_C), jnp.float32), *wo)
            ys, dcs, dss = odd_mixer(rmsnorm(xs, ln_mix_o[i]), state_delta_conv[i], state_delta_s[i], *wo)
            xp = xp + yp
            xs = xs + ys
            dcp_l.append(dcp); dsp_l.append(dsp); dcs_l.append(dcs); dss_l.append(dss)
            xp = xp + moe_swiglu(rmsnorm(xp, ln_ffn_o[i]), router_w[i], moe_wg[i], moe_wu[i], moe_wd[i])
            xs = xs + moe_swiglu(rmsnorm(xs, ln_ffn_o[i]), router_w[i], moe_wg[i], moe_wu[i], moe_wd[i])
    return (xp, xs,
            jnp.stack(kp_l), jnp.stack(vp_l), jnp.stack(cp_l), jnp.stack(hp_l), jnp.stack(dcp_l), jnp.stack(dsp_l),
            jnp.stack(ks_l), jnp.stack(vs_l), jnp.stack(cs_l), jnp.stack(hs_l), jnp.stack(dcs_l), jnp.stack(dss_l))
```

```python
import functools
import math

import jax
import jax.numpy as jnp
from jax import lax
from jax.experimental import pallas as pl
from jax.experimental.pallas import tpu as pltpu

F32 = jnp.float32
BF16 = jnp.bfloat16
HIGHEST = lax.Precision.HIGHEST

CONV_W = 4
RG_C = 8.0
H_B = 4
H_C = 8
TOP_K = 2
EPS = 1e-6
NEG_INF = -1e30

LANES = 128
SUBLANES = 8
VMEM_LIMIT_BYTES = 52 * 1024 * 1024

DELTA_CHUNK = 64
MOE_ROWS = 128


def _cparams(*sem):
    return pltpu.CompilerParams(dimension_semantics=sem, vmem_limit_bytes=VMEM_LIMIT_BYTES)


def _full(shape):
    nd = len(shape)
    return pl.BlockSpec(shape, lambda *_: (0,) * nd)


def _rms(x, g):
    return x * lax.rsqrt(jnp.mean(x * x, axis=-1, keepdims=True) + EPS) * g


def _softplus(x):
    return jnp.maximum(x, 0.0) + jnp.log1p(jnp.exp(-jnp.abs(x)))


def _silu(x):
    return x * jax.nn.sigmoid(x)


def _bdot(a, b):
    return jnp.dot(a.astype(BF16), b.astype(BF16), preferred_element_type=F32)


def _bdot_nt(a, b):
    return lax.dot_general(a.astype(BF16), b.astype(BF16), (((1,), (1,)), ((), ())),
                           preferred_element_type=F32)


def _bdot_tn(a, b):
    return lax.dot_general(a.astype(BF16), b.astype(BF16), (((0,), (0,)), ((), ())),
                           preferred_element_type=F32)


def _shifted_rows(prev8, x):
    full = jnp.concatenate([prev8, x], axis=0)
    return [pltpu.roll(full, s, axis=0)[SUBLANES:] for s in (1, 2, 3)]


def _inproj_even_body(x_ref, ln_ref, w_ref, gq_ref, gk_ref, seg_ref,
                      gg_ref, xr_ref, q_ref, k_ref, v_ref, kb_ref, vb_ref, *, width, dh):
    hb = _rms(x_ref[...], ln_ref[...]).astype(BF16)

    def proj(j):
        return jnp.dot(hb, w_ref[:, j * width:(j + 1) * width], preferred_element_type=F32)

    def qk_norm(t, g):
        ms = _bdot(t * t, seg_ref[...]) * (1.0 / dh)
        return t * lax.rsqrt(ms + EPS) * g

    gg_ref[...] = jax.nn.gelu(proj(0)).astype(BF16)
    xr_ref[...] = proj(1)
    q_ref[...] = qk_norm(proj(2), gq_ref[...]).astype(BF16)
    k = qk_norm(proj(3), gk_ref[...])
    k_ref[...] = k
    kb_ref[...] = k.astype(BF16)
    v = proj(4)
    v_ref[...] = v
    vb_ref[...] = v.astype(BF16)


def _inproj_even(x, ln, w, gq, gk, *, tm):
    m, d = x.shape
    width = w.shape[1] // 5
    dh = gq.shape[-1]
    reps = width // dh
    seg = jnp.kron(jnp.eye(reps, dtype=F32), jnp.ones((dh, dh), F32)).astype(BF16)
    gq_t = jnp.tile(gq.reshape(1, dh), (1, reps))
    gk_t = jnp.tile(gk.reshape(1, dh), (1, reps))
    row = lambda i: (i, 0)
    blk = pl.BlockSpec((tm, width), row)
    f32o = jax.ShapeDtypeStruct((m, width), F32)
    b16o = jax.ShapeDtypeStruct((m, width), BF16)
    return pl.pallas_call(
        functools.partial(_inproj_even_body, width=width, dh=dh),
        grid=(m // tm,),
        in_specs=[pl.BlockSpec((tm, d), row), _full((1, d)), _full(w.shape),
                  _full((1, width)), _full((1, width)), _full((width, width))],
        out_specs=[blk] * 7,
        out_shape=[b16o, f32o, b16o, f32o, f32o, b16o, b16o],
        compiler_params=_cparams("parallel"),
        name="inproj_even",
    )(x, ln.reshape(1, d), w, gq_t, gk_t, seg)


def _lru_coeffs(xc, wx_ref, bx_ref, wa_ref, ba_ref, lam_ref):
    xb = xc.astype(BF16)
    gate_x = jax.nn.sigmoid(jnp.dot(xb, wx_ref[...], preferred_element_type=F32) + bx_ref[...])
    gate_a = jax.nn.sigmoid(jnp.dot(xb, wa_ref[...], preferred_element_type=F32) + ba_ref[...])
    log_a = -RG_C * gate_a * _softplus(-lam_ref[...])
    a = jnp.exp(log_a)
    th = jnp.tanh(log_a)
    one_minus_a2 = -2.0 * th / (1.0 - th)
    b = jnp.sqrt(one_minus_a2) * (gate_x * xc)
    return a, b


def _lru_prompt_body(xr_ref, gg_ref, cw_ref, cb_ref, wx_ref, bx_ref, wa_ref, ba_ref, lam_ref,
                     ya_ref, hlast_ref, prev_s, h_s, a_s, b_s, *, tt):
    t = pl.program_id(1)

    @pl.when(t == 0)
    def _():
        prev_s[...] = jnp.zeros_like(prev_s)
        h_s[...] = jnp.zeros_like(h_s)

    xr = xr_ref[0]
    x1, x2, x3 = _shifted_rows(prev_s[...], xr)
    xc = (x3 * cw_ref[0:1, :] + x2 * cw_ref[1:2, :] + x1 * cw_ref[2:3, :] + xr * cw_ref[3:4, :]
          + cb_ref[...])
    prev_s[...] = xr[tt - SUBLANES:, :]
    a, b = _lru_coeffs(xc, wx_ref, bx_ref, wa_ref, ba_ref, lam_ref)
    a_s[...] = a
    b_s[...] = b

    def step(i, h):
        h = a_s[pl.ds(i, 1), :] * h + b_s[pl.ds(i, 1), :]
        b_s[pl.ds(i, 1), :] = h
        return h

    h = lax.fori_loop(0, tt, step, h_s[...], unroll=8)
    h_s[...] = h
    ya_ref[0] = (gg_ref[0].astype(F32) * b_s[...]).astype(BF16)
    hlast_ref[0] = h


def _lru_prompt(xr, gg, cw, cb, wx, bx, wa, ba, lam, *, tt):
    b, t, w = xr.shape
    seq = lambda i, j: (i, j, 0)
    vec = _full((1, w))
    return pl.pallas_call(
        functools.partial(_lru_prompt_body, tt=tt),
        grid=(b, t // tt),
        in_specs=[pl.BlockSpec((1, tt, w), seq), pl.BlockSpec((1, tt, w), seq), _full((CONV_W, w)), vec,
                  _full((w, w)), vec, _full((w, w)), vec, vec],
        out_specs=[pl.BlockSpec((1, tt, w), seq), pl.BlockSpec((1, 1, w), lambda i, j: (i, 0, 0))],
        out_shape=[jax.ShapeDtypeStruct((b, t, w), BF16), jax.ShapeDtypeStruct((b, 1, w), F32)],
        scratch_shapes=[pltpu.VMEM((SUBLANES, w), F32), pltpu.VMEM((1, w), F32),
                        pltpu.VMEM((tt, w), F32), pltpu.VMEM((tt, w), F32)],
        compiler_params=_cparams("parallel", "arbitrary"),
        name="lru_prompt",
    )(xr, gg, cw, cb, wx, bx, wa, ba, lam)


def _lru_step_body(xr_ref, gg_ref, b0_ref, b1_ref, b2_ref, h0_ref, cw_ref, cb_ref,
                   wx_ref, bx_ref, wa_ref, ba_ref, lam_ref, ya_ref, h_ref):
    xr = xr_ref[...]
    xc = (b0_ref[...] * cw_ref[0:1, :] + b1_ref[...] * cw_ref[1:2, :] + b2_ref[...] * cw_ref[2:3, :]
          + xr * cw_ref[3:4, :] + cb_ref[...])
    a, b = _lru_coeffs(xc, wx_ref, bx_ref, wa_ref, ba_ref, lam_ref)
    h = a * h0_ref[...] + b
    h_ref[...] = h
    ya_ref[...] = (gg_ref[...].astype(F32) * h).astype(BF16)


def _lru_step(xr, gg, buf, h0, cw, cb, wx, bx, wa, ba, lam):
    n, w = xr.shape
    args = (xr, gg, buf[:, 0], buf[:, 1], buf[:, 2], h0, cw, cb, wx, bx, wa, ba, lam)
    return pl.pallas_call(
        _lru_step_body,
        grid=(1,),
        in_specs=[_full(a.shape) for a in args],
        out_specs=[_full((n, w)), _full((n, w))],
        out_shape=[jax.ShapeDtypeStruct((n, w), BF16), jax.ShapeDtypeStruct((n, w), F32)],
        compiler_params=_cparams("arbitrary"),
        name="lru_step",
    )(*args)


def _diff_lambda(lq1_ref, lk1_ref, lq2_ref, lk2_ref, lam_init):
    s1 = jnp.sum(lq1_ref[...] * lk1_ref[...], axis=-1, keepdims=True)
    s2 = jnp.sum(lq2_ref[...] * lk2_ref[...], axis=-1, keepdims=True)
    return jnp.exp(s1) - jnp.exp(s2) + lam_init


def _attn_prompt_body(lq1_ref, lk1_ref, lq2_ref, lk2_ref, sg_ref, q_ref, k_ref, v_ref, o_ref,
                      *, tq, dh, lam_init):
    i = pl.program_id(2)
    lam = _diff_lambda(lq1_ref, lk1_ref, lq2_ref, lk2_ref, lam_init)
    q = q_ref[0] * (dh ** -0.5)
    lane = lax.broadcasted_iota(jnp.int32, q.shape, 1)
    zero = jnp.zeros_like(q)
    qq = jnp.concatenate([jnp.where(lane < dh, q, zero), jnp.where(lane >= dh, q, zero)], axis=0)
    rowi = lax.broadcasted_iota(jnp.int32, (2 * tq, tq), 0)
    coli = lax.broadcasted_iota(jnp.int32, (2 * tq, tq), 1)
    causal = coli <= jnp.where(rowi >= tq, rowi - tq, rowi)

    def block(j, carry, masked):
        m, l, acc = carry
        start = pl.multiple_of(j * tq, tq)
        kj = k_ref[0, pl.ds(start, tq), :]
        vj = v_ref[0, pl.ds(start, tq), :]
        s = _bdot_nt(qq, kj)
        if masked:
            s = jnp.where(causal, s, NEG_INF)
        m_new = jnp.maximum(m, jnp.max(s, axis=-1, keepdims=True))
        alpha = jnp.exp(m - m_new)
        p = jnp.exp(s - m_new)
        l = alpha * l + jnp.sum(p, axis=-1, keepdims=True)
        acc = alpha * acc + _bdot(p, vj)
        return m_new, l, acc

    init = (jnp.full((2 * tq, 1), NEG_INF, F32), jnp.zeros((2 * tq, 1), F32),
            jnp.zeros((2 * tq, 2 * dh), F32))
    carry = lax.fori_loop(0, i, lambda j, c: block(j, c, False), init)
    _, l, acc = block(i, carry, True)
    o = acc / l
    od = o[:tq] - lam * o[tq:]
    o_ref[0] = (_rms(od, sg_ref[...]) * (1.0 - lam_init)).astype(BF16)


def _attn_prompt(q, k, v, lq1, lk1, lq2, lk2, sub_g, *, lam_init, tq):
    b, t, w = q.shape
    hw = w // H_B
    dh = hw // 2
    vec = _full((1, dh))
    qblk = pl.BlockSpec((1, tq, hw), lambda bi, h, i: (bi, i, h))
    kvblk = pl.BlockSpec((1, t, hw), lambda bi, h, i: (bi, 0, h))
    return pl.pallas_call(
        functools.partial(_attn_prompt_body, tq=tq, dh=dh, lam_init=lam_init),
        grid=(b, H_B, t // tq),
        in_specs=[vec, vec, vec, vec, _full((1, hw)), qblk, kvblk, kvblk],
        out_specs=qblk,
        out_shape=jax.ShapeDtypeStruct((b, t, w), BF16),
        compiler_params=_cparams("parallel", "parallel", "arbitrary"),
        name="attn_prompt",
    )(lq1.reshape(1, dh), lk1.reshape(1, dh), lq2.reshape(1, dh), lk2.reshape(1, dh),
      sub_g.reshape(1, hw), q, k, v)


def _attn_decode_body(pt_ref, lq1_ref, lk1_ref, lq2_ref, lk2_ref, sg_ref, q_ref, kn_ref, vn_ref, *refs,
                      n_pages, dh, lam_init):
    del pt_ref
    k_refs = refs[:n_pages]
    v_refs = refs[n_pages:2 * n_pages]
    o_ref = refs[2 * n_pages]
    hw = 2 * dh
    rows = 2 * H_B
    lam = _diff_lambda(lq1_ref, lk1_ref, lq2_ref, lk2_ref, lam_init)

    row = lax.broadcasted_iota(jnp.int32, (rows, hw), 0)
    lane = lax.broadcasted_iota(jnp.int32, (rows, hw), 1)

    def by_head(vec_ref):
        out = jnp.zeros((rows, hw), F32)
        for h in range(H_B):
            piece = vec_ref[0, :, h * hw:(h + 1) * hw].astype(F32)
            out = jnp.where(row // 2 == h, jnp.broadcast_to(piece, (rows, hw)), out)
        return out

    half = (lane >= dh) == (row % 2 == 1)
    wq = jnp.where(half, by_head(q_ref), 0.0) * (dh ** -0.5)
    kn = by_head(kn_ref)
    vn = by_head(vn_ref)
    s_new = jnp.sum(wq * kn, axis=-1, keepdims=True)

    n_rows = k_refs[0].shape[1]
    col_head = lax.broadcasted_iota(jnp.int32, (rows, n_rows), 1) % H_B
    own = col_head == lax.broadcasted_iota(jnp.int32, (rows, n_rows), 0) // 2
    wq_b = wq.astype(BF16)
    scores = [jnp.where(own, _bdot_nt(wq_b, k_refs[p][0]), NEG_INF) for p in range(n_pages)]
    m = s_new
    for s in scores:
        m = jnp.maximum(m, jnp.max(s, axis=-1, keepdims=True))
    p_new = jnp.exp(s_new - m)
    l = p_new
    acc = p_new * vn
    for p in range(n_pages):
        pr = jnp.exp(scores[p] - m)
        l = l + jnp.sum(pr, axis=-1, keepdims=True)
        acc = acc + _bdot(pr, v_refs[p][0])
    o = acc / l
    r_out = lax.broadcasted_iota(jnp.int32, (SUBLANES, rows), 0)
    r_in = lax.broadcasted_iota(jnp.int32, (SUBLANES, rows), 1)
    sel = (jnp.where(r_in == 2 * r_out, 1.0, 0.0) - lam * jnp.where(r_in == 2 * r_out + 1, 1.0, 0.0))
    od = jnp.dot(sel, o, preferred_element_type=F32, precision=HIGHEST)[:H_B]
    o_ref[0] = _rms(od, sg_ref[...]) * (1.0 - lam_init)


def _attn_decode(q, k_new, v_new, cache_k, cache_v, page_table, lq1, lk1, lq2, lk2, sub_g, *, lam_init):
    n, w = q.shape
    hw = w // H_B
    dh = hw // 2
    n_pages = page_table.shape[1]
    n_pool, page = cache_k.shape[0], cache_k.shape[1]
    ck = cache_k.reshape(n_pool, page * H_B, hw)
    cv = cache_v.reshape(n_pool, page * H_B, hw)
    vec = pl.BlockSpec((1, dh), lambda i, pt: (0, 0))
    tok = pl.BlockSpec((1, 1, w), lambda i, pt: (i, 0, 0))

    def page_spec(p):
        return pl.BlockSpec((1, page * H_B, hw), lambda i, pt: (pt[i, p], 0, 0))

    pages = [page_spec(p) for p in range(n_pages)]
    out = pl.pallas_call(
        functools.partial(_attn_decode_body, n_pages=n_pages, dh=dh, lam_init=lam_init),
        grid_spec=pltpu.PrefetchScalarGridSpec(
            num_scalar_prefetch=1,
            grid=(n,),
            in_specs=[vec, vec, vec, vec, pl.BlockSpec((1, hw), lambda i, pt: (0, 0)), tok, tok, tok]
            + pages + pages,
            out_specs=pl.BlockSpec((1, H_B, hw), lambda i, pt: (i, 0, 0)),
        ),
        out_shape=jax.ShapeDtypeStruct((n, H_B, hw), F32),
        compiler_params=_cparams("arbitrary"),
        name="attn_decode",
    )(page_table, lq1.reshape(1, dh), lk1.reshape(1, dh), lq2.reshape(1, dh), lk2.reshape(1, dh),
      sub_g.reshape(1, hw), q.reshape(n, 1, w), k_new.reshape(n, 1, w), v_new.reshape(n, 1, w),
      *([ck] * n_pages), *([cv] * n_pages))
    return out.reshape(n, w)


def _outproj_body(*refs, n_in):
    x_ref = refs[0]
    o_ref = refs[1 + 2 * n_in]
    acc = x_ref[...]
    for j in range(n_in):
        acc = acc + jnp.dot(refs[1 + 2 * j][...].astype(BF16), refs[2 + 2 * j][...],
                            preferred_element_type=F32)
    o_ref[...] = acc


def _outproj(x, pairs, *, tm):
    m, d = x.shape
    row = lambda i: (i, 0)
    in_specs = [pl.BlockSpec((tm, d), row)]
    args = [x]
    for a, w in pairs:
        in_specs += [pl.BlockSpec((tm, a.shape[1]), row), _full(w.shape)]
        args += [a, w]
    return pl.pallas_call(
        functools.partial(_outproj_body, n_in=len(pairs)),
        grid=(m // tm,),
        in_specs=in_specs,
        out_specs=pl.BlockSpec((tm, d), row),
        out_shape=jax.ShapeDtypeStruct((m, d), F32),
        compiler_params=_cparams("parallel"),
        name="outproj",
    )(*args)


def _ffn_body(x_ref, ln_ref, wg_ref, wu_ref, wd_ref, o_ref, h_s):
    j = pl.program_id(1)

    @pl.when(j == 0)
    def _():
        x = x_ref[...]
        h_s[...] = _rms(x, ln_ref[...]).astype(BF16)
        o_ref[...] = x

    hb = h_s[...]
    g = jnp.dot(hb, wg_ref[...], preferred_element_type=F32)
    u = jnp.dot(hb, wu_ref[...], preferred_element_type=F32)
    o_ref[...] += jnp.dot((_silu(g) * u).astype(BF16), wd_ref[...], preferred_element_type=F32)


def _ffn(x, ln, wg, wu, wd, *, tm, tf):
    m, d = x.shape
    f = wg.shape[1]
    return pl.pallas_call(
        _ffn_body,
        grid=(m // tm, f // tf),
        in_specs=[pl.BlockSpec((tm, d), lambda i, j: (i, 0)), _full((1, d)),
                  pl.BlockSpec((d, tf), lambda i, j: (0, j)), pl.BlockSpec((d, tf), lambda i, j: (0, j)),
                  pl.BlockSpec((tf, d), lambda i, j: (j, 0))],
        out_specs=pl.BlockSpec((tm, d), lambda i, j: (i, 0)),
        out_shape=jax.ShapeDtypeStruct((m, d), F32),
        scratch_shapes=[pltpu.VMEM((tm, d), BF16)],
        compiler_params=_cparams("parallel", "arbitrary"),
        name="ffn",
    )(x, ln.reshape(1, d), wg, wu, wd)


def _inproj_odd_body(x_ref, ln_ref, w_ref, wba_ref, alog_ref, dtb_ref, qkv_ref, z_ref, gb_ref,
                     *, n_qkv, n_z, tn):
    hb = _rms(x_ref[...], ln_ref[...]).astype(BF16)
    for c in range(n_qkv // tn):
        qkv_ref[:, c * tn:(c + 1) * tn] = jnp.dot(hb, w_ref[:, c * tn:(c + 1) * tn],
                                                  preferred_element_type=F32)
    for c in range(n_z // tn):
        z_ref[:, c * tn:(c + 1) * tn] = jnp.dot(
            hb, w_ref[:, n_qkv + c * tn:n_qkv + (c + 1) * tn], preferred_element_type=F32).astype(BF16)
    ba = jnp.dot(hb, wba_ref[...], preferred_element_type=F32)
    lane = lax.broadcasted_iota(jnp.int32, ba.shape, 1)
    beta = jax.nn.sigmoid(ba)
    g = -jnp.exp(alog_ref[...]) * _softplus(ba + dtb_ref[...])
    gb_ref[...] = jnp.where(lane < H_C, beta, g)


def _inproj_odd(x, ln, w, a_log, dt_bias, *, tm):
    m, d = x.shape
    n_z = H_C * (d // H_C)
    n_qkv = w.shape[1] - n_z - 2 * H_C
    wba = jnp.zeros((d, LANES), BF16).at[:, :2 * H_C].set(w[:, n_qkv + n_z:])
    alog = jnp.zeros((1, LANES), F32).at[0, H_C:2 * H_C].set(a_log)
    dtb = jnp.zeros((1, LANES), F32).at[0, H_C:2 * H_C].set(dt_bias)
    row = lambda i: (i, 0)
    return pl.pallas_call(
        functools.partial(_inproj_odd_body, n_qkv=n_qkv, n_z=n_z, tn=512),
        grid=(m // tm,),
        in_specs=[pl.BlockSpec((tm, d), row), _full((1, d)), _full(w.shape), _full(wba.shape),
                  _full((1, LANES)), _full((1, LANES))],
        out_specs=[pl.BlockSpec((tm, n_qkv), row), pl.BlockSpec((tm, n_z), row),
                   pl.BlockSpec((tm, LANES), row)],
        out_shape=[jax.ShapeDtypeStruct((m, n_qkv), F32), jax.ShapeDtypeStruct((m, n_z), BF16),
                   jax.ShapeDtypeStruct((m, LANES), F32)],
        compiler_params=_cparams("parallel"),
        name="inproj_odd",
    )(x, ln.reshape(1, d), w, wba, alog, dtb)


def _l2n(x):
    return x * lax.rsqrt(jnp.sum(x * x, axis=-1, keepdims=True) + EPS)


def _unit_lower_inverse(a):
    c = a.shape[0]
    eye = (lax.broadcasted_iota(jnp.int32, (c, c), 0) == lax.broadcasted_iota(jnp.int32, (c, c), 1))
    p = -a
    x = jnp.where(eye, 1.0, 0.0) + p
    for _ in range(int(math.log2(c)) - 1):
        p = _bdot(p, p)
        x = x + _bdot(x, p)
    return x


def _delta_prompt_body(qkv_ref, z_ref, gb_ref, cw_ref, og_ref, o_ref, s_ref, prev_s, xc_s, *, tc, dk):
    t = pl.program_id(1)
    c = DELTA_CHUNK
    nk = H_C * dk

    @pl.when(t == 0)
    def _():
        prev_s[...] = jnp.zeros_like(prev_s)
        s_ref[...] = jnp.zeros_like(s_ref)

    x = qkv_ref[0]
    x1, x2, x3 = _shifted_rows(prev_s[...], x)
    xc_s[...] = _silu(x3 * cw_ref[0:1, :] + x2 * cw_ref[1:2, :] + x1 * cw_ref[2:3, :] + x * cw_ref[3:4, :])
    prev_s[...] = x[tc - SUBLANES:, :]

    ri = lax.broadcasted_iota(jnp.int32, (c, c), 0)
    ci = lax.broadcasted_iota(jnp.int32, (c, c), 1)
    incl = ri >= ci
    strict = ri > ci
    tri = jnp.where(incl, 1.0, 0.0)

    def chunk(ic, carry):
        r0 = pl.multiple_of(ic * c, c)
        rows = pl.ds(r0, c)
        gbc = gb_ref[0, rows, :]
        g_all = jnp.dot(tri, gbc, preferred_element_type=F32, precision=HIGHEST)
        g_t = g_all.T
        for h in range(H_C):
            beta = gbc[:, h:h + 1]
            g_col = g_all[:, H_C + h:H_C + h + 1]
            g_row = g_t[H_C + h:H_C + h + 1, :]
            g_end = g_all[c - 1:c, H_C + h:H_C + h + 1]
            q = _l2n(xc_s[rows, h * dk:(h + 1) * dk]) * (dk ** -0.5)
            k = _l2n(xc_s[rows, nk + h * dk:nk + (h + 1) * dk])
            v = xc_s[rows, 2 * nk + h * dk:2 * nk + (h + 1) * dk]
            decay = jnp.where(incl, jnp.exp(jnp.where(incl, g_col - g_row, 0.0)), 0.0)
            kb = k * beta
            e_g = jnp.exp(g_col)
            kk = _bdot_nt(jnp.concatenate([kb, q], axis=0), k)
            a = jnp.where(strict, kk[:c] * decay, 0.0)
            qk = kk[c:] * decay
            sol = _bdot(_unit_lower_inverse(a), jnp.concatenate([v * beta, kb * e_g], axis=1))
            u, w = sol[:, :dk], sol[:, dk:]
            s = s_ref[0, h]
            ws = _bdot(jnp.concatenate([w, q * e_g], axis=0), s)
            v_new = u - ws[:c]
            o = ws[c:] + _bdot(qk, v_new)
            k_dec = k * jnp.exp(g_end - g_col)
            s_ref[0, h] = s * jnp.exp(g_end) + _bdot_tn(k_dec, v_new)
            zt = z_ref[0, rows, h * dk:(h + 1) * dk].astype(F32)
            o_ref[0, rows, h * dk:(h + 1) * dk] = (_rms(o, og_ref[...]) * _silu(zt)).astype(BF16)
        return carry

    lax.fori_loop(0, tc // c, chunk, 0)


def _delta_prompt(qkv, z, gb, cw, og, *, tc):
    b, t, nq = qkv.shape
    nz = z.shape[2]
    dk = nz // H_C
    seq = lambda i, j: (i, j, 0)
    return pl.pallas_call(
        functools.partial(_delta_prompt_body, tc=tc, dk=dk),
        grid=(b, t // tc),
        in_specs=[pl.BlockSpec((1, tc, nq), seq), pl.BlockSpec((1, tc, nz), seq),
                  pl.BlockSpec((1, tc, LANES), seq), _full((CONV_W, nq)), _full((1, dk))],
        out_specs=[pl.BlockSpec((1, tc, nz), seq),
                   pl.BlockSpec((1, H_C, dk, dk), lambda i, j: (i, 0, 0, 0))],
        out_shape=[jax.ShapeDtypeStruct((b, t, nz), BF16), jax.ShapeDtypeStruct((b, H_C, dk, dk), F32)],
        scratch_shapes=[pltpu.VMEM((SUBLANES, nq), F32), pltpu.VMEM((tc, nq), F32)],
        compiler_params=_cparams("parallel", "arbitrary"),
        name="delta_prompt",
    )(qkv, z, gb, cw, og.reshape(1, dk))


def _delta_prep_step_body(x_ref, b0_ref, b1_ref, b2_ref, cw_ref, q_ref, k_ref, v_ref, *, dk):
    xc = _silu(b0_ref[...] * cw_ref[0:1, :] + b1_ref[...] * cw_ref[1:2, :] + b2_ref[...] * cw_ref[2:3, :]
               + x_ref[...] * cw_ref[3:4, :])
    nk = H_C * dk
    for h in range(H_C):
        sl = slice(h * dk, (h + 1) * dk)
        q_ref[:, sl] = _l2n(xc[:, h * dk:(h + 1) * dk]) * (dk ** -0.5)
        k_ref[:, sl] = _l2n(xc[:, nk + h * dk:nk + (h + 1) * dk])
    v_ref[...] = xc[:, 2 * nk:]


def _delta_step_body(q_ref, k_ref, v_ref, gb_ref, z_ref, og_ref, s_ref, o_ref, sn_ref, *, dk):
    row = lax.broadcasted_iota(jnp.int32, (SUBLANES, dk), 0)
    for h in range(H_C):
        sl = slice(h * dk, (h + 1) * dk)
        q, k, v = q_ref[0, :, sl], k_ref[0, :, sl], v_ref[0, :, sl]
        beta = gb_ref[0, :, h:h + 1]
        e_g = jnp.exp(gb_ref[0, :, H_C + h:H_C + h + 1])
        s = s_ref[0, h]
        kq = jnp.where(row == 0, jnp.broadcast_to(k, (SUBLANES, dk)),
                       jnp.where(row == 1, jnp.broadcast_to(q * e_g, (SUBLANES, dk)), 0.0))
        ks = _bdot(kq, s)
        v_new = beta * (v - e_g * ks[0:1])
        o = ks[1:2] + jnp.sum(q * k, axis=-1, keepdims=True) * v_new
        k8 = jnp.where(row == 0, jnp.broadcast_to(k, (SUBLANES, dk)), 0.0)
        v8 = jnp.where(row == 0, jnp.broadcast_to(v_new, (SUBLANES, dk)), 0.0)
        outer = lax.dot_general(k8, v8, (((0,), (0,)), ((), ())), preferred_element_type=F32,
                                precision=HIGHEST)
        sn_ref[0, h] = s * e_g + outer
        zt = z_ref[0, :, sl].astype(F32)
        o_ref[0, :, sl] = (_rms(o, og_ref[...]) * _silu(zt)).astype(BF16)


def _delta_step(qkv, buf, z, gb, s0, cw, og):
    n, nq = qkv.shape
    nz = z.shape[1]
    dk = nz // H_C
    args = (qkv, buf[:, 0], buf[:, 1], buf[:, 2], cw)
    q, k, v = pl.pallas_call(
        functools.partial(_delta_prep_step_body, dk=dk),
        grid=(1,),
        in_specs=[_full(a.shape) for a in args],
        out_specs=[_full((n, nz))] * 3,
        out_shape=[jax.ShapeDtypeStruct((n, nz), F32)] * 3,
        compiler_params=_cparams("arbitrary"),
        name="delta_prep_step",
    )(*args)
    tok = lambda width: pl.BlockSpec((1, 1, width), lambda i: (i, 0, 0))
    st = pl.BlockSpec((1, H_C, dk, dk), lambda i: (i, 0, 0, 0))
    o, s_new = pl.pallas_call(
        functools.partial(_delta_step_body, dk=dk),
        grid=(n,),
        in_specs=[tok(nz), tok(nz), tok(nz), tok(LANES), tok(nz), _full((1, dk)), st],
        out_specs=[tok(nz), st],
        out_shape=[jax.ShapeDtypeStruct((n, 1, nz), BF16), jax.ShapeDtypeStruct(s0.shape, F32)],
        compiler_params=_cparams("parallel"),
        name="delta_step",
    )(q.reshape(n, 1, nz), k.reshape(n, 1, nz), v.reshape(n, 1, nz), gb.reshape(n, 1, LANES),
      z.reshape(n, 1, nz), og.reshape(1, dk), s0)
    return o.reshape(n, nz), s_new


def _router_body(x_ref, ln_ref, rw_ref, hb_ref, comb_ref, rank_ref, rank_t_ref, cnt_ref, *, n_exp):
    h = _rms(x_ref[...], ln_ref[...])
    hb_ref[...] = h.astype(BF16)
    logits = jnp.dot(h, rw_ref[...], preferred_element_type=F32, precision=HIGHEST)
    tm = logits.shape[0]
    lane = lax.broadcasted_iota(jnp.int32, logits.shape, 1).astype(F32)
    logits = jnp.where(lane < n_exp, logits, -jnp.inf)
    m1 = jnp.max(logits, axis=-1, keepdims=True)
    i1 = jnp.min(jnp.where(logits == m1, lane, float(LANES)), axis=-1, keepdims=True)
    rest = jnp.where(lane == i1, -jnp.inf, logits)
    m2 = jnp.max(rest, axis=-1, keepdims=True)
    i2 = jnp.min(jnp.where(rest == m2, lane, float(LANES)), axis=-1, keepdims=True)
    e = jnp.exp(m2 - m1)
    g1 = 1.0 / (1.0 + e)
    g2 = e / (1.0 + e)
    comb = jnp.where(lane == i1, g1, 0.0) + jnp.where(lane == i2, g2, 0.0)
    comb_ref[...] = comb
    sel = comb != 0.0
    ri = lax.broadcasted_iota(jnp.int32, (tm, tm), 0)
    ci = lax.broadcasted_iota(jnp.int32, (tm, tm), 1)
    before = jnp.where(ri > ci, 1.0, 0.0).astype(BF16)
    self_f = jnp.where(sel, 1.0, 0.0)
    rank = jnp.where(sel, jnp.dot(before, self_f.astype(BF16), preferred_element_type=F32), -1.0)
    rank_ref[...] = rank
    rank_t_ref[...] = rank.T[:SUBLANES, :]
    cnt_ref[0] = jnp.sum(self_f, axis=0, keepdims=True).astype(jnp.int32)


def _router(x, ln, rw, *, tm):
    m, d = x.shape
    n_exp = rw.shape[1]
    assert n_exp <= SUBLANES
    rwp = jnp.zeros((d, LANES), F32).at[:, :n_exp].set(rw)
    row = lambda i: (i, 0)
    return pl.pallas_call(
        functools.partial(_router_body, n_exp=n_exp),
        grid=(m // tm,),
        in_specs=[pl.BlockSpec((tm, d), row), _full((1, d)), _full((d, LANES))],
        out_specs=[pl.BlockSpec((tm, d), row), pl.BlockSpec((tm, LANES), row), pl.BlockSpec((tm, LANES), row),
                   pl.BlockSpec((SUBLANES, tm), lambda i: (0, i)),
                   pl.BlockSpec((1, 1, LANES), lambda i: (i, 0, 0))],
        out_shape=[jax.ShapeDtypeStruct((m, d), BF16), jax.ShapeDtypeStruct((m, LANES), F32),
                   jax.ShapeDtypeStruct((m, LANES), F32), jax.ShapeDtypeStruct((SUBLANES, m), F32),
                   jax.ShapeDtypeStruct((m // tm, 1, LANES), jnp.int32)],
        compiler_params=_cparams("parallel"),
        name="router",
    )(x, ln.reshape(1, d), rwp)


def _moe_body(cnt_ref, x_ref, hb_ref, comb_ref, rank_ref, rank_t_ref, wg_ref, wu_ref, wd_ref, o_ref,
              xs_s, ys_s, *, tm, n_chunks):
    i, e, f = pl.program_id(0), pl.program_id(1), pl.program_id(2)
    n_f = pl.num_programs(2)
    count = cnt_ref[i, e]
    r = MOE_ROWS

    @pl.when(jnp.logical_and(e == 0, f == 0))
    def _():
        o_ref[...] = x_ref[...]

    for c in range(n_chunks):
        @pl.when(c * r < count)
        def _(c=c):
            rows = slice(c * r, (c + 1) * r)

            @pl.when(f == 0)
            def _():
                want = (lax.broadcasted_iota(jnp.int32, (r, tm), 0) + c * r).astype(F32)
                onehot = jnp.where(rank_t_ref[pl.ds(e, 1), :] == want, 1.0, 0.0).astype(BF16)
                xs_s[rows, :] = jnp.dot(onehot, hb_ref[...], preferred_element_type=F32).astype(BF16)

            xs = xs_s[rows, :]
            g = jnp.dot(xs, wg_ref[0], preferred_element_type=F32)
            u = jnp.dot(xs, wu_ref[0], preferred_element_type=F32)
            y = jnp.dot((_silu(g) * u).astype(BF16), wd_ref[0], preferred_element_type=F32)

            @pl.when(f == 0)
            def _():
                ys_s[rows, :] = y

            @pl.when(f > 0)
            def _():
                ys_s[rows, :] += y

            @pl.when(f == n_f - 1)
            def _():
                lane = lax.broadcasted_iota(jnp.int32, (tm, LANES), 1)
                pick = lane == e
                rank_col = jnp.sum(jnp.where(pick, rank_ref[...], 0.0), axis=-1, keepdims=True)
                gate_col = jnp.sum(jnp.where(pick, comb_ref[...], 0.0), axis=-1, keepdims=True)
                want = (lax.broadcasted_iota(jnp.int32, (tm, r), 1) + c * r).astype(F32)
                onehot = jnp.where(rank_col == want, 1.0, 0.0).astype(BF16)
                z = jnp.dot(onehot, ys_s[rows, :].astype(BF16), preferred_element_type=F32)
                o_ref[...] += gate_col * z


def _moe(x, hb, comb, rank, rank_t, cnt, wg, wu, wd, *, tm, tf):
    m, d = x.shape
    n_exp, _, f = wg.shape
    n_chunks = tm // MOE_ROWS
    tok = lambda width: pl.BlockSpec((tm, width), lambda i, e, j, cnt: (i, 0))
    return pl.pallas_call(
        functools.partial(_moe_body, tm=tm, n_chunks=n_chunks),
        grid_spec=pltpu.PrefetchScalarGridSpec(
            num_scalar_prefetch=1,
            grid=(m // tm, n_exp, f // tf),
            in_specs=[tok(d), tok(d), tok(LANES), tok(LANES),
                      pl.BlockSpec((SUBLANES, tm), lambda i, e, j, cnt: (0, i)),
                      pl.BlockSpec((1, d, tf), lambda i, e, j, cnt: (e, 0, j)),
                      pl.BlockSpec((1, d, tf), lambda i, e, j, cnt: (e, 0, j)),
                      pl.BlockSpec((1, tf, d), lambda i, e, j, cnt: (e, j, 0))],
            out_specs=tok(d),
            scratch_shapes=[pltpu.VMEM((tm, d), BF16), pltpu.VMEM((tm, d), F32)],
        ),
        out_shape=jax.ShapeDtypeStruct((m, d), F32),
        compiler_params=_cparams("parallel", "arbitrary", "arbitrary"),
        name="moe",
    )(cnt, x, hb, comb, rank, rank_t, wg, wu, wd)


def _moe_layer(x, ln, rw, wg, wu, wd, *, tm, tf):
    hb, comb, rank, rank_t, cnt = _router(x, ln, rw, tm=tm)
    return _moe(x, hb, comb, rank, rank_t, cnt.reshape(cnt.shape[0], LANES), wg, wu, wd, tm=tm, tf=tf)


def _block_diag(w):
    n, c, d = w.shape
    eye = jnp.eye(n, dtype=w.dtype)
    return (eye[:, None, :, None] * w[:, :, None, :]).reshape(n * c, n * d)


def _tile_rows(m, pref):
    return pref if m % pref == 0 else m


def kernel(x_prompt, x_sample, cache_k, cache_v, state_lru_conv, state_lru_h, state_delta_conv, state_delta_s, page_table, ln_mix_e, w_in_e, conv_lru_w, conv_lru_b, lru_wx, lru_bx, lru_wa, lru_ba, lru_lambda, qk_gain_q, qk_gain_k, lam_q1, lam_k1, lam_q2, lam_k2, subln_g, w_out_e, ln_ffn_e, ffn_wg, ffn_wu, ffn_wd, ln_mix_o, w_in_o, conv_delta_w, a_log, dt_bias, onorm_g, w_out_o, ln_ffn_o, router_w, moe_wg, moe_wu, moe_wd):
    bp, tp, d = x_prompt.shape
    bs = x_sample.shape[0]
    n_even, n_odd = w_in_e.shape[0], w_in_o.shape[0]
    lru_w = conv_lru_w.shape[-1]
    att_w = cache_k.shape[-1] * cache_k.shape[-2]
    qkv_c = conv_delta_w.shape[-1]
    dk = onorm_g.shape[-1]
    assert x_sample.shape[1] == 1 and lru_w == att_w and w_in_e.shape[-1] == 5 * lru_w

    xp = x_prompt.reshape(bp * tp, d)
    xs = x_sample.reshape(bs, d)
    mp = bp * tp
    tmp = _tile_rows(mp, 512)
    outs = {k: [] for k in ("kp", "vp", "cp", "hp", "dcp", "dsp", "ks", "vs", "cs", "hs", "dcs", "dss")}
    bf = lambda a: a.astype(BF16)
    vec = lambda a: a.reshape(1, -1)

    for layer in range(n_even + n_odd):
        i = layer // 2
        if layer % 2 == 0:
            lam_init = 0.8 - 0.6 * math.exp(-0.3 * layer)
            w_in, w_out = bf(w_in_e[i]), bf(w_out_e[i])
            lru_args = (conv_lru_w[i], vec(conv_lru_b[i]), bf(_block_diag(lru_wx[i])), vec(lru_bx[i]),
                        bf(_block_diag(lru_wa[i])), vec(lru_ba[i]), vec(lru_lambda[i]))
            lam_args = (lam_q1[i], lam_k1[i], lam_q2[i], lam_k2[i], subln_g[i])
            wg, wu, wd = bf(ffn_wg[i]), bf(ffn_wu[i]), bf(ffn_wd[i])
            tf = wg.shape[1] // 2

            gg, xr, qn, k_rows, v_rows, kb, vb = _inproj_even(xp, ln_mix_e[i], w_in, qk_gain_q[i],
                                                              qk_gain_k[i], tm=tmp)
            seq = lambda a: a.reshape(bp, tp, -1)
            ya, h_last = _lru_prompt(seq(xr), seq(gg), *lru_args, tt=_tile_rows(tp, 512))
            o = _attn_prompt(seq(qn), seq(kb), seq(vb), *lam_args, lam_init=lam_init, tq=_tile_rows(tp, 256))
            xp = _outproj(xp, [(ya.reshape(mp, lru_w), w_out[:lru_w]), (o.reshape(mp, att_w), w_out[lru_w:])],
                          tm=tmp)
            xp = _ffn(xp, ln_ffn_e[i], wg, wu, wd, tm=tmp, tf=tf)
            outs["kp"].append(k_rows.reshape(bp, tp, H_B, att_w // H_B))
            outs["vp"].append(v_rows.reshape(bp, tp, H_B, att_w // H_B))
            outs["cp"].append(seq(xr)[:, tp - (CONV_W - 1):])
            outs["hp"].append(h_last.reshape(bp, lru_w))

            gg, xr, qn, k_rows, v_rows, _, _ = _inproj_even(xs, ln_mix_e[i], w_in, qk_gain_q[i],
                                                            qk_gain_k[i], tm=bs)
            ya, h_new = _lru_step(xr, gg, state_lru_conv[i], state_lru_h[i], *lru_args)
            o = _attn_decode(qn, k_rows, v_rows, cache_k[i], cache_v[i], page_table, *lam_args,
                             lam_init=lam_init)
            xs = _outproj(xs, [(ya, w_out[:lru_w]), (o, w_out[lru_w:])], tm=bs)
            xs = _ffn(xs, ln_ffn_e[i], wg, wu, wd, tm=bs, tf=tf)
            outs["ks"].append(k_rows.reshape(bs, 1, H_B, att_w // H_B))
            outs["vs"].append(v_rows.reshape(bs, 1, H_B, att_w // H_B))
            outs["cs"].append(jnp.concatenate([state_lru_conv[i][:, 1:], xr[:, None]], axis=1))
            outs["hs"].append(h_new)
        else:
            w_in, w_out = bf(w_in_o[i]), bf(w_out_o[i])
            wg, wu, wd = bf(moe_wg[i]), bf(moe_wu[i]), bf(moe_wd[i])
            tf = 512

            qkv, z, gb = _inproj_odd(xp, ln_mix_o[i], w_in, a_log[i], dt_bias[i], tm=tmp)
            seq = lambda a: a.reshape(bp, tp, -1)
            o, s_fin = _delta_prompt(seq(qkv), seq(z), seq(gb), conv_delta_w[i], onorm_g[i],
                                     tc=_tile_rows(tp, 256))
            xp = _outproj(xp, [(o.reshape(mp, -1), w_out)], tm=tmp)
            xp = _moe_layer(xp, ln_ffn_o[i], router_w[i], wg, wu, wd, tm=_tile_rows(mp, 1024), tf=tf)
            outs["dcp"].append(seq(qkv)[:, tp - (CONV_W - 1):])
            outs["dsp"].append(s_fin)

            qkv, z, gb = _inproj_odd(xs, ln_mix_o[i], w_in, a_log[i], dt_bias[i], tm=bs)
            o, s_new = _delta_step(qkv, state_delta_conv[i], z, gb, state_delta_s[i], conv_delta_w[i],
                                   onorm_g[i])
            xs = _outproj(xs, [(o, w_out)], tm=bs)
            xs = _moe_layer(xs, ln_ffn_o[i], router_w[i], wg, wu, wd, tm=bs, tf=tf)
            outs["dcs"].append(jnp.concatenate([state_delta_conv[i][:, 1:], qkv[:, None]], axis=1))
            outs["dss"].append(s_new)

    st = lambda k: jnp.stack(outs[k])
    return (xp.reshape(bp, tp, d), xs.reshape(bs, 1, d),
            st("kp"), st("vp"), st("cp"), st("hp"), st("dcp"), st("dsp"),
            st("ks"), st("vs"), st("cs"), st("hs"), st("dcs"), st("dss"))
```

```python
import functools
import math

import jax
import jax.numpy as jnp
from jax import lax
from jax.experimental import pallas as pl
from jax.experimental.pallas import tpu as pltpu

F32 = jnp.float32
BF16 = jnp.bfloat16
HIGHEST = lax.Precision.HIGHEST

CONV_W = 4
RG_C = 8.0
H_B = 4
H_C = 8
TOP_K = 2
EPS = 1e-6
NEG_INF = -1e30

LANES = 128
SUBLANES = 8
VMEM_LIMIT_BYTES = 52 * 1024 * 1024

DELTA_CHUNK = 64
MOE_ROWS = 128


def _cparams(*sem):
    return pltpu.CompilerParams(dimension_semantics=sem, vmem_limit_bytes=VMEM_LIMIT_BYTES)


def _full(shape):
    nd = len(shape)
    return pl.BlockSpec(shape, lambda *_: (0,) * nd)


def _rms(x, g):
    return x * lax.rsqrt(jnp.mean(x * x, axis=-1, keepdims=True) + EPS) * g


def _softplus(x):
    return jnp.maximum(x, 0.0) + jnp.log1p(jnp.exp(-jnp.abs(x)))


def _silu(x):
    return x * jax.nn.sigmoid(x)


def _bdot(a, b):
    return jnp.dot(a.astype(BF16), b.astype(BF16), preferred_element_type=F32)


def _bdot_nt(a, b):
    return lax.dot_general(a.astype(BF16), b.astype(BF16), (((1,), (1,)), ((), ())),
                           preferred_element_type=F32)


def _bdot_tn(a, b):
    return lax.dot_general(a.astype(BF16), b.astype(BF16), (((0,), (0,)), ((), ())),
                           preferred_element_type=F32)


def _shifted_rows(prev8, x):
    full = jnp.concatenate([prev8, x], axis=0)
    return [pltpu.roll(full, s, axis=0)[SUBLANES:] for s in (1, 2, 3)]


def _inproj_even_body(x_ref, ln_ref, w_ref, gq_ref, gk_ref, seg_ref,
                      gg_ref, xr_ref, q_ref, k_ref, v_ref, kb_ref, vb_ref, *, width, dh):
    hb = _rms(x_ref[...], ln_ref[...]).astype(BF16)

    def proj(j):
        return jnp.dot(hb, w_ref[:, j * width:(j + 1) * width], preferred_element_type=F32)

    def qk_norm(t, g):
        ms = _bdot(t * t, seg_ref[...]) * (1.0 / dh)
        return t * lax.rsqrt(ms + EPS) * g

    gg_ref[...] = jax.nn.gelu(proj(0)).astype(BF16)
    xr_ref[...] = proj(1)
    q_ref[...] = qk_norm(proj(2), gq_ref[...]).astype(BF16)
    k = qk_norm(proj(3), gk_ref[...])
    k_ref[...] = k
    kb_ref[...] = k.astype(BF16)
    v = proj(4)
    v_ref[...] = v
    vb_ref[...] = v.astype(BF16)


def _inproj_even(x, ln, w, gq, gk, *, tm):
    m, d = x.shape
    width = w.shape[1] // 5
    dh = gq.shape[-1]
    reps = width // dh
    seg = jnp.kron(jnp.eye(reps, dtype=F32), jnp.ones((dh, dh), F32)).astype(BF16)
    gq_t = jnp.tile(gq.reshape(1, dh), (1, reps))
    gk_t = jnp.tile(gk.reshape(1, dh), (1, reps))
    row = lambda i: (i, 0)
    blk = pl.BlockSpec((tm, width), row)
    f32o = jax.ShapeDtypeStruct((m, width), F32)
    b16o = jax.ShapeDtypeStruct((m, width), BF16)
    return pl.pallas_call(
        functools.partial(_inproj_even_body, width=width, dh=dh),
        grid=(m // tm,),
        in_specs=[pl.BlockSpec((tm, d), row), _full((1, d)), _full(w.shape),
                  _full((1, width)), _full((1, width)), _full((width, width))],
        out_specs=[blk] * 7,
        out_shape=[b16o, f32o, b16o, f32o, f32o, b16o, b16o],
        compiler_params=_cparams("parallel"),
        name="inproj_even",
    )(x, ln.reshape(1, d), w, gq_t, gk_t, seg)


def _lru_coeffs(xc, wx_ref, bx_ref, wa_ref, ba_ref, lam_ref):
    xb = xc.astype(BF16)
    gate_x = jax.nn.sigmoid(jnp.dot(xb, wx_ref[...], preferred_element_type=F32) + bx_ref[...])
    gate_a = jax.nn.sigmoid(jnp.dot(xb, wa_ref[...], preferred_element_type=F32) + ba_ref[...])
    log_a = -RG_C * gate_a * _softplus(-lam_ref[...])
    a = jnp.exp(log_a)
    th = jnp.tanh(log_a)
    one_minus_a2 = -2.0 * th / (1.0 - th)
    b = jnp.sqrt(one_minus_a2) * (gate_x * xc)
    return a, b


def _lru_prompt_body(xr_ref, gg_ref, cw_ref, cb_ref, wx_ref, bx_ref, wa_ref, ba_ref, lam_ref,
                     ya_ref, hlast_ref, prev_s, h_s, a_s, b_s, *, tt):
    t = pl.program_id(1)

    @pl.when(t == 0)
    def _():
        prev_s[...] = jnp.zeros_like(prev_s)
        h_s[...] = jnp.zeros_like(h_s)

    xr = xr_ref[0]
    x1, x2, x3 = _shifted_rows(prev_s[...], xr)
    xc = (x3 * cw_ref[0:1, :] + x2 * cw_ref[1:2, :] + x1 * cw_ref[2:3, :] + xr * cw_ref[3:4, :]
          + cb_ref[...])
    prev_s[...] = xr[tt - SUBLANES:, :]
    a, b = _lru_coeffs(xc, wx_ref, bx_ref, wa_ref, ba_ref, lam_ref)
    a_s[...] = a
    b_s[...] = b

    def step(i, h):
        h = a_s[pl.ds(i, 1), :] * h + b_s[pl.ds(i, 1), :]
        b_s[pl.ds(i, 1), :] = h
        return h

    h = lax.fori_loop(0, tt, step, h_s[...], unroll=8)
    h_s[...] = h
    ya_ref[0] = (gg_ref[0].astype(F32) * b_s[...]).astype(BF16)
    hlast_ref[0] = h


def _lru_prompt(xr, gg, cw, cb, wx, bx, wa, ba, lam, *, tt):
    b, t, w = xr.shape
    seq = lambda i, j: (i, j, 0)
    vec = _full((1, w))
    return pl.pallas_call(
        functools.partial(_lru_prompt_body, tt=tt),
        grid=(b, t // tt),
        in_specs=[pl.BlockSpec((1, tt, w), seq), pl.BlockSpec((1, tt, w), seq), _full((CONV_W, w)), vec,
                  _full((w, w)), vec, _full((w, w)), vec, vec],
        out_specs=[pl.BlockSpec((1, tt, w), seq), pl.BlockSpec((1, 1, w), lambda i, j: (i, 0, 0))],
        out_shape=[jax.ShapeDtypeStruct((b, t, w), BF16), jax.ShapeDtypeStruct((b, 1, w), F32)],
        scratch_shapes=[pltpu.VMEM((SUBLANES, w), F32), pltpu.VMEM((1, w), F32),
                        pltpu.VMEM((tt, w), F32), pltpu.VMEM((tt, w), F32)],
        compiler_params=_cparams("parallel", "arbitrary"),
        name="lru_prompt",
    )(xr, gg, cw, cb, wx, bx, wa, ba, lam)


def _lru_step_body(xr_ref, gg_ref, b0_ref, b1_ref, b2_ref, h0_ref, cw_ref, cb_ref,
                   wx_ref, bx_ref, wa_ref, ba_ref, lam_ref, ya_ref, h_ref):
    xr = xr_ref[...]
    xc = (b0_ref[...] * cw_ref[0:1, :] + b1_ref[...] * cw_ref[1:2, :] + b2_ref[...] * cw_ref[2:3, :]
          + xr * cw_ref[3:4, :] + cb_ref[...])
    a, b = _lru_coeffs(xc, wx_ref, bx_ref, wa_ref, ba_ref, lam_ref)
    h = a * h0_ref[...] + b
    h_ref[...] = h
    ya_ref[...] = (gg_ref[...].astype(F32) * h).astype(BF16)


def _lru_step(xr, gg, buf, h0, cw, cb, wx, bx, wa, ba, lam):
    n, w = xr.shape
    args = (xr, gg, buf[:, 0], buf[:, 1], buf[:, 2], h0, cw, cb, wx, bx, wa, ba, lam)
    return pl.pallas_call(
        _lru_step_body,
        grid=(1,),
        in_specs=[_full(a.shape) for a in args],
        out_specs=[_full((n, w)), _full((n, w))],
        out_shape=[jax.ShapeDtypeStruct((n, w), BF16), jax.ShapeDtypeStruct((n, w), F32)],
        compiler_params=_cparams("arbitrary"),
        name="lru_step",
    )(*args)


def _diff_lambda(lq1_ref, lk1_ref, lq2_ref, lk2_ref, lam_init):
    s1 = jnp.sum(lq1_ref[...] * lk1_ref[...], axis=-1, keepdims=True)
    s2 = jnp.sum(lq2_ref[...] * lk2_ref[...], axis=-1, keepdims=True)
    return jnp.exp(s1) - jnp.exp(s2) + lam_init


def _attn_prompt_body(lq1_ref, lk1_ref, lq2_ref, lk2_ref, sg_ref, q_ref, k_ref, v_ref, o_ref,
                      *, tq, dh, lam_init):
    i = pl.program_id(1)
    hw = 2 * dh
    heads = range(H_B)
    lam = _diff_lambda(lq1_ref, lk1_ref, lq2_ref, lk2_ref, lam_init)
    lane = lax.broadcasted_iota(jnp.int32, (tq, hw), 1)
    qq = []
    for h in heads:
        q = q_ref[0, :, h * hw:(h + 1) * hw] * (dh ** -0.5)
        zero = jnp.zeros_like(q)
        qq.append(jnp.concatenate([jnp.where(lane < dh, q, zero), jnp.where(lane >= dh, q, zero)], axis=0))
    rowi = lax.broadcasted_iota(jnp.int32, (2 * tq, tq), 0)
    coli = lax.broadcasted_iota(jnp.int32, (2 * tq, tq), 1)
    causal = coli <= jnp.where(rowi >= tq, rowi - tq, rowi)

    def block(j, carry, masked):
        start = pl.multiple_of(j * tq, tq)
        kj = k_ref[0, pl.ds(start, tq), :]
        vj = v_ref[0, pl.ds(start, tq), :]
        s = [_bdot_nt(qq[h], kj[:, h * hw:(h + 1) * hw]) for h in heads]
        if masked:
            s = [jnp.where(causal, s[h], NEG_INF) for h in heads]
        m_new = [jnp.maximum(carry[h][0], jnp.max(s[h], axis=-1, keepdims=True)) for h in heads]
        alpha = [jnp.exp(carry[h][0] - m_new[h]) for h in heads]
        p = [jnp.exp(s[h] - m_new[h]) for h in heads]
        l = [alpha[h] * carry[h][1] + jnp.sum(p[h], axis=-1, keepdims=True) for h in heads]
        acc = [alpha[h] * carry[h][2] + _bdot(p[h], vj[:, h * hw:(h + 1) * hw]) for h in heads]
        return tuple((m_new[h], l[h], acc[h]) for h in heads)

    init = tuple((jnp.full((2 * tq, 1), NEG_INF, F32), jnp.zeros((2 * tq, 1), F32),
                  jnp.zeros((2 * tq, hw), F32)) for _ in heads)
    carry = lax.fori_loop(0, i, lambda j, c: block(j, c, False), init)
    carry = block(i, carry, True)
    out = []
    for h in heads:
        _, l, acc = carry[h]
        o = acc / l
        od = o[:tq] - lam * o[tq:]
        out.append(_rms(od, sg_ref[...]) * (1.0 - lam_init))
    o_ref[0] = jnp.concatenate(out, axis=1).astype(BF16)


def _attn_prompt(q, k, v, lq1, lk1, lq2, lk2, sub_g, *, lam_init, tq):
    b, t, w = q.shape
    hw = w // H_B
    dh = hw // 2
    vec = _full((1, dh))
    qblk = pl.BlockSpec((1, tq, w), lambda bi, i: (bi, i, 0))
    kvblk = pl.BlockSpec((1, t, w), lambda bi, i: (bi, 0, 0))
    return pl.pallas_call(
        functools.partial(_attn_prompt_body, tq=tq, dh=dh, lam_init=lam_init),
        grid=(b, t // tq),
        in_specs=[vec, vec, vec, vec, _full((1, hw)), qblk, kvblk, kvblk],
        out_specs=qblk,
        out_shape=jax.ShapeDtypeStruct((b, t, w), BF16),
        compiler_params=_cparams("parallel", "arbitrary"),
        name="attn_prompt",
    )(lq1.reshape(1, dh), lk1.reshape(1, dh), lq2.reshape(1, dh), lk2.reshape(1, dh),
      sub_g.reshape(1, hw), q, k, v)


def _attn_decode_body(pt_ref, lq1_ref, lk1_ref, lq2_ref, lk2_ref, sg_ref, q_ref, kn_ref, vn_ref, *refs,
                      n_pages, dh, lam_init):
    del pt_ref
    k_refs = refs[:n_pages]
    v_refs = refs[n_pages:2 * n_pages]
    o_ref = refs[2 * n_pages]
    hw = 2 * dh
    rows = 2 * H_B
    lam = _diff_lambda(lq1_ref, lk1_ref, lq2_ref, lk2_ref, lam_init)

    row = lax.broadcasted_iota(jnp.int32, (rows, hw), 0)
    lane = lax.broadcasted_iota(jnp.int32, (rows, hw), 1)

    def by_head(vec_ref):
        out = jnp.zeros((rows, hw), F32)
        for h in range(H_B):
            piece = vec_ref[0, :, h * hw:(h + 1) * hw].astype(F32)
            out = jnp.where(row // 2 == h, jnp.broadcast_to(piece, (rows, hw)), out)
        return out

    half = (lane >= dh) == (row % 2 == 1)
    wq = jnp.where(half, by_head(q_ref), 0.0) * (dh ** -0.5)
    kn = by_head(kn_ref)
    vn = by_head(vn_ref)
    s_new = jnp.sum(wq * kn, axis=-1, keepdims=True)

    n_rows = k_refs[0].shape[1]
    col_head = lax.broadcasted_iota(jnp.int32, (rows, n_rows), 1) % H_B
    own = col_head == lax.broadcasted_iota(jnp.int32, (rows, n_rows), 0) // 2
    wq_b = wq.astype(BF16)
    scores = [jnp.where(own, _bdot_nt(wq_b, k_refs[p][0]), NEG_INF) for p in range(n_pages)]
    m = s_new
    for s in scores:
        m = jnp.maximum(m, jnp.max(s, axis=-1, keepdims=True))
    p_new = jnp.exp(s_new - m)
    l = p_new
    acc = p_new * vn
    for p in range(n_pages):
        pr = jnp.exp(scores[p] - m)
        l = l + jnp.sum(pr, axis=-1, keepdims=True)
        acc = acc + _bdot(pr, v_refs[p][0])
    o = acc / l
    r_out = lax.broadcasted_iota(jnp.int32, (SUBLANES, rows), 0)
    r_in = lax.broadcasted_iota(jnp.int32, (SUBLANES, rows), 1)
    sel = (jnp.where(r_in == 2 * r_out, 1.0, 0.0) - lam * jnp.where(r_in == 2 * r_out + 1, 1.0, 0.0))
    od = jnp.dot(sel, o, preferred_element_type=F32, precision=HIGHEST)[:H_B]
    o_ref[0] = _rms(od, sg_ref[...]) * (1.0 - lam_init)


def _attn_decode(q, k_new, v_new, cache_k, cache_v, page_table, lq1, lk1, lq2, lk2, sub_g, *, lam_init):
    n, w = q.shape
    hw = w // H_B
    dh = hw // 2
    n_pages = page_table.shape[1]
    n_pool, page = cache_k.shape[0], cache_k.shape[1]
    ck = cache_k.reshape(n_pool, page * H_B, hw)
    cv = cache_v.reshape(n_pool, page * H_B, hw)
    vec = pl.BlockSpec((1, dh), lambda i, pt: (0, 0))
    tok = pl.BlockSpec((1, 1, w), lambda i, pt: (i, 0, 0))

    def page_spec(p):
        return pl.BlockSpec((1, page * H_B, hw), lambda i, pt: (pt[i, p], 0, 0))

    pages = [page_spec(p) for p in range(n_pages)]
    out = pl.pallas_call(
        functools.partial(_attn_decode_body, n_pages=n_pages, dh=dh, lam_init=lam_init),
        grid_spec=pltpu.PrefetchScalarGridSpec(
            num_scalar_prefetch=1,
            grid=(n,),
            in_specs=[vec, vec, vec, vec, pl.BlockSpec((1, hw), lambda i, pt: (0, 0)), tok, tok, tok]
            + pages + pages,
            out_specs=pl.BlockSpec((1, H_B, hw), lambda i, pt: (i, 0, 0)),
        ),
        out_shape=jax.ShapeDtypeStruct((n, H_B, hw), F32),
        compiler_params=_cparams("arbitrary"),
        name="attn_decode",
    )(page_table, lq1.reshape(1, dh), lk1.reshape(1, dh), lq2.reshape(1, dh), lk2.reshape(1, dh),
      sub_g.reshape(1, hw), q.reshape(n, 1, w), k_new.reshape(n, 1, w), v_new.reshape(n, 1, w),
      *([ck] * n_pages), *([cv] * n_pages))
    return out.reshape(n, w)


def _outproj_body(*refs, n_in):
    x_ref = refs[0]
    o_ref = refs[1 + 2 * n_in]
    acc = x_ref[...]
    for j in range(n_in):
        acc = acc + jnp.dot(refs[1 + 2 * j][...].astype(BF16), refs[2 + 2 * j][...],
                            preferred_element_type=F32)
    o_ref[...] = acc


def _outproj(x, pairs, *, tm):
    m, d = x.shape
    row = lambda i: (i, 0)
    in_specs = [pl.BlockSpec((tm, d), row)]
    args = [x]
    for a, w in pairs:
        in_specs += [pl.BlockSpec((tm, a.shape[1]), row), _full(w.shape)]
        args += [a, w]
    return pl.pallas_call(
        functools.partial(_outproj_body, n_in=len(pairs)),
        grid=(m // tm,),
        in_specs=in_specs,
        out_specs=pl.BlockSpec((tm, d), row),
        out_shape=jax.ShapeDtypeStruct((m, d), F32),
        compiler_params=_cparams("parallel"),
        name="outproj",
    )(*args)


def _ffn_body(x_ref, ln_ref, wg_ref, wu_ref, wd_ref, o_ref, h_s):
    j = pl.program_id(1)

    @pl.when(j == 0)
    def _():
        x = x_ref[...]
        h_s[...] = _rms(x, ln_ref[...]).astype(BF16)
        o_ref[...] = x

    hb = h_s[...]
    g = jnp.dot(hb, wg_ref[...], preferred_element_type=F32)
    u = jnp.dot(hb, wu_ref[...], preferred_element_type=F32)
    o_ref[...] += jnp.dot((_silu(g) * u).astype(BF16), wd_ref[...], preferred_element_type=F32)


def _ffn(x, ln, wg, wu, wd, *, tm, tf):
    m, d = x.shape
    f = wg.shape[1]
    return pl.pallas_call(
        _ffn_body,
        grid=(m // tm, f // tf),
        in_specs=[pl.BlockSpec((tm, d), lambda i, j: (i, 0)), _full((1, d)),
                  pl.BlockSpec((d, tf), lambda i, j: (0, j)), pl.BlockSpec((d, tf), lambda i, j: (0, j)),
                  pl.BlockSpec((tf, d), lambda i, j: (j, 0))],
        out_specs=pl.BlockSpec((tm, d), lambda i, j: (i, 0)),
        out_shape=jax.ShapeDtypeStruct((m, d), F32),
        scratch_shapes=[pltpu.VMEM((tm, d), BF16)],
        compiler_params=_cparams("parallel", "arbitrary"),
        name="ffn",
    )(x, ln.reshape(1, d), wg, wu, wd)


def _inproj_odd_body(x_ref, ln_ref, w_ref, wba_ref, alog_ref, dtb_ref, qkv_ref, z_ref, gb_ref,
                     *, n_qkv, n_z, tn):
    hb = _rms(x_ref[...], ln_ref[...]).astype(BF16)
    for c in range(n_qkv // tn):
        qkv_ref[:, c * tn:(c + 1) * tn] = jnp.dot(hb, w_ref[:, c * tn:(c + 1) * tn],
                                                  preferred_element_type=F32)
    for c in range(n_z // tn):
        z_ref[:, c * tn:(c + 1) * tn] = jnp.dot(
            hb, w_ref[:, n_qkv + c * tn:n_qkv + (c + 1) * tn], preferred_element_type=F32).astype(BF16)
    ba = jnp.dot(hb, wba_ref[...], preferred_element_type=F32)
    lane = lax.broadcasted_iota(jnp.int32, ba.shape, 1)
    beta = jax.nn.sigmoid(ba)
    g = -jnp.exp(alog_ref[...]) * _softplus(ba + dtb_ref[...])
    gb_ref[...] = jnp.where(lane < H_C, beta, g)


def _inproj_odd(x, ln, w, a_log, dt_bias, *, tm):
    m, d = x.shape
    n_z = H_C * (d // H_C)
    n_qkv = w.shape[1] - n_z - 2 * H_C
    wba = jnp.zeros((d, LANES), BF16).at[:, :2 * H_C].set(w[:, n_qkv + n_z:])
    alog = jnp.zeros((1, LANES), F32).at[0, H_C:2 * H_C].set(a_log)
    dtb = jnp.zeros((1, LANES), F32).at[0, H_C:2 * H_C].set(dt_bias)
    row = lambda i: (i, 0)
    return pl.pallas_call(
        functools.partial(_inproj_odd_body, n_qkv=n_qkv, n_z=n_z, tn=512),
        grid=(m // tm,),
        in_specs=[pl.BlockSpec((tm, d), row), _full((1, d)), _full(w.shape), _full(wba.shape),
                  _full((1, LANES)), _full((1, LANES))],
        out_specs=[pl.BlockSpec((tm, n_qkv), row), pl.BlockSpec((tm, n_z), row),
                   pl.BlockSpec((tm, LANES), row)],
        out_shape=[jax.ShapeDtypeStruct((m, n_qkv), F32), jax.ShapeDtypeStruct((m, n_z), BF16),
                   jax.ShapeDtypeStruct((m, LANES), F32)],
        compiler_params=_cparams("parallel"),
        name="inproj_odd",
    )(x, ln.reshape(1, d), w, wba, alog, dtb)


def _l2n(x):
    return x * lax.rsqrt(jnp.sum(x * x, axis=-1, keepdims=True) + EPS)


def _delta_prompt_body(qkv_ref, z_ref, gb_ref, cw_ref, og_ref, o_ref, s_ref, prev_s, xc_s, *, tc, dk):
    t = pl.program_id(1)
    c = DELTA_CHUNK
    nk = H_C * dk

    @pl.when(t == 0)
    def _():
        prev_s[...] = jnp.zeros_like(prev_s)
        s_ref[...] = jnp.zeros_like(s_ref)

    x = qkv_ref[0]
    x1, x2, x3 = _shifted_rows(prev_s[...], x)
    xc_s[...] = _silu(x3 * cw_ref[0:1, :] + x2 * cw_ref[1:2, :] + x1 * cw_ref[2:3, :] + x * cw_ref[3:4, :])
    prev_s[...] = x[tc - SUBLANES:, :]

    ri = lax.broadcasted_iota(jnp.int32, (c, c), 0)
    ci = lax.broadcasted_iota(jnp.int32, (c, c), 1)
    incl = ri >= ci
    strict = ri > ci
    tri = jnp.where(incl, 1.0, 0.0)
    eye = jnp.where(ri == ci, 1.0, 0.0)
    heads = range(H_C)
    n_chunks = tc // c

    q, k, v, beta, e_g, decay, g_end, k_dec = ([] for _ in range(8))
    for ic in range(n_chunks):
        rows = slice(ic * c, (ic + 1) * c)
        gbc = gb_ref[0, rows, :]
        g_all = jnp.dot(tri, gbc, preferred_element_type=F32, precision=HIGHEST)
        g_t = g_all.T
        for h in heads:
            g_col = g_all[:, H_C + h:H_C + h + 1]
            g_row = g_t[H_C + h:H_C + h + 1, :]
            g_last = g_all[c - 1:c, H_C + h:H_C + h + 1]
            kn = _l2n(xc_s[rows, nk + h * dk:nk + (h + 1) * dk])
            q.append(_l2n(xc_s[rows, h * dk:(h + 1) * dk]) * (dk ** -0.5))
            k.append(kn)
            v.append(xc_s[rows, 2 * nk + h * dk:2 * nk + (h + 1) * dk])
            beta.append(gbc[:, h:h + 1])
            e_g.append(jnp.exp(g_col))
            decay.append(jnp.where(incl, jnp.exp(jnp.where(incl, g_col - g_row, 0.0)), 0.0))
            g_end.append(jnp.exp(g_last))
            k_dec.append(kn * jnp.exp(g_last - g_col))
    pairs = range(n_chunks * H_C)
    kb = [k[i] * beta[i] for i in pairs]
    kk = [_bdot_nt(jnp.concatenate([kb[i], q[i]], axis=0), k[i]) for i in pairs]
    qk = [kk[i][c:] * decay[i] for i in pairs]
    pw = [-jnp.where(strict, kk[i][:c] * decay[i], 0.0) for i in pairs]
    inv = [eye + pw[i] for i in pairs]
    for _ in range(int(math.log2(c)) - 1):
        pw = [_bdot(pw[i], pw[i]) for i in pairs]
        inv = [inv[i] + _bdot(inv[i], pw[i]) for i in pairs]
    sol = [_bdot(inv[i], jnp.concatenate([v[i] * beta[i], kb[i] * e_g[i]], axis=1)) for i in pairs]
    wq = [jnp.concatenate([sol[i][:, dk:], q[i] * e_g[i]], axis=0).astype(BF16) for i in pairs]

    s = [s_ref[0, h] for h in heads]
    for ic in range(n_chunks):
        rows = slice(ic * c, (ic + 1) * c)
        idx = [ic * H_C + h for h in heads]
        ws = [_bdot(wq[i], s[h]) for h, i in zip(heads, idx)]
        v_new = [sol[i][:, :dk] - ws[h][:c] for h, i in zip(heads, idx)]
        o = [ws[h][c:] + _bdot(qk[i], v_new[h]) for h, i in zip(heads, idx)]
        s = [s[h] * g_end[i] + _bdot_tn(k_dec[i], v_new[h]) for h, i in zip(heads, idx)]
        zt = z_ref[0, rows, :].astype(F32)
        o_ref[0, rows, :] = jnp.concatenate(
            [_rms(o[h], og_ref[...]) * _silu(zt[:, h * dk:(h + 1) * dk]) for h in heads], axis=1).astype(BF16)
    for h in heads:
        s_ref[0, h] = s[h]


def _delta_prompt(qkv, z, gb, cw, og, *, tc):
    b, t, nq = qkv.shape
    nz = z.shape[2]
    dk = nz // H_C
    seq = lambda i, j: (i, j, 0)
    return pl.pallas_call(
        functools.partial(_delta_prompt_body, tc=tc, dk=dk),
        grid=(b, t // tc),
        in_specs=[pl.BlockSpec((1, tc, nq), seq), pl.BlockSpec((1, tc, nz), seq),
                  pl.BlockSpec((1, tc, LANES), seq), _full((CONV_W, nq)), _full((1, dk))],
        out_specs=[pl.BlockSpec((1, tc, nz), seq),
                   pl.BlockSpec((1, H_C, dk, dk), lambda i, j: (i, 0, 0, 0))],
        out_shape=[jax.ShapeDtypeStruct((b, t, nz), BF16), jax.ShapeDtypeStruct((b, H_C, dk, dk), F32)],
        scratch_shapes=[pltpu.VMEM((SUBLANES, nq), F32), pltpu.VMEM((tc, nq), F32)],
        compiler_params=_cparams("parallel", "arbitrary"),
        name="delta_prompt",
    )(qkv, z, gb, cw, og.reshape(1, dk))


def _delta_prep_step_body(x_ref, b0_ref, b1_ref, b2_ref, cw_ref, q_ref, k_ref, v_ref, *, dk):
    xc = _silu(b0_ref[...] * cw_ref[0:1, :] + b1_ref[...] * cw_ref[1:2, :] + b2_ref[...] * cw_ref[2:3, :]
               + x_ref[...] * cw_ref[3:4, :])
    nk = H_C * dk
    for h in range(H_C):
        sl = slice(h * dk, (h + 1) * dk)
        q_ref[:, sl] = _l2n(xc[:, h * dk:(h + 1) * dk]) * (dk ** -0.5)
        k_ref[:, sl] = _l2n(xc[:, nk + h * dk:nk + (h + 1) * dk])
    v_ref[...] = xc[:, 2 * nk:]


def _delta_step_body(q_ref, k_ref, v_ref, gb_ref, z_ref, og_ref, s_ref, o_ref, sn_ref, *, dk):
    row = lax.broadcasted_iota(jnp.int32, (SUBLANES, dk), 0)
    heads = range(H_C)
    rows8 = lambda x: jnp.broadcast_to(x, (SUBLANES, dk))
    q = [q_ref[0, :, h * dk:(h + 1) * dk] for h in heads]
    k = [k_ref[0, :, h * dk:(h + 1) * dk] for h in heads]
    v = [v_ref[0, :, h * dk:(h + 1) * dk] for h in heads]
    beta = [gb_ref[0, :, h:h + 1] for h in heads]
    e_g = [jnp.exp(gb_ref[0, :, H_C + h:H_C + h + 1]) for h in heads]
    s = [s_ref[0, h] for h in heads]
    kq = [jnp.where(row == 0, rows8(k[h]), jnp.where(row == 1, rows8(q[h] * e_g[h]), 0.0)) for h in heads]
    ks = [_bdot(kq[h], s[h]) for h in heads]
    v_new = [beta[h] * (v[h] - e_g[h] * ks[h][0:1]) for h in heads]
    o = [ks[h][1:2] + jnp.sum(q[h] * k[h], axis=-1, keepdims=True) * v_new[h] for h in heads]
    outer = [lax.dot_general(jnp.where(row == 0, rows8(k[h]), 0.0), jnp.where(row == 0, rows8(v_new[h]), 0.0),
                             (((0,), (0,)), ((), ())), preferred_element_type=F32, precision=HIGHEST)
             for h in heads]
    for h in heads:
        sn_ref[0, h] = s[h] * e_g[h] + outer[h]
    zt = z_ref[0].astype(F32)
    o_ref[0] = jnp.concatenate(
        [_rms(o[h], og_ref[...]) * _silu(zt[:, h * dk:(h + 1) * dk]) for h in heads], axis=1).astype(BF16)


def _delta_step(qkv, buf, z, gb, s0, cw, og):
    n, nq = qkv.shape
    nz = z.shape[1]
    dk = nz // H_C
    args = (qkv, buf[:, 0], buf[:, 1], buf[:, 2], cw)
    q, k, v = pl.pallas_call(
        functools.partial(_delta_prep_step_body, dk=dk),
        grid=(1,),
        in_specs=[_full(a.shape) for a in args],
        out_specs=[_full((n, nz))] * 3,
        out_shape=[jax.ShapeDtypeStruct((n, nz), F32)] * 3,
        compiler_params=_cparams("arbitrary"),
        name="delta_prep_step",
    )(*args)
    tok = lambda width: pl.BlockSpec((1, 1, width), lambda i: (i, 0, 0))
    st = pl.BlockSpec((1, H_C, dk, dk), lambda i: (i, 0, 0, 0))
    o, s_new = pl.pallas_call(
        functools.partial(_delta_step_body, dk=dk),
        grid=(n,),
        in_specs=[tok(nz), tok(nz), tok(nz), tok(LANES), tok(nz), _full((1, dk)), st],
        out_specs=[tok(nz), st],
        out_shape=[jax.ShapeDtypeStruct((n, 1, nz), BF16), jax.ShapeDtypeStruct(s0.shape, F32)],
        compiler_params=_cparams("parallel"),
        name="delta_step",
    )(q.reshape(n, 1, nz), k.reshape(n, 1, nz), v.reshape(n, 1, nz), gb.reshape(n, 1, LANES),
      z.reshape(n, 1, nz), og.reshape(1, dk), s0)
    return o.reshape(n, nz), s_new


def _router_body(x_ref, ln_ref, rw_ref, hb_ref, comb_ref, rank_ref, rank_t_ref, cnt_ref, *, n_exp):
    h = _rms(x_ref[...], ln_ref[...])
    hb_ref[...] = h.astype(BF16)
    logits = jnp.dot(h, rw_ref[...], preferred_element_type=F32, precision=HIGHEST)
    tm = logits.shape[0]
    lane = lax.broadcasted_iota(jnp.int32, logits.shape, 1).astype(F32)
    logits = jnp.where(lane < n_exp, logits, -jnp.inf)
    m1 = jnp.max(logits, axis=-1, keepdims=True)
    i1 = jnp.min(jnp.where(logits == m1, lane, float(LANES)), axis=-1, keepdims=True)
    rest = jnp.where(lane == i1, -jnp.inf, logits)
    m2 = jnp.max(rest, axis=-1, keepdims=True)
    i2 = jnp.min(jnp.where(rest == m2, lane, float(LANES)), axis=-1, keepdims=True)
    e = jnp.exp(m2 - m1)
    g1 = 1.0 / (1.0 + e)
    g2 = e / (1.0 + e)
    comb = jnp.where(lane == i1, g1, 0.0) + jnp.where(lane == i2, g2, 0.0)
    comb_ref[...] = comb
    sel = comb != 0.0
    ri = lax.broadcasted_iota(jnp.int32, (tm, tm), 0)
    ci = lax.broadcasted_iota(jnp.int32, (tm, tm), 1)
    before = jnp.where(ri > ci, 1.0, 0.0).astype(BF16)
    self_f = jnp.where(sel, 1.0, 0.0)
    rank = jnp.where(sel, jnp.dot(before, self_f.astype(BF16), preferred_element_type=F32), -1.0)
    rank_ref[...] = rank
    rank_t_ref[...] = rank.T[:SUBLANES, :]
    cnt_ref[0] = jnp.sum(self_f, axis=0, keepdims=True).astype(jnp.int32)


def _router(x, ln, rw, *, tm):
    m, d = x.shape
    n_exp = rw.shape[1]
    assert n_exp <= SUBLANES
    rwp = jnp.zeros((d, LANES), F32).at[:, :n_exp].set(rw)
    row = lambda i: (i, 0)
    return pl.pallas_call(
        functools.partial(_router_body, n_exp=n_exp),
        grid=(m // tm,),
        in_specs=[pl.BlockSpec((tm, d), row), _full((1, d)), _full((d, LANES))],
        out_specs=[pl.BlockSpec((tm, d), row), pl.BlockSpec((tm, LANES), row), pl.BlockSpec((tm, LANES), row),
                   pl.BlockSpec((SUBLANES, tm), lambda i: (0, i)),
                   pl.BlockSpec((1, 1, LANES), lambda i: (i, 0, 0))],
        out_shape=[jax.ShapeDtypeStruct((m, d), BF16), jax.ShapeDtypeStruct((m, LANES), F32),
                   jax.ShapeDtypeStruct((m, LANES), F32), jax.ShapeDtypeStruct((SUBLANES, m), F32),
                   jax.ShapeDtypeStruct((m // tm, 1, LANES), jnp.int32)],
        compiler_params=_cparams("parallel"),
        name="router",
    )(x, ln.reshape(1, d), rwp)


def _moe_body(cnt_ref, x_ref, hb_ref, comb_ref, rank_ref, rank_t_ref, wg_ref, wu_ref, wd_ref, o_ref,
              xs_s, ys_s, *, tm):
    i, e, f = pl.program_id(0), pl.program_id(1), pl.program_id(2)
    n_f = pl.num_programs(2)
    count = cnt_ref[i, e]
    r = MOE_ROWS

    @pl.when(jnp.logical_and(e == 0, f == 0))
    def _():
        o_ref[...] = x_ref[...]

    def gather(lo, n):
        want = (lax.broadcasted_iota(jnp.int32, (n, tm), 0) + lo).astype(F32)
        onehot = jnp.where(rank_t_ref[pl.ds(e, 1), :] == want, 1.0, 0.0).astype(BF16)
        xs_s[lo:lo + n, :] = jnp.dot(onehot, hb_ref[...], preferred_element_type=F32).astype(BF16)

    def ffn(lo, n):
        xs = xs_s[lo:lo + n, :]
        g = jnp.dot(xs, wg_ref[0], preferred_element_type=F32)
        u = jnp.dot(xs, wu_ref[0], preferred_element_type=F32)
        y = jnp.dot((_silu(g) * u).astype(BF16), wd_ref[0], preferred_element_type=F32)

        @pl.when(f == 0)
        def _():
            ys_s[lo:lo + n, :] = y

        @pl.when(f > 0)
        def _():
            ys_s[lo:lo + n, :] += y

    def scatter(lo, n):
        lane = lax.broadcasted_iota(jnp.int32, (tm, LANES), 1)
        pick = lane == e
        rank_col = jnp.sum(jnp.where(pick, rank_ref[...], 0.0), axis=-1, keepdims=True)
        gate_col = jnp.sum(jnp.where(pick, comb_ref[...], 0.0), axis=-1, keepdims=True)
        want = (lax.broadcasted_iota(jnp.int32, (tm, n), 1) + lo).astype(F32)
        onehot = jnp.where(rank_col == want, 1.0, 0.0).astype(BF16)
        z = jnp.dot(onehot, ys_s[lo:lo + n, :].astype(BF16), preferred_element_type=F32)
        o_ref[...] += gate_col * z

    grp = min(2 * r, tm)
    for lo in range(0, tm, grp):
        if grp == 2 * r:
            @pl.when(count > lo + r)
            def _(lo=lo):
                @pl.when(f == 0)
                def _():
                    gather(lo, 2 * r)
                ffn(lo, 2 * r)

            @pl.when(jnp.logical_and(count > lo, count <= lo + r))
            def _(lo=lo):
                @pl.when(f == 0)
                def _():
                    gather(lo, r)
                    ys_s[lo + r:lo + 2 * r, :] = jnp.zeros((r, ys_s.shape[1]), F32)
                ffn(lo, r)
        else:
            @pl.when(count > lo)
            def _(lo=lo):
                @pl.when(f == 0)
                def _():
                    gather(lo, r)
                ffn(lo, r)

        @pl.when(jnp.logical_and(f == n_f - 1, count > lo))
        def _(lo=lo):
            scatter(lo, grp)


def _moe(x, hb, comb, rank, rank_t, cnt, wg, wu, wd, *, tm, tf):
    m, d = x.shape
    n_exp, _, f = wg.shape
    assert tm % MOE_ROWS == 0 and (tm <= MOE_ROWS or tm % (2 * MOE_ROWS) == 0)
    tok = lambda width: pl.BlockSpec((tm, width), lambda i, e, j, cnt: (i, 0))
    return pl.pallas_call(
        functools.partial(_moe_body, tm=tm),
        grid_spec=pltpu.PrefetchScalarGridSpec(
            num_scalar_prefetch=1,
            grid=(m // tm, n_exp, f // tf),
            in_specs=[tok(d), tok(d), tok(LANES), tok(LANES),
                      pl.BlockSpec((SUBLANES, tm), lambda i, e, j, cnt: (0, i)),
                      pl.BlockSpec((1, d, tf), lambda i, e, j, cnt: (e, 0, j)),
                      pl.BlockSpec((1, d, tf), lambda i, e, j, cnt: (e, 0, j)),
                      pl.BlockSpec((1, tf, d), lambda i, e, j, cnt: (e, j, 0))],
            out_specs=tok(d),
            scratch_shapes=[pltpu.VMEM((tm, d), BF16), pltpu.VMEM((tm, d), F32)],
        ),
        out_shape=jax.ShapeDtypeStruct((m, d), F32),
        compiler_params=_cparams("parallel", "arbitrary", "arbitrary"),
        name="moe",
    )(cnt, x, hb, comb, rank, rank_t, wg, wu, wd)


def _moe_layer(x, ln, rw, wg, wu, wd, *, tm, tf):
    hb, comb, rank, rank_t, cnt = _router(x, ln, rw, tm=tm)
    return _moe(x, hb, comb, rank, rank_t, cnt.reshape(cnt.shape[0], LANES), wg, wu, wd, tm=tm, tf=tf)


def _block_diag(w):
    n, c, d = w.shape
    eye = jnp.eye(n, dtype=w.dtype)
    return (eye[:, None, :, None] * w[:, :, None, :]).reshape(n * c, n * d)


def _tile_rows(m, pref):
    return pref if m % pref == 0 else m


def kernel(x_prompt, x_sample, cache_k, cache_v, state_lru_conv, state_lru_h, state_delta_conv, state_delta_s, page_table, ln_mix_e, w_in_e, conv_lru_w, conv_lru_b, lru_wx, lru_bx, lru_wa, lru_ba, lru_lambda, qk_gain_q, qk_gain_k, lam_q1, lam_k1, lam_q2, lam_k2, subln_g, w_out_e, ln_ffn_e, ffn_wg, ffn_wu, ffn_wd, ln_mix_o, w_in_o, conv_delta_w, a_log, dt_bias, onorm_g, w_out_o, ln_ffn_o, router_w, moe_wg, moe_wu, moe_wd):
    bp, tp, d = x_prompt.shape
    bs = x_sample.shape[0]
    n_even, n_odd = w_in_e.shape[0], w_in_o.shape[0]
    lru_w = conv_lru_w.shape[-1]
    att_w = cache_k.shape[-1] * cache_k.shape[-2]
    qkv_c = conv_delta_w.shape[-1]
    dk = onorm_g.shape[-1]
    assert x_sample.shape[1] == 1 and lru_w == att_w and w_in_e.shape[-1] == 5 * lru_w

    xp = x_prompt.reshape(bp * tp, d)
    xs = x_sample.reshape(bs, d)
    mp = bp * tp
    tmp = _tile_rows(mp, 512)
    outs = {k: [] for k in ("kp", "vp", "cp", "hp", "dcp", "dsp", "ks", "vs", "cs", "hs", "dcs", "dss")}
    bf = lambda a: a.astype(BF16)
    vec = lambda a: a.reshape(1, -1)

    for layer in range(n_even + n_odd):
        i = layer // 2
        if layer % 2 == 0:
            lam_init = 0.8 - 0.6 * math.exp(-0.3 * layer)
            w_in, w_out = bf(w_in_e[i]), bf(w_out_e[i])
            lru_args = (conv_lru_w[i], vec(conv_lru_b[i]), bf(_block_diag(lru_wx[i])), vec(lru_bx[i]),
                        bf(_block_diag(lru_wa[i])), vec(lru_ba[i]), vec(lru_lambda[i]))
            lam_args = (lam_q1[i], lam_k1[i], lam_q2[i], lam_k2[i], subln_g[i])
            wg, wu, wd = bf(ffn_wg[i]), bf(ffn_wu[i]), bf(ffn_wd[i])
            tf = wg.shape[1] // 2

            gg, xr, qn, k_rows, v_rows, kb, vb = _inproj_even(xp, ln_mix_e[i], w_in, qk_gain_q[i],
                                                              qk_gain_k[i], tm=tmp)
            seq = lambda a: a.reshape(bp, tp, -1)
            ya, h_last = _lru_prompt(seq(xr), seq(gg), *lru_args, tt=_tile_rows(tp, 512))
            o = _attn_prompt(seq(qn), seq(kb), seq(vb), *lam_args, lam_init=lam_init, tq=_tile_rows(tp, 256))
            xp = _outproj(xp, [(ya.reshape(mp, lru_w), w_out[:lru_w]), (o.reshape(mp, att_w), w_out[lru_w:])],
                          tm=tmp)
            xp = _ffn(xp, ln_ffn_e[i], wg, wu, wd, tm=tmp, tf=tf)
            outs["kp"].append(k_rows.reshape(bp, tp, H_B, att_w // H_B))
            outs["vp"].append(v_rows.reshape(bp, tp, H_B, att_w // H_B))
            outs["cp"].append(seq(xr)[:, tp - (CONV_W - 1):])
            outs["hp"].append(h_last.reshape(bp, lru_w))

            gg, xr, qn, k_rows, v_rows, _, _ = _inproj_even(xs, ln_mix_e[i], w_in, qk_gain_q[i],
                                                            qk_gain_k[i], tm=bs)
            ya, h_new = _lru_step(xr, gg, state_lru_conv[i], state_lru_h[i], *lru_args)
            o = _attn_decode(qn, k_rows, v_rows, cache_k[i], cache_v[i], page_table, *lam_args,
                             lam_init=lam_init)
            xs = _outproj(xs, [(ya, w_out[:lru_w]), (o, w_out[lru_w:])], tm=bs)
            xs = _ffn(xs, ln_ffn_e[i], wg, wu, wd, tm=bs, tf=tf)
            outs["ks"].append(k_rows.reshape(bs, 1, H_B, att_w // H_B))
            outs["vs"].append(v_rows.reshape(bs, 1, H_B, att_w // H_B))
            outs["cs"].append(jnp.concatenate([state_lru_conv[i][:, 1:], xr[:, None]], axis=1))
            outs["hs"].append(h_new)
        else:
            w_in, w_out = bf(w_in_o[i]), bf(w_out_o[i])
            wg, wu, wd = bf(moe_wg[i]), bf(moe_wu[i]), bf(moe_wd[i])
            tf = 896

            qkv, z, gb = _inproj_odd(xp, ln_mix_o[i], w_in, a_log[i], dt_bias[i], tm=tmp)
            seq = lambda a: a.reshape(bp, tp, -1)
            o, s_fin = _delta_prompt(seq(qkv), seq(z), seq(gb), conv_delta_w[i], onorm_g[i],
                                     tc=_tile_rows(tp, 256))
            xp = _outproj(xp, [(o.reshape(mp, -1), w_out)], tm=tmp)
            xp = _moe_layer(xp, ln_ffn_o[i], router_w[i], wg, wu, wd, tm=_tile_rows(mp, 1024), tf=tf)
            outs["dcp"].append(seq(qkv)[:, tp - (CONV_W - 1):])
            outs["dsp"].append(s_fin)

            qkv, z, gb = _inproj_odd(xs, ln_mix_o[i], w_in, a_log[i], dt_bias[i], tm=bs)
            o, s_new = _delta_step(qkv, state_delta_conv[i], z, gb, state_delta_s[i], conv_delta_w[i],
                                   onorm_g[i])
            xs = _outproj(xs, [(o, w_out)], tm=bs)
            xs = _moe_layer(xs, ln_ffn_o[i], router_w[i], wg, wu, wd, tm=bs, tf=tf)
            outs["dcs"].append(jnp.concatenate([state_delta_conv[i][:, 1:], qkv[:, None]], axis=1))
            outs["dss"].append(s_new)

    st = lambda k: jnp.stack(outs[k])
    return (xp.reshape(bp, tp, d), xs.reshape(bs, 1, d),
            st("kp"), st("vp"), st("cp"), st("hp"), st("dcp"), st("dsp"),
            st("ks"), st("vs"), st("cs"), st("hs"), st("dcs"), st("dss"))
```

```python
import functools
import math

import jax
import jax.numpy as jnp
from jax import lax
from jax.experimental import pallas as pl
from jax.experimental.pallas import tpu as pltpu

F32 = jnp.float32
BF16 = jnp.bfloat16
HIGHEST = lax.Precision.HIGHEST

CONV_W = 4
RG_C = 8.0
H_B = 4
H_C = 8
TOP_K = 2
EPS = 1e-6
NEG_INF = -1e30

LANES = 128
SUBLANES = 8
VMEM_LIMIT_BYTES = 52 * 1024 * 1024
MOE_VMEM_LIMIT_BYTES = 60 * 1024 * 1024

DELTA_CHUNK = 64
ATTN_HEADS_PER_STEP = 4
MOE_ROWS = 128
MOE_TOKENS = 2048
MOE_FF_TILE = 512
FFN_CHUNK = 512


def _cparams(*sem, vmem_limit_bytes=VMEM_LIMIT_BYTES):
    return pltpu.CompilerParams(dimension_semantics=sem, vmem_limit_bytes=vmem_limit_bytes)


def _full(shape):
    nd = len(shape)
    return pl.BlockSpec(shape, lambda *_: (0,) * nd)


def _rms(x, g):
    return x * lax.rsqrt(jnp.mean(x * x, axis=-1, keepdims=True) + EPS) * g


def _softplus(x):
    return jnp.maximum(x, 0.0) + jnp.log1p(jnp.exp(-jnp.abs(x)))


def _silu(x):
    return x * jax.nn.sigmoid(x)


def _bdot(a, b):
    return jnp.dot(a.astype(BF16), b.astype(BF16), preferred_element_type=F32)


def _bdot_nt(a, b):
    return lax.dot_general(a.astype(BF16), b.astype(BF16), (((1,), (1,)), ((), ())),
                           preferred_element_type=F32)


def _bdot_tn(a, b):
    return lax.dot_general(a.astype(BF16), b.astype(BF16), (((0,), (0,)), ((), ())),
                           preferred_element_type=F32)


def _shifted_rows(prev8, x):
    full = jnp.concatenate([prev8, x], axis=0)
    return [pltpu.roll(full, s, axis=0)[SUBLANES:] for s in (1, 2, 3)]


def _inproj_even_body(x_ref, ln_ref, w_ref, gq_ref, gk_ref, seg_ref,
                      gg_ref, xr_ref, q_ref, k_ref, v_ref, kb_ref, vb_ref, *, width, dh):
    hb = _rms(x_ref[...], ln_ref[...]).astype(BF16)

    def proj(j):
        return jnp.dot(hb, w_ref[:, j * width:(j + 1) * width], preferred_element_type=F32)

    def qk_norm(t, g):
        ms = _bdot(t * t, seg_ref[...]) * (1.0 / dh)
        return t * lax.rsqrt(ms + EPS) * g

    gg_ref[...] = jax.nn.gelu(proj(0)).astype(BF16)
    xr_ref[...] = proj(1)
    q_ref[...] = qk_norm(proj(2), gq_ref[...]).astype(BF16)
    k = qk_norm(proj(3), gk_ref[...])
    k_ref[...] = k
    kb_ref[...] = k.astype(BF16)
    v = proj(4)
    v_ref[...] = v
    vb_ref[...] = v.astype(BF16)


def _inproj_even(x, ln, w, gq, gk, *, tm):
    m, d = x.shape
    width = w.shape[1] // 5
    dh = gq.shape[-1]
    reps = width // dh
    seg = jnp.kron(jnp.eye(reps, dtype=F32), jnp.ones((dh, dh), F32)).astype(BF16)
    gq_t = jnp.tile(gq.reshape(1, dh), (1, reps))
    gk_t = jnp.tile(gk.reshape(1, dh), (1, reps))
    row = lambda i: (i, 0)
    blk = pl.BlockSpec((tm, width), row)
    f32o = jax.ShapeDtypeStruct((m, width), F32)
    b16o = jax.ShapeDtypeStruct((m, width), BF16)
    return pl.pallas_call(
        functools.partial(_inproj_even_body, width=width, dh=dh),
        grid=(m // tm,),
        in_specs=[pl.BlockSpec((tm, d), row), _full((1, d)), _full(w.shape),
                  _full((1, width)), _full((1, width)), _full((width, width))],
        out_specs=[blk] * 7,
        out_shape=[b16o, f32o, b16o, f32o, f32o, b16o, b16o],
        compiler_params=_cparams("parallel"),
        name="inproj_even",
    )(x, ln.reshape(1, d), w, gq_t, gk_t, seg)


def _lru_coeffs(xc, wx_ref, bx_ref, wa_ref, ba_ref, lam_ref):
    xb = xc.astype(BF16)
    gate_x = jax.nn.sigmoid(jnp.dot(xb, wx_ref[...], preferred_element_type=F32) + bx_ref[...])
    gate_a = jax.nn.sigmoid(jnp.dot(xb, wa_ref[...], preferred_element_type=F32) + ba_ref[...])
    log_a = -RG_C * gate_a * _softplus(-lam_ref[...])
    a = jnp.exp(log_a)
    th = jnp.tanh(log_a)
    one_minus_a2 = -2.0 * th / (1.0 - th)
    b = jnp.sqrt(one_minus_a2) * (gate_x * xc)
    return a, b


def _lru_prompt_body(xr_ref, gg_ref, cw_ref, cb_ref, wx_ref, bx_ref, wa_ref, ba_ref, lam_ref,
                     ya_ref, hlast_ref, prev_s, h_s, a_s, b_s, *, tt):
    t = pl.program_id(1)

    @pl.when(t == 0)
    def _():
        prev_s[...] = jnp.zeros_like(prev_s)
        h_s[...] = jnp.zeros_like(h_s)

    xr = xr_ref[0]
    x1, x2, x3 = _shifted_rows(prev_s[...], xr)
    xc = (x3 * cw_ref[0:1, :] + x2 * cw_ref[1:2, :] + x1 * cw_ref[2:3, :] + xr * cw_ref[3:4, :]
          + cb_ref[...])
    prev_s[...] = xr[tt - SUBLANES:, :]
    a, b = _lru_coeffs(xc, wx_ref, bx_ref, wa_ref, ba_ref, lam_ref)
    a_s[...] = a
    b_s[...] = b

    def step(i, h):
        h = a_s[pl.ds(i, 1), :] * h + b_s[pl.ds(i, 1), :]
        b_s[pl.ds(i, 1), :] = h
        return h

    h = lax.fori_loop(0, tt, step, h_s[...], unroll=8)
    h_s[...] = h
    ya_ref[0] = (gg_ref[0].astype(F32) * b_s[...]).astype(BF16)
    hlast_ref[0] = h


def _lru_prompt(xr, gg, cw, cb, wx, bx, wa, ba, lam, *, tt):
    b, t, w = xr.shape
    seq = lambda i, j: (i, j, 0)
    vec = _full((1, w))
    return pl.pallas_call(
        functools.partial(_lru_prompt_body, tt=tt),
        grid=(b, t // tt),
        in_specs=[pl.BlockSpec((1, tt, w), seq), pl.BlockSpec((1, tt, w), seq), _full((CONV_W, w)), vec,
                  _full((w, w)), vec, _full((w, w)), vec, vec],
        out_specs=[pl.BlockSpec((1, tt, w), seq), pl.BlockSpec((1, 1, w), lambda i, j: (i, 0, 0))],
        out_shape=[jax.ShapeDtypeStruct((b, t, w), BF16), jax.ShapeDtypeStruct((b, 1, w), F32)],
        scratch_shapes=[pltpu.VMEM((SUBLANES, w), F32), pltpu.VMEM((1, w), F32),
                        pltpu.VMEM((tt, w), F32), pltpu.VMEM((tt, w), F32)],
        compiler_params=_cparams("parallel", "arbitrary"),
        name="lru_prompt",
    )(xr, gg, cw, cb, wx, bx, wa, ba, lam)


def _lru_step_body(xr_ref, gg_ref, b0_ref, b1_ref, b2_ref, h0_ref, cw_ref, cb_ref,
                   wx_ref, bx_ref, wa_ref, ba_ref, lam_ref, ya_ref, h_ref):
    xr = xr_ref[...]
    xc = (b0_ref[...] * cw_ref[0:1, :] + b1_ref[...] * cw_ref[1:2, :] + b2_ref[...] * cw_ref[2:3, :]
          + xr * cw_ref[3:4, :] + cb_ref[...])
    a, b = _lru_coeffs(xc, wx_ref, bx_ref, wa_ref, ba_ref, lam_ref)
    h = a * h0_ref[...] + b
    h_ref[...] = h
    ya_ref[...] = (gg_ref[...].astype(F32) * h).astype(BF16)


def _lru_step(xr, gg, buf, h0, cw, cb, wx, bx, wa, ba, lam):
    n, w = xr.shape
    args = (xr, gg, buf[:, 0], buf[:, 1], buf[:, 2], h0, cw, cb, wx, bx, wa, ba, lam)
    return pl.pallas_call(
        _lru_step_body,
        grid=(1,),
        in_specs=[_full(a.shape) for a in args],
        out_specs=[_full((n, w)), _full((n, w))],
        out_shape=[jax.ShapeDtypeStruct((n, w), BF16), jax.ShapeDtypeStruct((n, w), F32)],
        compiler_params=_cparams("arbitrary"),
        name="lru_step",
    )(*args)


def _diff_lambda(lq1_ref, lk1_ref, lq2_ref, lk2_ref, lam_init):
    s1 = jnp.sum(lq1_ref[...] * lk1_ref[...], axis=-1, keepdims=True)
    s2 = jnp.sum(lq2_ref[...] * lk2_ref[...], axis=-1, keepdims=True)
    return jnp.exp(s1) - jnp.exp(s2) + lam_init


def _attn_prompt_body(lq1_ref, lk1_ref, lq2_ref, lk2_ref, sg_ref, q_ref, k_ref, v_ref, o_ref,
                      *, tq, dh, nh, lam_init):
    i = pl.program_id(2)
    hw = 2 * dh
    heads = range(nh)
    lam = _diff_lambda(lq1_ref, lk1_ref, lq2_ref, lk2_ref, lam_init)
    lane = lax.broadcasted_iota(jnp.int32, (tq, hw), 1)
    qq = []
    for h in heads:
        q = q_ref[0, :, h * hw:(h + 1) * hw] * (dh ** -0.5)
        zero = jnp.zeros_like(q)
        qq.append(jnp.concatenate([jnp.where(lane < dh, q, zero), jnp.where(lane >= dh, q, zero)], axis=0))
    rowi = lax.broadcasted_iota(jnp.int32, (2 * tq, tq), 0)
    coli = lax.broadcasted_iota(jnp.int32, (2 * tq, tq), 1)
    causal = coli <= jnp.where(rowi >= tq, rowi - tq, rowi)

    def block(j, carry, masked):
        start = pl.multiple_of(j * tq, tq)
        kj = k_ref[0, pl.ds(start, tq), :]
        vj = v_ref[0, pl.ds(start, tq), :]
        s = [_bdot_nt(qq[h], kj[:, h * hw:(h + 1) * hw]) for h in heads]
        if masked:
            s = [jnp.where(causal, s[h], NEG_INF) for h in heads]
        m_new = [jnp.maximum(carry[h][0], jnp.max(s[h], axis=-1, keepdims=True)) for h in heads]
        alpha = [jnp.exp(carry[h][0] - m_new[h]) for h in heads]
        p = [jnp.exp(s[h] - m_new[h]) for h in heads]
        l = [alpha[h] * carry[h][1] + jnp.sum(p[h], axis=-1, keepdims=True) for h in heads]
        acc = [alpha[h] * carry[h][2] + _bdot(p[h], vj[:, h * hw:(h + 1) * hw]) for h in heads]
        return tuple((m_new[h], l[h], acc[h]) for h in heads)

    init = tuple((jnp.full((2 * tq, 1), NEG_INF, F32), jnp.zeros((2 * tq, 1), F32),
                  jnp.zeros((2 * tq, hw), F32)) for _ in heads)
    carry = lax.fori_loop(0, i, lambda j, c: block(j, c, False), init)
    carry = block(i, carry, True)
    out = []
    for h in heads:
        _, l, acc = carry[h]
        o = acc / l
        od = o[:tq] - lam * o[tq:]
        out.append(_rms(od, sg_ref[...]) * (1.0 - lam_init))
    o_ref[0] = jnp.concatenate(out, axis=1).astype(BF16)


def _attn_prompt(q, k, v, lq1, lk1, lq2, lk2, sub_g, *, lam_init, tq):
    b, t, w = q.shape
    hw = w // H_B
    dh = hw // 2
    vec = _full((1, dh))
    nh = ATTN_HEADS_PER_STEP
    qblk = pl.BlockSpec((1, tq, nh * hw), lambda bi, hg, i: (bi, i, hg))
    kvblk = pl.BlockSpec((1, t, nh * hw), lambda bi, hg, i: (bi, 0, hg))
    return pl.pallas_call(
        functools.partial(_attn_prompt_body, tq=tq, dh=dh, nh=nh, lam_init=lam_init),
        grid=(b, H_B // nh, t // tq),
        in_specs=[vec, vec, vec, vec, _full((1, hw)), qblk, kvblk, kvblk],
        out_specs=qblk,
        out_shape=jax.ShapeDtypeStruct((b, t, w), BF16),
        compiler_params=_cparams("parallel", "parallel", "arbitrary"),
        name="attn_prompt",
    )(lq1.reshape(1, dh), lk1.reshape(1, dh), lq2.reshape(1, dh), lk2.reshape(1, dh),
      sub_g.reshape(1, hw), q, k, v)


def _attn_decode_body(pt_ref, lq1_ref, lk1_ref, lq2_ref, lk2_ref, sg_ref, q_ref, kn_ref, vn_ref, *refs,
                      n_pages, dh, lam_init):
    del pt_ref
    k_refs = refs[:n_pages]
    v_refs = refs[n_pages:2 * n_pages]
    o_ref = refs[2 * n_pages]
    hw = 2 * dh
    rows = 2 * H_B
    lam = _diff_lambda(lq1_ref, lk1_ref, lq2_ref, lk2_ref, lam_init)

    row = lax.broadcasted_iota(jnp.int32, (rows, hw), 0)
    lane = lax.broadcasted_iota(jnp.int32, (rows, hw), 1)

    def by_head(vec_ref):
        out = jnp.zeros((rows, hw), F32)
        for h in range(H_B):
            piece = vec_ref[0, :, h * hw:(h + 1) * hw].astype(F32)
            out = jnp.where(row // 2 == h, jnp.broadcast_to(piece, (rows, hw)), out)
        return out

    half = (lane >= dh) == (row % 2 == 1)
    wq = jnp.where(half, by_head(q_ref), 0.0) * (dh ** -0.5)
    kn = by_head(kn_ref)
    vn = by_head(vn_ref)
    s_new = jnp.sum(wq * kn, axis=-1, keepdims=True)

    n_rows = k_refs[0].shape[1]
    col_head = lax.broadcasted_iota(jnp.int32, (rows, n_rows), 1) % H_B
    own = col_head == lax.broadcasted_iota(jnp.int32, (rows, n_rows), 0) // 2
    wq_b = wq.astype(BF16)
    scores = [jnp.where(own, _bdot_nt(wq_b, k_refs[p][0]), NEG_INF) for p in range(n_pages)]
    m = s_new
    for s in scores:
        m = jnp.maximum(m, jnp.max(s, axis=-1, keepdims=True))
    p_new = jnp.exp(s_new - m)
    l = p_new
    acc = p_new * vn
    for p in range(n_pages):
        pr = jnp.exp(scores[p] - m)
        l = l + jnp.sum(pr, axis=-1, keepdims=True)
        acc = acc + _bdot(pr, v_refs[p][0])
    o = acc / l
    r_out = lax.broadcasted_iota(jnp.int32, (SUBLANES, rows), 0)
    r_in = lax.broadcasted_iota(jnp.int32, (SUBLANES, rows), 1)
    sel = (jnp.where(r_in == 2 * r_out, 1.0, 0.0) - lam * jnp.where(r_in == 2 * r_out + 1, 1.0, 0.0))
    od = jnp.dot(sel, o, preferred_element_type=F32, precision=HIGHEST)[:H_B]
    o_ref[0] = _rms(od, sg_ref[...]) * (1.0 - lam_init)


def _attn_decode(q, k_new, v_new, cache_k, cache_v, page_table, lq1, lk1, lq2, lk2, sub_g, *, lam_init):
    n, w = q.shape
    hw = w // H_B
    dh = hw // 2
    n_pages = page_table.shape[1]
    n_pool, page = cache_k.shape[0], cache_k.shape[1]
    ck = cache_k.reshape(n_pool, page * H_B, hw)
    cv = cache_v.reshape(n_pool, page * H_B, hw)
    vec = pl.BlockSpec((1, dh), lambda i, pt: (0, 0))
    tok = pl.BlockSpec((1, 1, w), lambda i, pt: (i, 0, 0))

    def page_spec(p):
        return pl.BlockSpec((1, page * H_B, hw), lambda i, pt: (pt[i, p], 0, 0))

    pages = [page_spec(p) for p in range(n_pages)]
    out = pl.pallas_call(
        functools.partial(_attn_decode_body, n_pages=n_pages, dh=dh, lam_init=lam_init),
        grid_spec=pltpu.PrefetchScalarGridSpec(
            num_scalar_prefetch=1,
            grid=(n,),
            in_specs=[vec, vec, vec, vec, pl.BlockSpec((1, hw), lambda i, pt: (0, 0)), tok, tok, tok]
            + pages + pages,
            out_specs=pl.BlockSpec((1, H_B, hw), lambda i, pt: (i, 0, 0)),
        ),
        out_shape=jax.ShapeDtypeStruct((n, H_B, hw), F32),
        compiler_params=_cparams("arbitrary"),
        name="attn_decode",
    )(page_table, lq1.reshape(1, dh), lk1.reshape(1, dh), lq2.reshape(1, dh), lk2.reshape(1, dh),
      sub_g.reshape(1, hw), q.reshape(n, 1, w), k_new.reshape(n, 1, w), v_new.reshape(n, 1, w),
      *([ck] * n_pages), *([cv] * n_pages))
    return out.reshape(n, w)


def _outproj_body(*refs, n_in):
    x_ref = refs[0]
    o_ref = refs[1 + 2 * n_in]
    acc = x_ref[...]
    for j in range(n_in):
        acc = acc + jnp.dot(refs[1 + 2 * j][...].astype(BF16), refs[2 + 2 * j][...],
                            preferred_element_type=F32)
    o_ref[...] = acc


def _outproj(x, pairs, *, tm):
    m, d = x.shape
    row = lambda i: (i, 0)
    in_specs = [pl.BlockSpec((tm, d), row)]
    args = [x]
    for a, w in pairs:
        in_specs += [pl.BlockSpec((tm, a.shape[1]), row), _full(w.shape)]
        args += [a, w]
    return pl.pallas_call(
        functools.partial(_outproj_body, n_in=len(pairs)),
        grid=(m // tm,),
        in_specs=in_specs,
        out_specs=pl.BlockSpec((tm, d), row),
        out_shape=jax.ShapeDtypeStruct((m, d), F32),
        compiler_params=_cparams("parallel"),
        name="outproj",
    )(*args)


def _resident(shape):
    nd = len(shape)
    return pl.BlockSpec(shape, lambda *_: (0,) * nd, pipeline_mode=pl.Buffered(1))


def _ffn_body(x_ref, ln_ref, wg_ref, wu_ref, wd_ref, o_ref, *, chunks):
    x = x_ref[...]
    hb = _rms(x, ln_ref[...]).astype(BF16)
    acc = x
    for lo, n in chunks:
        g = jnp.dot(hb, wg_ref[:, lo:lo + n], preferred_element_type=F32)
        u = jnp.dot(hb, wu_ref[:, lo:lo + n], preferred_element_type=F32)
        acc = acc + jnp.dot((_silu(g) * u).astype(BF16), wd_ref[lo:lo + n, :], preferred_element_type=F32)
    o_ref[...] = acc


def _ffn(x, ln, wg, wu, wd, *, tm):
    m, d = x.shape
    f = wg.shape[1]
    chunks = [(lo, min(FFN_CHUNK, f - lo)) for lo in range(0, f, FFN_CHUNK)]
    return pl.pallas_call(
        functools.partial(_ffn_body, chunks=chunks),
        grid=(m // tm,),
        in_specs=[pl.BlockSpec((tm, d), lambda i: (i, 0)), _full((1, d)),
                  _resident(wg.shape), _resident(wu.shape), _resident(wd.shape)],
        out_specs=pl.BlockSpec((tm, d), lambda i: (i, 0)),
        out_shape=jax.ShapeDtypeStruct((m, d), F32),
        compiler_params=_cparams("parallel"),
        name="ffn",
    )(x, ln.reshape(1, d), wg, wu, wd)


def _inproj_odd_body(x_ref, ln_ref, w_ref, wba_ref, alog_ref, dtb_ref, qkv_ref, z_ref, gb_ref,
                     *, n_qkv, n_z, tn):
    hb = _rms(x_ref[...], ln_ref[...]).astype(BF16)
    for c in range(n_qkv // tn):
        qkv_ref[:, c * tn:(c + 1) * tn] = jnp.dot(hb, w_ref[:, c * tn:(c + 1) * tn],
                                                  preferred_element_type=F32)
    for c in range(n_z // tn):
        z_ref[:, c * tn:(c + 1) * tn] = jnp.dot(
            hb, w_ref[:, n_qkv + c * tn:n_qkv + (c + 1) * tn], preferred_element_type=F32).astype(BF16)
    ba = jnp.dot(hb, wba_ref[...], preferred_element_type=F32)
    lane = lax.broadcasted_iota(jnp.int32, ba.shape, 1)
    beta = jax.nn.sigmoid(ba)
    g = -jnp.exp(alog_ref[...]) * _softplus(ba + dtb_ref[...])
    gb_ref[...] = jnp.where(lane < H_C, beta, g)


def _inproj_odd(x, ln, w, a_log, dt_bias, *, tm):
    m, d = x.shape
    n_z = H_C * (d // H_C)
    n_qkv = w.shape[1] - n_z - 2 * H_C
    wba = jnp.zeros((d, LANES), BF16).at[:, :2 * H_C].set(w[:, n_qkv + n_z:])
    alog = jnp.zeros((1, LANES), F32).at[0, H_C:2 * H_C].set(a_log)
    dtb = jnp.zeros((1, LANES), F32).at[0, H_C:2 * H_C].set(dt_bias)
    row = lambda i: (i, 0)
    return pl.pallas_call(
        functools.partial(_inproj_odd_body, n_qkv=n_qkv, n_z=n_z, tn=512),
        grid=(m // tm,),
        in_specs=[pl.BlockSpec((tm, d), row), _full((1, d)), _full(w.shape), _full(wba.shape),
                  _full((1, LANES)), _full((1, LANES))],
        out_specs=[pl.BlockSpec((tm, n_qkv), row), pl.BlockSpec((tm, n_z), row),
                   pl.BlockSpec((tm, LANES), row)],
        out_shape=[jax.ShapeDtypeStruct((m, n_qkv), F32), jax.ShapeDtypeStruct((m, n_z), BF16),
                   jax.ShapeDtypeStruct((m, LANES), F32)],
        compiler_params=_cparams("parallel"),
        name="inproj_odd",
    )(x, ln.reshape(1, d), w, wba, alog, dtb)


def _l2n(x):
    return x * lax.rsqrt(jnp.sum(x * x, axis=-1, keepdims=True) + EPS)


def _delta_prompt_body(qkv_ref, z_ref, gb_ref, cw_ref, og_ref, o_ref, s_ref, prev_s, xc_s, *, tc, dk):
    t = pl.program_id(1)
    c = DELTA_CHUNK
    nk = H_C * dk

    @pl.when(t == 0)
    def _():
        prev_s[...] = jnp.zeros_like(prev_s)
        s_ref[...] = jnp.zeros_like(s_ref)

    x = qkv_ref[0]
    x1, x2, x3 = _shifted_rows(prev_s[...], x)
    xc_s[...] = _silu(x3 * cw_ref[0:1, :] + x2 * cw_ref[1:2, :] + x1 * cw_ref[2:3, :] + x * cw_ref[3:4, :])
    prev_s[...] = x[tc - SUBLANES:, :]

    ri = lax.broadcasted_iota(jnp.int32, (c, c), 0)
    ci = lax.broadcasted_iota(jnp.int32, (c, c), 1)
    incl = ri >= ci
    strict = ri > ci
    tri = jnp.where(incl, 1.0, 0.0)
    eye = jnp.where(ri == ci, 1.0, 0.0)
    heads = range(H_C)
    n_chunks = tc // c

    q, k, v, beta, e_g, decay, g_end, k_dec = ([] for _ in range(8))
    for ic in range(n_chunks):
        rows = slice(ic * c, (ic + 1) * c)
        gbc = gb_ref[0, rows, :]
        g_all = jnp.dot(tri, gbc, preferred_element_type=F32, precision=HIGHEST)
        g_t = g_all.T
        for h in heads:
            g_col = g_all[:, H_C + h:H_C + h + 1]
            g_row = g_t[H_C + h:H_C + h + 1, :]
            g_last = g_all[c - 1:c, H_C + h:H_C + h + 1]
            kn = _l2n(xc_s[rows, nk + h * dk:nk + (h + 1) * dk])
            q.append(_l2n(xc_s[rows, h * dk:(h + 1) * dk]) * (dk ** -0.5))
            k.append(kn)
            v.append(xc_s[rows, 2 * nk + h * dk:2 * nk + (h + 1) * dk])
            beta.append(gbc[:, h:h + 1])
            e_g.append(jnp.exp(g_col))
            decay.append(jnp.where(incl, jnp.exp(jnp.where(incl, g_col - g_row, 0.0)), 0.0))
            g_end.append(jnp.exp(g_last))
            k_dec.append(kn * jnp.exp(g_last - g_col))
    pairs = range(n_chunks * H_C)
    kb = [k[i] * beta[i] for i in pairs]
    kk = [_bdot_nt(jnp.concatenate([kb[i], q[i]], axis=0), k[i]) for i in pairs]
    qk = [kk[i][c:] * decay[i] for i in pairs]
    pw = [-jnp.where(strict, kk[i][:c] * decay[i], 0.0) for i in pairs]
    inv = [eye + pw[i] for i in pairs]
    for _ in range(int(math.log2(c)) - 1):
        pw = [_bdot(pw[i], pw[i]) for i in pairs]
        inv = [inv[i] + _bdot(inv[i], pw[i]) for i in pairs]
    sol = [_bdot(inv[i], jnp.concatenate([v[i] * beta[i], kb[i] * e_g[i]], axis=1)) for i in pairs]
    wq = [jnp.concatenate([sol[i][:, dk:], q[i] * e_g[i]], axis=0).astype(BF16) for i in pairs]

    s = [s_ref[0, h] for h in heads]
    for ic in range(n_chunks):
        rows = slice(ic * c, (ic + 1) * c)
        idx = [ic * H_C + h for h in heads]
        ws = [_bdot(wq[i], s[h]) for h, i in zip(heads, idx)]
        v_new = [sol[i][:, :dk] - ws[h][:c] for h, i in zip(heads, idx)]
        o = [ws[h][c:] + _bdot(qk[i], v_new[h]) for h, i in zip(heads, idx)]
        s = [s[h] * g_end[i] + _bdot_tn(k_dec[i], v_new[h]) for h, i in zip(heads, idx)]
        zt = z_ref[0, rows, :].astype(F32)
        o_ref[0, rows, :] = jnp.concatenate(
            [_rms(o[h], og_ref[...]) * _silu(zt[:, h * dk:(h + 1) * dk]) for h in heads], axis=1).astype(BF16)
    for h in heads:
        s_ref[0, h] = s[h]


def _delta_prompt(qkv, z, gb, cw, og, *, tc):
    b, t, nq = qkv.shape
    nz = z.shape[2]
    dk = nz // H_C
    seq = lambda i, j: (i, j, 0)
    return pl.pallas_call(
        functools.partial(_delta_prompt_body, tc=tc, dk=dk),
        grid=(b, t // tc),
        in_specs=[pl.BlockSpec((1, tc, nq), seq), pl.BlockSpec((1, tc, nz), seq),
                  pl.BlockSpec((1, tc, LANES), seq), _full((CONV_W, nq)), _full((1, dk))],
        out_specs=[pl.BlockSpec((1, tc, nz), seq),
                   pl.BlockSpec((1, H_C, dk, dk), lambda i, j: (i, 0, 0, 0))],
        out_shape=[jax.ShapeDtypeStruct((b, t, nz), BF16), jax.ShapeDtypeStruct((b, H_C, dk, dk), F32)],
        scratch_shapes=[pltpu.VMEM((SUBLANES, nq), F32), pltpu.VMEM((tc, nq), F32)],
        compiler_params=_cparams("parallel", "arbitrary"),
        name="delta_prompt",
    )(qkv, z, gb, cw, og.reshape(1, dk))


def _delta_prep_step_body(x_ref, b0_ref, b1_ref, b2_ref, cw_ref, q_ref, k_ref, v_ref, *, dk):
    xc = _silu(b0_ref[...] * cw_ref[0:1, :] + b1_ref[...] * cw_ref[1:2, :] + b2_ref[...] * cw_ref[2:3, :]
               + x_ref[...] * cw_ref[3:4, :])
    nk = H_C * dk
    for h in range(H_C):
        sl = slice(h * dk, (h + 1) * dk)
        q_ref[:, sl] = _l2n(xc[:, h * dk:(h + 1) * dk]) * (dk ** -0.5)
        k_ref[:, sl] = _l2n(xc[:, nk + h * dk:nk + (h + 1) * dk])
    v_ref[...] = xc[:, 2 * nk:]


def _delta_step_body(q_ref, k_ref, v_ref, gb_ref, z_ref, og_ref, s_ref, o_ref, sn_ref, *, dk):
    row = lax.broadcasted_iota(jnp.int32, (SUBLANES, dk), 0)
    heads = range(H_C)
    rows8 = lambda x: jnp.broadcast_to(x, (SUBLANES, dk))
    q = [q_ref[0, :, h * dk:(h + 1) * dk] for h in heads]
    k = [k_ref[0, :, h * dk:(h + 1) * dk] for h in heads]
    v = [v_ref[0, :, h * dk:(h + 1) * dk] for h in heads]
    beta = [gb_ref[0, :, h:h + 1] for h in heads]
    e_g = [jnp.exp(gb_ref[0, :, H_C + h:H_C + h + 1]) for h in heads]
    s = [s_ref[0, h] for h in heads]
    kq = [jnp.where(row == 0, rows8(k[h]), jnp.where(row == 1, rows8(q[h] * e_g[h]), 0.0)) for h in heads]
    ks = [_bdot(kq[h], s[h]) for h in heads]
    v_new = [beta[h] * (v[h] - e_g[h] * ks[h][0:1]) for h in heads]
    o = [ks[h][1:2] + jnp.sum(q[h] * k[h], axis=-1, keepdims=True) * v_new[h] for h in heads]
    outer = [lax.dot_general(jnp.where(row == 0, rows8(k[h]), 0.0), jnp.where(row == 0, rows8(v_new[h]), 0.0),
                             (((0,), (0,)), ((), ())), preferred_element_type=F32, precision=HIGHEST)
             for h in heads]
    for h in heads:
        sn_ref[0, h] = s[h] * e_g[h] + outer[h]
    zt = z_ref[0].astype(F32)
    o_ref[0] = jnp.concatenate(
        [_rms(o[h], og_ref[...]) * _silu(zt[:, h * dk:(h + 1) * dk]) for h in heads], axis=1).astype(BF16)


def _delta_step(qkv, buf, z, gb, s0, cw, og):
    n, nq = qkv.shape
    nz = z.shape[1]
    dk = nz // H_C
    args = (qkv, buf[:, 0], buf[:, 1], buf[:, 2], cw)
    q, k, v = pl.pallas_call(
        functools.partial(_delta_prep_step_body, dk=dk),
        grid=(1,),
        in_specs=[_full(a.shape) for a in args],
        out_specs=[_full((n, nz))] * 3,
        out_shape=[jax.ShapeDtypeStruct((n, nz), F32)] * 3,
        compiler_params=_cparams("arbitrary"),
        name="delta_prep_step",
    )(*args)
    tok = lambda width: pl.BlockSpec((1, 1, width), lambda i: (i, 0, 0))
    st = pl.BlockSpec((1, H_C, dk, dk), lambda i: (i, 0, 0, 0))
    o, s_new = pl.pallas_call(
        functools.partial(_delta_step_body, dk=dk),
        grid=(n,),
        in_specs=[tok(nz), tok(nz), tok(nz), tok(LANES), tok(nz), _full((1, dk)), st],
        out_specs=[tok(nz), st],
        out_shape=[jax.ShapeDtypeStruct((n, 1, nz), BF16), jax.ShapeDtypeStruct(s0.shape, F32)],
        compiler_params=_cparams("parallel"),
        name="delta_step",
    )(q.reshape(n, 1, nz), k.reshape(n, 1, nz), v.reshape(n, 1, nz), gb.reshape(n, 1, LANES),
      z.reshape(n, 1, nz), og.reshape(1, dk), s0)
    return o.reshape(n, nz), s_new


ROUTER_ROWS = 512
SCATTER_ROWS = 512


def _router_body(x_ref, ln_ref, rw_ref, hb_ref, comb_ref, rank_ref, rank_t_ref, cnt_ref, *, n_exp, tm):
    tb = min(ROUTER_ROWS, tm)
    lane = lax.broadcasted_iota(jnp.int32, (tb, LANES), 1).astype(F32)
    ri = lax.broadcasted_iota(jnp.int32, (tb, tb), 0)
    ci = lax.broadcasted_iota(jnp.int32, (tb, tb), 1)
    before = jnp.where(ri > ci, 1.0, 0.0).astype(BF16)
    running = jnp.zeros((1, LANES), F32)
    for t0 in range(0, tm, tb):
        rows = slice(t0, t0 + tb)
        h = _rms(x_ref[rows, :], ln_ref[...])
        hb_ref[rows, :] = h.astype(BF16)
        logits = jnp.dot(h, rw_ref[...], preferred_element_type=F32, precision=HIGHEST)
        logits = jnp.where(lane < n_exp, logits, -jnp.inf)
        m1 = jnp.max(logits, axis=-1, keepdims=True)
        i1 = jnp.min(jnp.where(logits == m1, lane, float(LANES)), axis=-1, keepdims=True)
        rest = jnp.where(lane == i1, -jnp.inf, logits)
        m2 = jnp.max(rest, axis=-1, keepdims=True)
        i2 = jnp.min(jnp.where(rest == m2, lane, float(LANES)), axis=-1, keepdims=True)
        e = jnp.exp(m2 - m1)
        g1 = 1.0 / (1.0 + e)
        g2 = e / (1.0 + e)
        comb = jnp.where(lane == i1, g1, 0.0) + jnp.where(lane == i2, g2, 0.0)
        comb_ref[rows, :] = comb
        sel = comb != 0.0
        self_f = jnp.where(sel, 1.0, 0.0)
        local = jnp.dot(before, self_f.astype(BF16), preferred_element_type=F32)
        rank = jnp.where(sel, local + running, -1.0)
        rank_ref[rows, :] = rank
        rank_t_ref[:, rows] = rank.T[:SUBLANES, :]
        running = running + jnp.sum(self_f, axis=0, keepdims=True)
    cnt_ref[0] = running.astype(jnp.int32)


def _router(x, ln, rw, *, tm):
    m, d = x.shape
    n_exp = rw.shape[1]
    assert n_exp <= SUBLANES
    rwp = jnp.zeros((d, LANES), F32).at[:, :n_exp].set(rw)
    row = lambda i: (i, 0)
    return pl.pallas_call(
        functools.partial(_router_body, n_exp=n_exp, tm=tm),
        grid=(m // tm,),
        in_specs=[pl.BlockSpec((tm, d), row), _full((1, d)), _full((d, LANES))],
        out_specs=[pl.BlockSpec((tm, d), row), pl.BlockSpec((tm, LANES), row), pl.BlockSpec((tm, LANES), row),
                   pl.BlockSpec((SUBLANES, tm), lambda i: (0, i)),
                   pl.BlockSpec((1, 1, LANES), lambda i: (i, 0, 0))],
        out_shape=[jax.ShapeDtypeStruct((m, d), BF16), jax.ShapeDtypeStruct((m, LANES), F32),
                   jax.ShapeDtypeStruct((m, LANES), F32), jax.ShapeDtypeStruct((SUBLANES, m), F32),
                   jax.ShapeDtypeStruct((m // tm, 1, LANES), jnp.int32)],
        compiler_params=_cparams("parallel"),
        name="router",
    )(x, ln.reshape(1, d), rwp)


def _moe_body(cnt_ref, x_hbm, hb_ref, comb_ref, rank_ref, rank_t_ref, wg_ref, wu_ref, wd_ref, o_ref,
              xs_s, ys_s, x_sem, *, tm):
    i, e, f = pl.program_id(0), pl.program_id(1), pl.program_id(2)
    n_f = pl.num_programs(2)
    count = cnt_ref[i, e]
    r = MOE_ROWS

    @pl.when(jnp.logical_and(e == 0, f == 0))
    def _():
        rows = pl.ds(pl.multiple_of(i * tm, tm), tm)
        copy = pltpu.make_async_copy(x_hbm.at[rows, :], o_ref, x_sem)
        copy.start()
        copy.wait()

    def gather(lo, n):
        want = (lax.broadcasted_iota(jnp.int32, (n, tm), 0) + lo).astype(F32)
        onehot = jnp.where(rank_t_ref[pl.ds(e, 1), :] == want, 1.0, 0.0).astype(BF16)
        xs_s[lo:lo + n, :] = jnp.dot(onehot, hb_ref[...], preferred_element_type=F32).astype(BF16)

    def ffn(lo, n):
        xs = xs_s[lo:lo + n, :]
        g = jnp.dot(xs, wg_ref[0], preferred_element_type=F32)
        u = jnp.dot(xs, wu_ref[0], preferred_element_type=F32)
        y = jnp.dot((_silu(g) * u).astype(BF16), wd_ref[0], preferred_element_type=F32)

        @pl.when(f == 0)
        def _():
            ys_s[lo:lo + n, :] = y

        @pl.when(f > 0)
        def _():
            ys_s[lo:lo + n, :] += y

    def scatter(lo, n):
        tb = min(SCATTER_ROWS, tm)
        pick = lax.broadcasted_iota(jnp.int32, (tb, LANES), 1) == e
        want = (lax.broadcasted_iota(jnp.int32, (tb, n), 1) + lo).astype(F32)
        yb = ys_s[lo:lo + n, :].astype(BF16)
        for t0 in range(0, tm, tb):
            rank_col = jnp.sum(jnp.where(pick, rank_ref[t0:t0 + tb, :], 0.0), axis=-1, keepdims=True)
            gate_col = jnp.sum(jnp.where(pick, comb_ref[t0:t0 + tb, :], 0.0), axis=-1, keepdims=True)
            onehot = jnp.where(rank_col == want, 1.0, 0.0).astype(BF16)
            o_ref[t0:t0 + tb, :] += gate_col * jnp.dot(onehot, yb, preferred_element_type=F32)

    grp = min(2 * r, tm)
    for lo in range(0, tm, grp):
        if grp == 2 * r:
            @pl.when(count > lo + r)
            def _(lo=lo):
                @pl.when(f == 0)
                def _():
                    gather(lo, 2 * r)
                ffn(lo, 2 * r)

            @pl.when(jnp.logical_and(count > lo, count <= lo + r))
            def _(lo=lo):
                @pl.when(f == 0)
                def _():
                    gather(lo, r)
                    ys_s[lo + r:lo + 2 * r, :] = jnp.zeros((r, ys_s.shape[1]), F32)
                ffn(lo, r)
        else:
            @pl.when(count > lo)
            def _(lo=lo):
                @pl.when(f == 0)
                def _():
                    gather(lo, r)
                ffn(lo, r)

        @pl.when(jnp.logical_and(f == n_f - 1, count > lo))
        def _(lo=lo):
            scatter(lo, grp)


def _moe(x, hb, comb, rank, rank_t, cnt, wg, wu, wd, *, tm, tf):
    m, d = x.shape
    n_exp, _, f = wg.shape
    assert tm % MOE_ROWS == 0 and (tm <= MOE_ROWS or tm % (2 * MOE_ROWS) == 0)
    tok = lambda width: pl.BlockSpec((tm, width), lambda i, e, j, cnt: (i, 0))
    return pl.pallas_call(
        functools.partial(_moe_body, tm=tm),
        grid_spec=pltpu.PrefetchScalarGridSpec(
            num_scalar_prefetch=1,
            grid=(m // tm, n_exp, f // tf),
            in_specs=[pl.BlockSpec(memory_space=pl.ANY), tok(d), tok(LANES), tok(LANES),
                      pl.BlockSpec((SUBLANES, tm), lambda i, e, j, cnt: (0, i)),
                      pl.BlockSpec((1, d, tf), lambda i, e, j, cnt: (e, 0, j)),
                      pl.BlockSpec((1, d, tf), lambda i, e, j, cnt: (e, 0, j)),
                      pl.BlockSpec((1, tf, d), lambda i, e, j, cnt: (e, j, 0))],
            out_specs=tok(d),
            scratch_shapes=[pltpu.VMEM((tm, d), BF16), pltpu.VMEM((tm, d), F32), pltpu.SemaphoreType.DMA(())],
        ),
        out_shape=jax.ShapeDtypeStruct((m, d), F32),
        compiler_params=_cparams("parallel", "arbitrary", "arbitrary", vmem_limit_bytes=MOE_VMEM_LIMIT_BYTES),
        name="moe",
    )(cnt, x, hb, comb, rank, rank_t, wg, wu, wd)


def _moe_layer(x, ln, rw, wg, wu, wd, *, tm, tf):
    hb, comb, rank, rank_t, cnt = _router(x, ln, rw, tm=tm)
    return _moe(x, hb, comb, rank, rank_t, cnt.reshape(cnt.shape[0], LANES), wg, wu, wd, tm=tm, tf=tf)


def _block_diag(w):
    n, c, d = w.shape
    eye = jnp.eye(n, dtype=w.dtype)
    return (eye[:, None, :, None] * w[:, :, None, :]).reshape(n * c, n * d)


def _tile_rows(m, pref):
    return pref if m % pref == 0 else m


def kernel(x_prompt, x_sample, cache_k, cache_v, state_lru_conv, state_lru_h, state_delta_conv, state_delta_s, page_table, ln_mix_e, w_in_e, conv_lru_w, conv_lru_b, lru_wx, lru_bx, lru_wa, lru_ba, lru_lambda, qk_gain_q, qk_gain_k, lam_q1, lam_k1, lam_q2, lam_k2, subln_g, w_out_e, ln_ffn_e, ffn_wg, ffn_wu, ffn_wd, ln_mix_o, w_in_o, conv_delta_w, a_log, dt_bias, onorm_g, w_out_o, ln_ffn_o, router_w, moe_wg, moe_wu, moe_wd):
    bp, tp, d = x_prompt.shape
    bs = x_sample.shape[0]
    n_even, n_odd = w_in_e.shape[0], w_in_o.shape[0]
    lru_w = conv_lru_w.shape[-1]
    att_w = cache_k.shape[-1] * cache_k.shape[-2]
    qkv_c = conv_delta_w.shape[-1]
    dk = onorm_g.shape[-1]
    assert x_sample.shape[1] == 1 and lru_w == att_w and w_in_e.shape[-1] == 5 * lru_w

    xp = x_prompt.reshape(bp * tp, d)
    xs = x_sample.reshape(bs, d)
    mp = bp * tp
    tmp = _tile_rows(mp, 512)
    outs = {k: [] for k in ("kp", "vp", "cp", "hp", "dcp", "dsp", "ks", "vs", "cs", "hs", "dcs", "dss")}
    bf = lambda a: a.astype(BF16)
    vec = lambda a: a.reshape(1, -1)

    for layer in range(n_even + n_odd):
        i = layer // 2
        if layer % 2 == 0:
            lam_init = 0.8 - 0.6 * math.exp(-0.3 * layer)
            w_in, w_out = bf(w_in_e[i]), bf(w_out_e[i])
            lru_args = (conv_lru_w[i], vec(conv_lru_b[i]), bf(_block_diag(lru_wx[i])), vec(lru_bx[i]),
                        bf(_block_diag(lru_wa[i])), vec(lru_ba[i]), vec(lru_lambda[i]))
            lam_args = (lam_q1[i], lam_k1[i], lam_q2[i], lam_k2[i], subln_g[i])
            wg, wu, wd = bf(ffn_wg[i]), bf(ffn_wu[i]), bf(ffn_wd[i])

            gg, xr, qn, k_rows, v_rows, kb, vb = _inproj_even(xp, ln_mix_e[i], w_in, qk_gain_q[i],
                                                              qk_gain_k[i], tm=tmp)
            seq = lambda a: a.reshape(bp, tp, -1)
            ya, h_last = _lru_prompt(seq(xr), seq(gg), *lru_args, tt=_tile_rows(tp, 512))
            o = _attn_prompt(seq(qn), seq(kb), seq(vb), *lam_args, lam_init=lam_init, tq=_tile_rows(tp, 256))
            xp = _outproj(xp, [(ya.reshape(mp, lru_w), w_out[:lru_w]), (o.reshape(mp, att_w), w_out[lru_w:])],
                          tm=tmp)
            xp = _ffn(xp, ln_ffn_e[i], wg, wu, wd, tm=tmp)
            outs["kp"].append(k_rows.reshape(bp, tp, H_B, att_w // H_B))
            outs["vp"].append(v_rows.reshape(bp, tp, H_B, att_w // H_B))
            outs["cp"].append(seq(xr)[:, tp - (CONV_W - 1):])
            outs["hp"].append(h_last.reshape(bp, lru_w))

            gg, xr, qn, k_rows, v_rows, _, _ = _inproj_even(xs, ln_mix_e[i], w_in, qk_gain_q[i],
                                                            qk_gain_k[i], tm=bs)
            ya, h_new = _lru_step(xr, gg, state_lru_conv[i], state_lru_h[i], *lru_args)
            o = _attn_decode(qn, k_rows, v_rows, cache_k[i], cache_v[i], page_table, *lam_args,
                             lam_init=lam_init)
            xs = _outproj(xs, [(ya, w_out[:lru_w]), (o, w_out[lru_w:])], tm=bs)
            xs = _ffn(xs, ln_ffn_e[i], wg, wu, wd, tm=bs)
            outs["ks"].append(k_rows.reshape(bs, 1, H_B, att_w // H_B))
            outs["vs"].append(v_rows.reshape(bs, 1, H_B, att_w // H_B))
            outs["cs"].append(jnp.concatenate([state_lru_conv[i][:, 1:], xr[:, None]], axis=1))
            outs["hs"].append(h_new)
        else:
            w_in, w_out = bf(w_in_o[i]), bf(w_out_o[i])
            wg, wu, wd = bf(moe_wg[i]), bf(moe_wu[i]), bf(moe_wd[i])
            tf = MOE_FF_TILE

            qkv, z, gb = _inproj_odd(xp, ln_mix_o[i], w_in, a_log[i], dt_bias[i], tm=tmp)
            seq = lambda a: a.reshape(bp, tp, -1)
            o, s_fin = _delta_prompt(seq(qkv), seq(z), seq(gb), conv_delta_w[i], onorm_g[i],
                                     tc=_tile_rows(tp, 256))
            xp = _outproj(xp, [(o.reshape(mp, -1), w_out)], tm=tmp)
            xp = _moe_layer(xp, ln_ffn_o[i], router_w[i], wg, wu, wd, tm=_tile_rows(mp, MOE_TOKENS), tf=tf)
            outs["dcp"].append(seq(qkv)[:, tp - (CONV_W - 1):])
            outs["dsp"].append(s_fin)

            qkv, z, gb = _inproj_odd(xs, ln_mix_o[i], w_in, a_log[i], dt_bias[i], tm=bs)
            o, s_new = _delta_step(qkv, state_delta_conv[i], z, gb, state_delta_s[i], conv_delta_w[i],
                                   onorm_g[i])
            xs = _outproj(xs, [(o, w_out)], tm=bs)
            xs = _moe_layer(xs, ln_ffn_o[i], router_w[i], wg, wu, wd, tm=bs, tf=tf)
            outs["dcs"].append(jnp.concatenate([state_delta_conv[i][:, 1:], qkv[:, None]], axis=1))
            outs["dss"].append(s_new)

    st = lambda k: jnp.stack(outs[k])
    return (xp.reshape(bp, tp, d), xs.reshape(bs, 1, d),
            st("kp"), st("vp"), st("cp"), st("hp"), st("dcp"), st("dsp"),
            st("ks"), st("vs"), st("cs"), st("hs"), st("dcs"), st("dss"))
```

```python
import functools
import math

import jax
import jax.numpy as jnp
from jax import lax
from jax.experimental import pallas as pl
from jax.experimental.pallas import tpu as pltpu

F32 = jnp.float32
BF16 = jnp.bfloat16
HIGHEST = lax.Precision.HIGHEST

CONV_W = 4
RG_C = 8.0
H_B = 4
H_C = 8
TOP_K = 2
EPS = 1e-6
NEG_INF = -1e30

LANES = 128
SUBLANES = 8
VMEM_LIMIT_BYTES = 52 * 1024 * 1024
MOE_VMEM_LIMIT_BYTES = 60 * 1024 * 1024

DELTA_CHUNK = 64
ATTN_HEADS_PER_STEP = 4
ATTN_FIXED_SHIFT_MAX = 40.0
MOE_ROWS = 128
MOE_TOKENS = 1024
MOE_FF_TILE = 1792
FFN_CHUNK = 512


def _cparams(*sem, vmem_limit_bytes=VMEM_LIMIT_BYTES):
    return pltpu.CompilerParams(dimension_semantics=sem, vmem_limit_bytes=vmem_limit_bytes)


def _full(shape):
    nd = len(shape)
    return pl.BlockSpec(shape, lambda *_: (0,) * nd)


def _rms(x, g):
    return x * lax.rsqrt(jnp.mean(x * x, axis=-1, keepdims=True) + EPS) * g


def _softplus(x):
    return jnp.maximum(x, 0.0) + jnp.log1p(jnp.exp(-jnp.abs(x)))


def _silu(x):
    return x * jax.nn.sigmoid(x)


def _bdot(a, b):
    return jnp.dot(a.astype(BF16), b.astype(BF16), preferred_element_type=F32)


def _bdot_nt(a, b):
    return lax.dot_general(a.astype(BF16), b.astype(BF16), (((1,), (1,)), ((), ())),
                           preferred_element_type=F32)


def _bdot_tn(a, b):
    return lax.dot_general(a.astype(BF16), b.astype(BF16), (((0,), (0,)), ((), ())),
                           preferred_element_type=F32)


def _shifted_rows(prev8, x):
    full = jnp.concatenate([prev8, x], axis=0)
    return [pltpu.roll(full, s, axis=0)[SUBLANES:] for s in (1, 2, 3)]


def _inproj_even_body(x_ref, ln_ref, w_ref, gq_ref, gk_ref, seg_ref,
                      gg_ref, xr_ref, q_ref, k_ref, v_ref, kb_ref, vb_ref, *, width, dh):
    hb = _rms(x_ref[...], ln_ref[...]).astype(BF16)

    def proj(j):
        return jnp.dot(hb, w_ref[:, j * width:(j + 1) * width], preferred_element_type=F32)

    def qk_norm(t, g):
        ms = _bdot(t * t, seg_ref[...]) * (1.0 / dh)
        return t * lax.rsqrt(ms + EPS) * g

    gg_ref[...] = jax.nn.gelu(proj(0)).astype(BF16)
    xr_ref[...] = proj(1)
    q_ref[...] = qk_norm(proj(2), gq_ref[...]).astype(BF16)
    k = qk_norm(proj(3), gk_ref[...])
    kb_ref[...] = k.astype(BF16)
    v = proj(4)
    vb_ref[...] = v.astype(BF16)
    hw = width // H_B
    for h in range(H_B):
        k_ref[:, h, :] = k[:, h * hw:(h + 1) * hw]
        v_ref[:, h, :] = v[:, h * hw:(h + 1) * hw]


def _inproj_even(x, ln, w, gq, gk, *, tm):
    m, d = x.shape
    width = w.shape[1] // 5
    dh = gq.shape[-1]
    reps = width // dh
    seg = jnp.kron(jnp.eye(reps, dtype=F32), jnp.ones((dh, dh), F32)).astype(BF16)
    gq_t = jnp.tile(gq.reshape(1, dh), (1, reps))
    gk_t = jnp.tile(gk.reshape(1, dh), (1, reps))
    row = lambda i: (i, 0)
    blk = pl.BlockSpec((tm, width), row)
    f32o = jax.ShapeDtypeStruct((m, width), F32)
    b16o = jax.ShapeDtypeStruct((m, width), BF16)
    kvblk = pl.BlockSpec((tm, H_B, width // H_B), lambda i: (i, 0, 0))
    kvo = jax.ShapeDtypeStruct((m, H_B, width // H_B), F32)
    return pl.pallas_call(
        functools.partial(_inproj_even_body, width=width, dh=dh),
        grid=(m // tm,),
        in_specs=[pl.BlockSpec((tm, d), row), _full((1, d)), _full(w.shape),
                  _full((1, width)), _full((1, width)), _full((width, width))],
        out_specs=[blk, blk, blk, kvblk, kvblk, blk, blk],
        out_shape=[b16o, f32o, b16o, kvo, kvo, b16o, b16o],
        compiler_params=_cparams("parallel"),
        name="inproj_even",
    )(x, ln.reshape(1, d), w, gq_t, gk_t, seg)


def _lru_coeffs(xc, wx_ref, bx_ref, wa_ref, ba_ref, lam_ref):
    xb = xc.astype(BF16)
    gate_x = jax.nn.sigmoid(jnp.dot(xb, wx_ref[...], preferred_element_type=F32) + bx_ref[...])
    gate_a = jax.nn.sigmoid(jnp.dot(xb, wa_ref[...], preferred_element_type=F32) + ba_ref[...])
    log_a = -RG_C * gate_a * _softplus(-lam_ref[...])
    a = jnp.exp(log_a)
    th = jnp.tanh(log_a)
    one_minus_a2 = -2.0 * th / (1.0 - th)
    b = jnp.sqrt(one_minus_a2) * (gate_x * xc)
    return a, b


def _lru_prompt_body(xr_ref, gg_ref, cw_ref, cb_ref, wx_ref, bx_ref, wa_ref, ba_ref, lam_ref,
                     ya_ref, hlast_ref, prev_s, h_s, a_s, b_s, *, tt):
    t = pl.program_id(1)

    @pl.when(t == 0)
    def _():
        prev_s[...] = jnp.zeros_like(prev_s)
        h_s[...] = jnp.zeros_like(h_s)

    xr = xr_ref[0]
    x1, x2, x3 = _shifted_rows(prev_s[...], xr)
    xc = (x3 * cw_ref[0:1, :] + x2 * cw_ref[1:2, :] + x1 * cw_ref[2:3, :] + xr * cw_ref[3:4, :]
          + cb_ref[...])
    prev_s[...] = xr[tt - SUBLANES:, :]
    a, b = _lru_coeffs(xc, wx_ref, bx_ref, wa_ref, ba_ref, lam_ref)
    a_s[...] = a
    b_s[...] = b

    def step(i, h):
        h = a_s[pl.ds(i, 1), :] * h + b_s[pl.ds(i, 1), :]
        b_s[pl.ds(i, 1), :] = h
        return h

    h = lax.fori_loop(0, tt, step, h_s[...], unroll=8)
    h_s[...] = h
    ya_ref[0] = (gg_ref[0].astype(F32) * b_s[...]).astype(BF16)
    hlast_ref[0] = h


def _lru_prompt(xr, gg, cw, cb, wx, bx, wa, ba, lam, *, tt):
    b, t, w = xr.shape
    seq = lambda i, j: (i, j, 0)
    vec = _full((1, w))
    return pl.pallas_call(
        functools.partial(_lru_prompt_body, tt=tt),
        grid=(b, t // tt),
        in_specs=[pl.BlockSpec((1, tt, w), seq), pl.BlockSpec((1, tt, w), seq), _full((CONV_W, w)), vec,
                  _full((w, w)), vec, _full((w, w)), vec, vec],
        out_specs=[pl.BlockSpec((1, tt, w), seq), pl.BlockSpec((1, 1, w), lambda i, j: (i, 0, 0))],
        out_shape=[jax.ShapeDtypeStruct((b, t, w), BF16), jax.ShapeDtypeStruct((b, 1, w), F32)],
        scratch_shapes=[pltpu.VMEM((SUBLANES, w), F32), pltpu.VMEM((1, w), F32),
                        pltpu.VMEM((tt, w), F32), pltpu.VMEM((tt, w), F32)],
        compiler_params=_cparams("parallel", "arbitrary"),
        name="lru_prompt",
    )(xr, gg, cw, cb, wx, bx, wa, ba, lam)


def _lru_step_body(xr_ref, gg_ref, b0_ref, b1_ref, b2_ref, h0_ref, cw_ref, cb_ref,
                   wx_ref, bx_ref, wa_ref, ba_ref, lam_ref, ya_ref, h_ref):
    xr = xr_ref[...]
    xc = (b0_ref[...] * cw_ref[0:1, :] + b1_ref[...] * cw_ref[1:2, :] + b2_ref[...] * cw_ref[2:3, :]
          + xr * cw_ref[3:4, :] + cb_ref[...])
    a, b = _lru_coeffs(xc, wx_ref, bx_ref, wa_ref, ba_ref, lam_ref)
    h = a * h0_ref[...] + b
    h_ref[...] = h
    ya_ref[...] = (gg_ref[...].astype(F32) * h).astype(BF16)


def _lru_step(xr, gg, buf, h0, cw, cb, wx, bx, wa, ba, lam):
    n, w = xr.shape
    args = (xr, gg, buf[:, 0], buf[:, 1], buf[:, 2], h0, cw, cb, wx, bx, wa, ba, lam)
    return pl.pallas_call(
        _lru_step_body,
        grid=(1,),
        in_specs=[_full(a.shape) for a in args],
        out_specs=[_full((n, w)), _full((n, w))],
        out_shape=[jax.ShapeDtypeStruct((n, w), BF16), jax.ShapeDtypeStruct((n, w), F32)],
        compiler_params=_cparams("arbitrary"),
        name="lru_step",
    )(*args)


def _diff_lambda(lq1_ref, lk1_ref, lq2_ref, lk2_ref, lam_init):
    s1 = jnp.sum(lq1_ref[...] * lk1_ref[...], axis=-1, keepdims=True)
    s2 = jnp.sum(lq2_ref[...] * lk2_ref[...], axis=-1, keepdims=True)
    return jnp.exp(s1) - jnp.exp(s2) + lam_init


def _attn_prompt_body(lq1_ref, lk1_ref, lq2_ref, lk2_ref, sg_ref, gq_ref, gk_ref, q_ref, k_ref, v_ref, o_ref,
                      *, tq, dh, nh, lam_init):
    i = pl.program_id(2)
    hw = 2 * dh
    heads = range(nh)
    lam = _diff_lambda(lq1_ref, lk1_ref, lq2_ref, lk2_ref, lam_init)
    lane = lax.broadcasted_iota(jnp.int32, (tq, hw), 1)
    qq = []
    for h in heads:
        q = q_ref[0, :, h * hw:(h + 1) * hw] * (dh ** -0.5)
        zero = jnp.zeros_like(q)
        qq.append(jnp.concatenate([jnp.where(lane < dh, q, zero), jnp.where(lane >= dh, q, zero)], axis=0))
    rowi = lax.broadcasted_iota(jnp.int32, (2 * tq, tq), 0)
    coli = lax.broadcasted_iota(jnp.int32, (2 * tq, tq), 1)
    causal = coli <= jnp.where(rowi >= tq, rowi - tq, rowi)

    def scores(j, masked):
        start = pl.multiple_of(j * tq, tq)
        kj = k_ref[0, pl.ds(start, tq), :]
        vj = v_ref[0, pl.ds(start, tq), :]
        s = [_bdot_nt(qq[h], kj[:, h * hw:(h + 1) * hw]) for h in heads]
        if masked:
            s = [jnp.where(causal, s[h], NEG_INF) for h in heads]
        return s, [vj[:, h * hw:(h + 1) * hw] for h in heads]

    def finish(l, acc):
        out = []
        for h in heads:
            o = acc[h] / l[h]
            od = o[:tq] - lam * o[tq:]
            out.append(_rms(od, sg_ref[...]) * (1.0 - lam_init))
        o_ref[0] = jnp.concatenate(out, axis=1).astype(BF16)

    bound = (dh ** 0.5) * 1.05 * jnp.max(jnp.abs(gq_ref[...])) * jnp.max(jnp.abs(gk_ref[...]))
    fixed_ok = bound <= ATTN_FIXED_SHIFT_MAX

    @pl.when(fixed_ok)
    def _():
        def block(j, carry, masked):
            s, vj = scores(j, masked)
            p = [jnp.exp(s[h] - bound) for h in heads]
            l = [carry[h][0] + sum(p[h][:, c0:c0 + LANES] for c0 in range(0, tq, LANES)) for h in heads]
            acc = [carry[h][1] + _bdot(p[h], vj[h]) for h in heads]
            return tuple((l[h], acc[h]) for h in heads)

        init = tuple((jnp.zeros((2 * tq, LANES), F32), jnp.zeros((2 * tq, hw), F32)) for _ in heads)
        carry = lax.fori_loop(0, i, lambda j, c: block(j, c, False), init)
        carry = block(i, carry, True)
        finish([jnp.sum(carry[h][0], axis=-1, keepdims=True) for h in heads], [carry[h][1] for h in heads])

    @pl.when(jnp.logical_not(fixed_ok))
    def _():
        def block(j, carry, masked):
            s, vj = scores(j, masked)
            m_new = [jnp.maximum(carry[h][0], jnp.max(s[h], axis=-1, keepdims=True)) for h in heads]
            alpha = [jnp.exp(carry[h][0] - m_new[h]) for h in heads]
            p = [jnp.exp(s[h] - m_new[h]) for h in heads]
            l = [alpha[h] * carry[h][1] + jnp.sum(p[h], axis=-1, keepdims=True) for h in heads]
            acc = [alpha[h] * carry[h][2] + _bdot(p[h], vj[h]) for h in heads]
            return tuple((m_new[h], l[h], acc[h]) for h in heads)

        init = tuple((jnp.full((2 * tq, 1), NEG_INF, F32), jnp.zeros((2 * tq, 1), F32),
                      jnp.zeros((2 * tq, hw), F32)) for _ in heads)
        carry = lax.fori_loop(0, i, lambda j, c: block(j, c, False), init)
        carry = block(i, carry, True)
        finish([carry[h][1] for h in heads], [carry[h][2] for h in heads])


def _attn_prompt(q, k, v, lq1, lk1, lq2, lk2, sub_g, gq, gk, *, lam_init, tq):
    b, t, w = q.shape
    hw = w // H_B
    dh = hw // 2
    vec = _full((1, dh))
    nh = ATTN_HEADS_PER_STEP
    assert tq % LANES == 0
    qblk = pl.BlockSpec((1, tq, nh * hw), lambda bi, hg, i: (bi, i, hg))
    kvblk = pl.BlockSpec((1, t, nh * hw), lambda bi, hg, i: (bi, 0, hg))
    return pl.pallas_call(
        functools.partial(_attn_prompt_body, tq=tq, dh=dh, nh=nh, lam_init=lam_init),
        grid=(b, H_B // nh, t // tq),
        in_specs=[vec, vec, vec, vec, _full((1, hw)), vec, vec, qblk, kvblk, kvblk],
        out_specs=qblk,
        out_shape=jax.ShapeDtypeStruct((b, t, w), BF16),
        compiler_params=_cparams("parallel", "parallel", "arbitrary"),
        name="attn_prompt",
    )(lq1.reshape(1, dh), lk1.reshape(1, dh), lq2.reshape(1, dh), lk2.reshape(1, dh),
      sub_g.reshape(1, hw), gq.reshape(1, dh), gk.reshape(1, dh), q, k, v)


def _attn_decode_body(pt_ref, lq1_ref, lk1_ref, lq2_ref, lk2_ref, sg_ref, q_ref, kn_ref, vn_ref, *refs,
                      n_pages, dh, lam_init):
    del pt_ref
    k_refs = refs[:n_pages]
    v_refs = refs[n_pages:2 * n_pages]
    o_ref = refs[2 * n_pages]
    hw = 2 * dh
    rows = 2 * H_B
    lam = _diff_lambda(lq1_ref, lk1_ref, lq2_ref, lk2_ref, lam_init)

    row = lax.broadcasted_iota(jnp.int32, (rows, hw), 0)
    lane = lax.broadcasted_iota(jnp.int32, (rows, hw), 1)

    def by_head(vec_ref):
        out = jnp.zeros((rows, hw), F32)
        for h in range(H_B):
            piece = vec_ref[0, :, h * hw:(h + 1) * hw].astype(F32)
            out = jnp.where(row // 2 == h, jnp.broadcast_to(piece, (rows, hw)), out)
        return out

    half = (lane >= dh) == (row % 2 == 1)
    wq = jnp.where(half, by_head(q_ref), 0.0) * (dh ** -0.5)
    kn = by_head(kn_ref)
    vn = by_head(vn_ref)
    s_new = jnp.sum(wq * kn, axis=-1, keepdims=True)

    n_rows = k_refs[0].shape[1]
    col_head = lax.broadcasted_iota(jnp.int32, (rows, n_rows), 1) % H_B
    own = col_head == lax.broadcasted_iota(jnp.int32, (rows, n_rows), 0) // 2
    wq_b = wq.astype(BF16)
    scores = [jnp.where(own, _bdot_nt(wq_b, k_refs[p][0]), NEG_INF) for p in range(n_pages)]
    m = s_new
    for s in scores:
        m = jnp.maximum(m, jnp.max(s, axis=-1, keepdims=True))
    p_new = jnp.exp(s_new - m)
    l = p_new
    acc = p_new * vn
    for p in range(n_pages):
        pr = jnp.exp(scores[p] - m)
        l = l + jnp.sum(pr, axis=-1, keepdims=True)
        acc = acc + _bdot(pr, v_refs[p][0])
    o = acc / l
    r_out = lax.broadcasted_iota(jnp.int32, (SUBLANES, rows), 0)
    r_in = lax.broadcasted_iota(jnp.int32, (SUBLANES, rows), 1)
    sel = (jnp.where(r_in == 2 * r_out, 1.0, 0.0) - lam * jnp.where(r_in == 2 * r_out + 1, 1.0, 0.0))
    od = jnp.dot(sel, o, preferred_element_type=F32, precision=HIGHEST)[:H_B]
    o_ref[0] = _rms(od, sg_ref[...]) * (1.0 - lam_init)


def _attn_decode(q, k_new, v_new, cache_k, cache_v, page_table, lq1, lk1, lq2, lk2, sub_g, *, lam_init):
    n, w = q.shape
    hw = w // H_B
    dh = hw // 2
    n_pages = page_table.shape[1]
    n_pool, page = cache_k.shape[0], cache_k.shape[1]
    ck = cache_k.reshape(n_pool, page * H_B, hw)
    cv = cache_v.reshape(n_pool, page * H_B, hw)
    vec = pl.BlockSpec((1, dh), lambda i, pt: (0, 0))
    tok = pl.BlockSpec((1, 1, w), lambda i, pt: (i, 0, 0))

    def page_spec(p):
        return pl.BlockSpec((1, page * H_B, hw), lambda i, pt: (pt[i, p], 0, 0))

    pages = [page_spec(p) for p in range(n_pages)]
    out = pl.pallas_call(
        functools.partial(_attn_decode_body, n_pages=n_pages, dh=dh, lam_init=lam_init),
        grid_spec=pltpu.PrefetchScalarGridSpec(
            num_scalar_prefetch=1,
            grid=(n,),
            in_specs=[vec, vec, vec, vec, pl.BlockSpec((1, hw), lambda i, pt: (0, 0)), tok, tok, tok]
            + pages + pages,
            out_specs=pl.BlockSpec((1, H_B, hw), lambda i, pt: (i, 0, 0)),
        ),
        out_shape=jax.ShapeDtypeStruct((n, H_B, hw), F32),
        compiler_params=_cparams("arbitrary"),
        name="attn_decode",
    )(page_table, lq1.reshape(1, dh), lk1.reshape(1, dh), lq2.reshape(1, dh), lk2.reshape(1, dh),
      sub_g.reshape(1, hw), q.reshape(n, 1, w), k_new.reshape(n, 1, w), v_new.reshape(n, 1, w),
      *([ck] * n_pages), *([cv] * n_pages))
    return out.reshape(n, w)


def _outproj_body(*refs, n_in):
    x_ref = refs[0]
    o_ref = refs[1 + 2 * n_in]
    acc = x_ref[...]
    for j in range(n_in):
        acc = acc + jnp.dot(refs[1 + 2 * j][...].astype(BF16), refs[2 + 2 * j][...],
                            preferred_element_type=F32)
    o_ref[...] = acc


def _outproj(x, pairs, *, tm):
    m, d = x.shape
    row = lambda i: (i, 0)
    in_specs = [pl.BlockSpec((tm, d), row)]
    args = [x]
    for a, w in pairs:
        in_specs += [pl.BlockSpec((tm, a.shape[1]), row), _full(w.shape)]
        args += [a, w]
    return pl.pallas_call(
        functools.partial(_outproj_body, n_in=len(pairs)),
        grid=(m // tm,),
        in_specs=in_specs,
        out_specs=pl.BlockSpec((tm, d), row),
        out_shape=jax.ShapeDtypeStruct((m, d), F32),
        compiler_params=_cparams("parallel"),
        name="outproj",
    )(*args)


def _resident(shape):
    nd = len(shape)
    return pl.BlockSpec(shape, lambda *_: (0,) * nd, pipeline_mode=pl.Buffered(1))


def _ffn_body(x_ref, ln_ref, wg_ref, wu_ref, wd_ref, o_ref, *, chunks):
    x = x_ref[...]
    hb = _rms(x, ln_ref[...]).astype(BF16)
    acc = x
    for lo, n in chunks:
        g = jnp.dot(hb, wg_ref[:, lo:lo + n], preferred_element_type=F32)
        u = jnp.dot(hb, wu_ref[:, lo:lo + n], preferred_element_type=F32)
        acc = acc + jnp.dot((_silu(g) * u).astype(BF16), wd_ref[lo:lo + n, :], preferred_element_type=F32)
    o_ref[...] = acc


def _ffn(x, ln, wg, wu, wd, *, tm):
    m, d = x.shape
    f = wg.shape[1]
    chunks = [(lo, min(FFN_CHUNK, f - lo)) for lo in range(0, f, FFN_CHUNK)]
    return pl.pallas_call(
        functools.partial(_ffn_body, chunks=chunks),
        grid=(m // tm,),
        in_specs=[pl.BlockSpec((tm, d), lambda i: (i, 0)), _full((1, d)),
                  _resident(wg.shape), _resident(wu.shape), _resident(wd.shape)],
        out_specs=pl.BlockSpec((tm, d), lambda i: (i, 0)),
        out_shape=jax.ShapeDtypeStruct((m, d), F32),
        compiler_params=_cparams("parallel"),
        name="ffn",
    )(x, ln.reshape(1, d), wg, wu, wd)


def _inproj_odd_body(x_ref, ln_ref, w_ref, wba_ref, alog_ref, dtb_ref, qkv_ref, z_ref, gb_ref,
                     *, n_qkv, n_z, tn):
    hb = _rms(x_ref[...], ln_ref[...]).astype(BF16)
    for c in range(n_qkv // tn):
        qkv_ref[:, c * tn:(c + 1) * tn] = jnp.dot(hb, w_ref[:, c * tn:(c + 1) * tn],
                                                  preferred_element_type=F32)
    for c in range(n_z // tn):
        z_ref[:, c * tn:(c + 1) * tn] = jnp.dot(
            hb, w_ref[:, n_qkv + c * tn:n_qkv + (c + 1) * tn], preferred_element_type=F32).astype(BF16)
    ba = jnp.dot(hb, wba_ref[...], preferred_element_type=F32)
    lane = lax.broadcasted_iota(jnp.int32, ba.shape, 1)
    beta = jax.nn.sigmoid(ba)
    g = -jnp.exp(alog_ref[...]) * _softplus(ba + dtb_ref[...])
    gb_ref[...] = jnp.where(lane < H_C, beta, g)


def _inproj_odd(x, ln, w, a_log, dt_bias, *, tm):
    m, d = x.shape
    n_z = H_C * (d // H_C)
    n_qkv = w.shape[1] - n_z - 2 * H_C
    wba = jnp.zeros((d, LANES), BF16).at[:, :2 * H_C].set(w[:, n_qkv + n_z:])
    alog = jnp.zeros((1, LANES), F32).at[0, H_C:2 * H_C].set(a_log)
    dtb = jnp.zeros((1, LANES), F32).at[0, H_C:2 * H_C].set(dt_bias)
    row = lambda i: (i, 0)
    return pl.pallas_call(
        functools.partial(_inproj_odd_body, n_qkv=n_qkv, n_z=n_z, tn=512),
        grid=(m // tm,),
        in_specs=[pl.BlockSpec((tm, d), row), _full((1, d)), _full(w.shape), _full(wba.shape),
                  _full((1, LANES)), _full((1, LANES))],
        out_specs=[pl.BlockSpec((tm, n_qkv), row), pl.BlockSpec((tm, n_z), row),
                   pl.BlockSpec((tm, LANES), row)],
        out_shape=[jax.ShapeDtypeStruct((m, n_qkv), F32), jax.ShapeDtypeStruct((m, n_z), BF16),
                   jax.ShapeDtypeStruct((m, LANES), F32)],
        compiler_params=_cparams("parallel"),
        name="inproj_odd",
    )(x, ln.reshape(1, d), w, wba, alog, dtb)


def _l2n(x):
    return x * lax.rsqrt(jnp.sum(x * x, axis=-1, keepdims=True) + EPS)


def _delta_prompt_body(qkv_ref, z_ref, gb_ref, cw_ref, og_ref, o_ref, s_ref, prev_s, xc_s, *, tc, dk):
    t = pl.program_id(1)
    c = DELTA_CHUNK
    nk = H_C * dk

    @pl.when(t == 0)
    def _():
        prev_s[...] = jnp.zeros_like(prev_s)
        s_ref[...] = jnp.zeros_like(s_ref)

    x = qkv_ref[0]
    x1, x2, x3 = _shifted_rows(prev_s[...], x)
    xc_s[...] = _silu(x3 * cw_ref[0:1, :] + x2 * cw_ref[1:2, :] + x1 * cw_ref[2:3, :] + x * cw_ref[3:4, :])
    prev_s[...] = x[tc - SUBLANES:, :]

    ri = lax.broadcasted_iota(jnp.int32, (c, c), 0)
    ci = lax.broadcasted_iota(jnp.int32, (c, c), 1)
    incl = ri >= ci
    strict = ri > ci
    tri = jnp.where(incl, 1.0, 0.0)
    eye = jnp.where(ri == ci, 1.0, 0.0)
    heads = range(H_C)
    n_chunks = tc // c

    q, k, v, beta, e_g, decay, g_end, k_dec = ([] for _ in range(8))
    for ic in range(n_chunks):
        rows = slice(ic * c, (ic + 1) * c)
        gbc = gb_ref[0, rows, :]
        g_all = jnp.dot(tri, gbc, preferred_element_type=F32, precision=HIGHEST)
        g_t = g_all.T
        for h in heads:
            g_col = g_all[:, H_C + h:H_C + h + 1]
            g_row = g_t[H_C + h:H_C + h + 1, :]
            g_last = g_all[c - 1:c, H_C + h:H_C + h + 1]
            kn = _l2n(xc_s[rows, nk + h * dk:nk + (h + 1) * dk])
            q.append(_l2n(xc_s[rows, h * dk:(h + 1) * dk]) * (dk ** -0.5))
            k.append(kn)
            v.append(xc_s[rows, 2 * nk + h * dk:2 * nk + (h + 1) * dk])
            beta.append(gbc[:, h:h + 1])
            e_g.append(jnp.exp(g_col))
            decay.append(jnp.where(incl, jnp.exp(jnp.where(incl, g_col - g_row, 0.0)), 0.0))
            g_end.append(jnp.exp(g_last))
            k_dec.append(kn * jnp.exp(g_last - g_col))
    pairs = range(n_chunks * H_C)
    kb = [k[i] * beta[i] for i in pairs]
    kk = [_bdot_nt(jnp.concatenate([kb[i], q[i]], axis=0), k[i]) for i in pairs]
    qk = [kk[i][c:] * decay[i] for i in pairs]
    pw = [-jnp.where(strict, kk[i][:c] * decay[i], 0.0) for i in pairs]
    inv = [eye + pw[i] for i in pairs]
    for _ in range(int(math.log2(c)) - 1):
        pw = [_bdot(pw[i], pw[i]) for i in pairs]
        inv = [inv[i] + _bdot(inv[i], pw[i]) for i in pairs]
    sol = [_bdot(inv[i], jnp.concatenate([v[i] * beta[i], kb[i] * e_g[i]], axis=1)) for i in pairs]
    wq = [jnp.concatenate([sol[i][:, dk:], q[i] * e_g[i]], axis=0).astype(BF16) for i in pairs]

    s = [s_ref[0, h] for h in heads]
    for ic in range(n_chunks):
        rows = slice(ic * c, (ic + 1) * c)
        idx = [ic * H_C + h for h in heads]
        ws = [_bdot(wq[i], s[h]) for h, i in zip(heads, idx)]
        v_new = [sol[i][:, :dk] - ws[h][:c] for h, i in zip(heads, idx)]
        o = [ws[h][c:] + _bdot(qk[i], v_new[h]) for h, i in zip(heads, idx)]
        s = [s[h] * g_end[i] + _bdot_tn(k_dec[i], v_new[h]) for h, i in zip(heads, idx)]
        zt = z_ref[0, rows, :].astype(F32)
        o_ref[0, rows, :] = jnp.concatenate(
            [_rms(o[h], og_ref[...]) * _silu(zt[:, h * dk:(h + 1) * dk]) for h in heads], axis=1).astype(BF16)
    for h in heads:
        s_ref[0, h] = s[h]


def _delta_prompt(qkv, z, gb, cw, og, *, tc):
    b, t, nq = qkv.shape
    nz = z.shape[2]
    dk = nz // H_C
    seq = lambda i, j: (i, j, 0)
    return pl.pallas_call(
        functools.partial(_delta_prompt_body, tc=tc, dk=dk),
        grid=(b, t // tc),
        in_specs=[pl.BlockSpec((1, tc, nq), seq), pl.BlockSpec((1, tc, nz), seq),
                  pl.BlockSpec((1, tc, LANES), seq), _full((CONV_W, nq)), _full((1, dk))],
        out_specs=[pl.BlockSpec((1, tc, nz), seq),
                   pl.BlockSpec((1, H_C, dk, dk), lambda i, j: (i, 0, 0, 0))],
        out_shape=[jax.ShapeDtypeStruct((b, t, nz), BF16), jax.ShapeDtypeStruct((b, H_C, dk, dk), F32)],
        scratch_shapes=[pltpu.VMEM((SUBLANES, nq), F32), pltpu.VMEM((tc, nq), F32)],
        compiler_params=_cparams("parallel", "arbitrary"),
        name="delta_prompt",
    )(qkv, z, gb, cw, og.reshape(1, dk))


def _delta_prep_step_body(x_ref, b0_ref, b1_ref, b2_ref, cw_ref, q_ref, k_ref, v_ref, *, dk):
    xc = _silu(b0_ref[...] * cw_ref[0:1, :] + b1_ref[...] * cw_ref[1:2, :] + b2_ref[...] * cw_ref[2:3, :]
               + x_ref[...] * cw_ref[3:4, :])
    nk = H_C * dk
    for h in range(H_C):
        sl = slice(h * dk, (h + 1) * dk)
        q_ref[:, sl] = _l2n(xc[:, h * dk:(h + 1) * dk]) * (dk ** -0.5)
        k_ref[:, sl] = _l2n(xc[:, nk + h * dk:nk + (h + 1) * dk])
    v_ref[...] = xc[:, 2 * nk:]


def _delta_step_body(q_ref, k_ref, v_ref, gb_ref, z_ref, og_ref, s_ref, o_ref, sn_ref, *, dk):
    row = lax.broadcasted_iota(jnp.int32, (SUBLANES, dk), 0)
    heads = range(H_C)
    rows8 = lambda x: jnp.broadcast_to(x, (SUBLANES, dk))
    q = [q_ref[0, :, h * dk:(h + 1) * dk] for h in heads]
    k = [k_ref[0, :, h * dk:(h + 1) * dk] for h in heads]
    v = [v_ref[0, :, h * dk:(h + 1) * dk] for h in heads]
    beta = [gb_ref[0, :, h:h + 1] for h in heads]
    e_g = [jnp.exp(gb_ref[0, :, H_C + h:H_C + h + 1]) for h in heads]
    s = [s_ref[0, h] for h in heads]
    kq = [jnp.where(row == 0, rows8(k[h]), jnp.where(row == 1, rows8(q[h] * e_g[h]), 0.0)) for h in heads]
    ks = [_bdot(kq[h], s[h]) for h in heads]
    v_new = [beta[h] * (v[h] - e_g[h] * ks[h][0:1]) for h in heads]
    o = [ks[h][1:2] + jnp.sum(q[h] * k[h], axis=-1, keepdims=True) * v_new[h] for h in heads]
    outer = [lax.dot_general(jnp.where(row == 0, rows8(k[h]), 0.0), jnp.where(row == 0, rows8(v_new[h]), 0.0),
                             (((0,), (0,)), ((), ())), preferred_element_type=F32, precision=HIGHEST)
             for h in heads]
    for h in heads:
        sn_ref[0, h] = s[h] * e_g[h] + outer[h]
    zt = z_ref[0].astype(F32)
    o_ref[0] = jnp.concatenate(
        [_rms(o[h], og_ref[...]) * _silu(zt[:, h * dk:(h + 1) * dk]) for h in heads], axis=1).astype(BF16)


def _delta_step(qkv, buf, z, gb, s0, cw, og):
    n, nq = qkv.shape
    nz = z.shape[1]
    dk = nz // H_C
    args = (qkv, buf[:, 0], buf[:, 1], buf[:, 2], cw)
    q, k, v = pl.pallas_call(
        functools.partial(_delta_prep_step_body, dk=dk),
        grid=(1,),
        in_specs=[_full(a.shape) for a in args],
        out_specs=[_full((n, nz))] * 3,
        out_shape=[jax.ShapeDtypeStruct((n, nz), F32)] * 3,
        compiler_params=_cparams("arbitrary"),
        name="delta_prep_step",
    )(*args)
    tok = lambda width: pl.BlockSpec((1, 1, width), lambda i: (i, 0, 0))
    st = pl.BlockSpec((1, H_C, dk, dk), lambda i: (i, 0, 0, 0))
    o, s_new = pl.pallas_call(
        functools.partial(_delta_step_body, dk=dk),
        grid=(n,),
        in_specs=[tok(nz), tok(nz), tok(nz), tok(LANES), tok(nz), _full((1, dk)), st],
        out_specs=[tok(nz), st],
        out_shape=[jax.ShapeDtypeStruct((n, 1, nz), BF16), jax.ShapeDtypeStruct(s0.shape, F32)],
        compiler_params=_cparams("parallel"),
        name="delta_step",
    )(q.reshape(n, 1, nz), k.reshape(n, 1, nz), v.reshape(n, 1, nz), gb.reshape(n, 1, LANES),
      z.reshape(n, 1, nz), og.reshape(1, dk), s0)
    return o.reshape(n, nz), s_new


ROUTER_ROWS = 512
SCATTER_ROWS = 512


def _router_body(x_ref, ln_ref, rw_ref, hb_ref, comb_ref, rank_ref, rank_t_ref, cnt_ref, *, n_exp, tm):
    tb = min(ROUTER_ROWS, tm)
    lane = lax.broadcasted_iota(jnp.int32, (tb, LANES), 1).astype(F32)
    ri = lax.broadcasted_iota(jnp.int32, (tb, tb), 0)
    ci = lax.broadcasted_iota(jnp.int32, (tb, tb), 1)
    before = jnp.where(ri > ci, 1.0, 0.0).astype(BF16)
    running = jnp.zeros((1, LANES), F32)
    for t0 in range(0, tm, tb):
        rows = slice(t0, t0 + tb)
        h = _rms(x_ref[rows, :], ln_ref[...])
        hb_ref[rows, :] = h.astype(BF16)
        logits = jnp.dot(h, rw_ref[...], preferred_element_type=F32, precision=HIGHEST)
        logits = jnp.where(lane < n_exp, logits, -jnp.inf)
        m1 = jnp.max(logits, axis=-1, keepdims=True)
        i1 = jnp.min(jnp.where(logits == m1, lane, float(LANES)), axis=-1, keepdims=True)
        rest = jnp.where(lane == i1, -jnp.inf, logits)
        m2 = jnp.max(rest, axis=-1, keepdims=True)
        i2 = jnp.min(jnp.where(rest == m2, lane, float(LANES)), axis=-1, keepdims=True)
        e = jnp.exp(m2 - m1)
        g1 = 1.0 / (1.0 + e)
        g2 = e / (1.0 + e)
        comb = jnp.where(lane == i1, g1, 0.0) + jnp.where(lane == i2, g2, 0.0)
        comb_ref[rows, :] = comb
        sel = comb != 0.0
        self_f = jnp.where(sel, 1.0, 0.0)
        local = jnp.dot(before, self_f.astype(BF16), preferred_element_type=F32)
        rank = jnp.where(sel, local + running, -1.0)
        rank_ref[rows, :] = rank
        rank_t_ref[:, rows] = rank.T[:SUBLANES, :]
        running = running + jnp.sum(self_f, axis=0, keepdims=True)
    cnt_ref[0] = running.astype(jnp.int32)


def _router(x, ln, rw, *, tm):
    m, d = x.shape
    n_exp = rw.shape[1]
    assert n_exp <= SUBLANES
    rwp = jnp.zeros((d, LANES), F32).at[:, :n_exp].set(rw)
    row = lambda i: (i, 0)
    return pl.pallas_call(
        functools.partial(_router_body, n_exp=n_exp, tm=tm),
        grid=(m // tm,),
        in_specs=[pl.BlockSpec((tm, d), row), _full((1, d)), _full((d, LANES))],
        out_specs=[pl.BlockSpec((tm, d), row), pl.BlockSpec((tm, LANES), row), pl.BlockSpec((tm, LANES), row),
                   pl.BlockSpec((SUBLANES, tm), lambda i: (0, i)),
                   pl.BlockSpec((1, 1, LANES), lambda i: (i, 0, 0))],
        out_shape=[jax.ShapeDtypeStruct((m, d), BF16), jax.ShapeDtypeStruct((m, LANES), F32),
                   jax.ShapeDtypeStruct((m, LANES), F32), jax.ShapeDtypeStruct((SUBLANES, m), F32),
                   jax.ShapeDtypeStruct((m // tm, 1, LANES), jnp.int32)],
        compiler_params=_cparams("parallel"),
        name="router",
    )(x, ln.reshape(1, d), rwp)


def _moe_body(cnt_ref, x_hbm, hb_ref, comb_ref, rank_ref, rank_t_ref, wg_ref, wu_ref, wd_ref, o_ref,
              xs_s, ys_s, x_sem, *, tm):
    i, e, f = pl.program_id(0), pl.program_id(1), pl.program_id(2)
    n_f = pl.num_programs(2)
    count = cnt_ref[i, e]
    r = MOE_ROWS

    @pl.when(jnp.logical_and(e == 0, f == 0))
    def _():
        rows = pl.ds(pl.multiple_of(i * tm, tm), tm)
        copy = pltpu.make_async_copy(x_hbm.at[rows, :], o_ref, x_sem)
        copy.start()
        copy.wait()

    def gather(lo, n):
        want = (lax.broadcasted_iota(jnp.int32, (n, tm), 0) + lo).astype(F32)
        onehot = jnp.where(rank_t_ref[pl.ds(e, 1), :] == want, 1.0, 0.0).astype(BF16)
        xs_s[lo:lo + n, :] = jnp.dot(onehot, hb_ref[...], preferred_element_type=F32).astype(BF16)

    def ffn(lo, n):
        xs = xs_s[lo:lo + n, :]
        g = jnp.dot(xs, wg_ref[0], preferred_element_type=F32)
        u = jnp.dot(xs, wu_ref[0], preferred_element_type=F32)
        y = jnp.dot((_silu(g) * u).astype(BF16), wd_ref[0], preferred_element_type=F32)

        @pl.when(f == 0)
        def _():
            ys_s[lo:lo + n, :] = y

        @pl.when(f > 0)
        def _():
            ys_s[lo:lo + n, :] += y

    def scatter(lo, n):
        tb = min(SCATTER_ROWS, tm)
        pick = lax.broadcasted_iota(jnp.int32, (tb, LANES), 1) == e
        want = (lax.broadcasted_iota(jnp.int32, (tb, n), 1) + lo).astype(F32)
        yb = ys_s[lo:lo + n, :].astype(BF16)
        for t0 in range(0, tm, tb):
            rank_col = jnp.sum(jnp.where(pick, rank_ref[t0:t0 + tb, :], 0.0), axis=-1, keepdims=True)
            gate_col = jnp.sum(jnp.where(pick, comb_ref[t0:t0 + tb, :], 0.0), axis=-1, keepdims=True)
            onehot = jnp.where(rank_col == want, 1.0, 0.0).astype(BF16)
            o_ref[t0:t0 + tb, :] += gate_col * jnp.dot(onehot, yb, preferred_element_type=F32)

    grp = min(2 * r, tm)
    for lo in range(0, tm, grp):
        if grp == 2 * r:
            @pl.when(count > lo + r)
            def _(lo=lo):
                @pl.when(f == 0)
                def _():
                    gather(lo, 2 * r)
                ffn(lo, 2 * r)

            @pl.when(jnp.logical_and(count > lo, count <= lo + r))
            def _(lo=lo):
                @pl.when(f == 0)
                def _():
                    gather(lo, r)
                    ys_s[lo + r:lo + 2 * r, :] = jnp.zeros((r, ys_s.shape[1]), F32)
                ffn(lo, r)
        else:
            @pl.when(count > lo)
            def _(lo=lo):
                @pl.when(f == 0)
                def _():
                    gather(lo, r)
                ffn(lo, r)

        @pl.when(jnp.logical_and(f == n_f - 1, count > lo))
        def _(lo=lo):
            scatter(lo, grp)


def _moe(x, hb, comb, rank, rank_t, cnt, wg, wu, wd, *, tm, tf):
    m, d = x.shape
    n_exp, _, f = wg.shape
    assert tm % MOE_ROWS == 0 and (tm <= MOE_ROWS or tm % (2 * MOE_ROWS) == 0)
    tok = lambda width: pl.BlockSpec((tm, width), lambda i, e, j, cnt: (i, 0))
    return pl.pallas_call(
        functools.partial(_moe_body, tm=tm),
        grid_spec=pltpu.PrefetchScalarGridSpec(
            num_scalar_prefetch=1,
            grid=(m // tm, n_exp, f // tf),
            in_specs=[pl.BlockSpec(memory_space=pl.ANY), tok(d), tok(LANES), tok(LANES),
                      pl.BlockSpec((SUBLANES, tm), lambda i, e, j, cnt: (0, i)),
                      pl.BlockSpec((1, d, tf), lambda i, e, j, cnt: (e, 0, j)),
                      pl.BlockSpec((1, d, tf), lambda i, e, j, cnt: (e, 0, j)),
                      pl.BlockSpec((1, tf, d), lambda i, e, j, cnt: (e, j, 0))],
            out_specs=tok(d),
            scratch_shapes=[pltpu.VMEM((tm, d), BF16), pltpu.VMEM((tm, d), F32), pltpu.SemaphoreType.DMA(())],
        ),
        out_shape=jax.ShapeDtypeStruct((m, d), F32),
        compiler_params=_cparams("parallel", "arbitrary", "arbitrary", vmem_limit_bytes=MOE_VMEM_LIMIT_BYTES),
        name="moe",
    )(cnt, x, hb, comb, rank, rank_t, wg, wu, wd)


def _moe_layer(x, ln, rw, wg, wu, wd, *, tm, tf):
    hb, comb, rank, rank_t, cnt = _router(x, ln, rw, tm=tm)
    return _moe(x, hb, comb, rank, rank_t, cnt.reshape(cnt.shape[0], LANES), wg, wu, wd, tm=tm, tf=tf)


def _block_diag(w):
    n, c, d = w.shape
    eye = jnp.eye(n, dtype=w.dtype)
    return (eye[:, None, :, None] * w[:, :, None, :]).reshape(n * c, n * d)


def _tile_rows(m, pref):
    return pref if m % pref == 0 else m


def kernel(x_prompt, x_sample, cache_k, cache_v, state_lru_conv, state_lru_h, state_delta_conv, state_delta_s, page_table, ln_mix_e, w_in_e, conv_lru_w, conv_lru_b, lru_wx, lru_bx, lru_wa, lru_ba, lru_lambda, qk_gain_q, qk_gain_k, lam_q1, lam_k1, lam_q2, lam_k2, subln_g, w_out_e, ln_ffn_e, ffn_wg, ffn_wu, ffn_wd, ln_mix_o, w_in_o, conv_delta_w, a_log, dt_bias, onorm_g, w_out_o, ln_ffn_o, router_w, moe_wg, moe_wu, moe_wd):
    bp, tp, d = x_prompt.shape
    bs = x_sample.shape[0]
    n_even, n_odd = w_in_e.shape[0], w_in_o.shape[0]
    lru_w = conv_lru_w.shape[-1]
    att_w = cache_k.shape[-1] * cache_k.shape[-2]
    qkv_c = conv_delta_w.shape[-1]
    dk = onorm_g.shape[-1]
    assert x_sample.shape[1] == 1 and lru_w == att_w and w_in_e.shape[-1] == 5 * lru_w

    xp = x_prompt.reshape(bp * tp, d)
    xs = x_sample.reshape(bs, d)
    mp = bp * tp
    tmp = _tile_rows(mp, 512)
    outs = {k: [] for k in ("kp", "vp", "cp", "hp", "dcp", "dsp", "ks", "vs", "cs", "hs", "dcs", "dss")}
    bf = lambda a: a.astype(BF16)
    vec = lambda a: a.reshape(1, -1)

    for layer in range(n_even + n_odd):
        i = layer // 2
        if layer % 2 == 0:
            lam_init = 0.8 - 0.6 * math.exp(-0.3 * layer)
            w_in, w_out = bf(w_in_e[i]), bf(w_out_e[i])
            lru_args = (conv_lru_w[i], vec(conv_lru_b[i]), bf(_block_diag(lru_wx[i])), vec(lru_bx[i]),
                        bf(_block_diag(lru_wa[i])), vec(lru_ba[i]), vec(lru_lambda[i]))
            lam_args = (lam_q1[i], lam_k1[i], lam_q2[i], lam_k2[i], subln_g[i])
            wg, wu, wd = bf(ffn_wg[i]), bf(ffn_wu[i]), bf(ffn_wd[i])

            gg, xr, qn, k_rows, v_rows, kb, vb = _inproj_even(xp, ln_mix_e[i], w_in, qk_gain_q[i],
                                                              qk_gain_k[i], tm=tmp)
            seq = lambda a: a.reshape(bp, tp, -1)
            ya, h_last = _lru_prompt(seq(xr), seq(gg), *lru_args, tt=_tile_rows(tp, 512))
            o = _attn_prompt(seq(qn), seq(kb), seq(vb), *lam_args, qk_gain_q[i], qk_gain_k[i],
                             lam_init=lam_init, tq=_tile_rows(tp, 256))
            xp = _outproj(xp, [(ya.reshape(mp, lru_w), w_out[:lru_w]), (o.reshape(mp, att_w), w_out[lru_w:])],
                          tm=tmp)
            xp = _ffn(xp, ln_ffn_e[i], wg, wu, wd, tm=tmp)
            outs["kp"].append(k_rows.reshape(bp, tp, H_B, att_w // H_B))
            outs["vp"].append(v_rows.reshape(bp, tp, H_B, att_w // H_B))
            outs["cp"].append(seq(xr)[:, tp - (CONV_W - 1):])
            outs["hp"].append(h_last.reshape(bp, lru_w))

            gg, xr, qn, k_rows, v_rows, _, _ = _inproj_even(xs, ln_mix_e[i], w_in, qk_gain_q[i],
                                                            qk_gain_k[i], tm=bs)
            ya, h_new = _lru_step(xr, gg, state_lru_conv[i], state_lru_h[i], *lru_args)
            o = _attn_decode(qn, k_rows.reshape(bs, att_w), v_rows.reshape(bs, att_w), cache_k[i], cache_v[i],
                             page_table, *lam_args,
                             lam_init=lam_init)
            xs = _outproj(xs, [(ya, w_out[:lru_w]), (o, w_out[lru_w:])], tm=bs)
            xs = _ffn(xs, ln_ffn_e[i], wg, wu, wd, tm=bs)
            outs["ks"].append(k_rows.reshape(bs, 1, H_B, att_w // H_B))
            outs["vs"].append(v_rows.reshape(bs, 1, H_B, att_w // H_B))
            outs["cs"].append(jnp.concatenate([state_lru_conv[i][:, 1:], xr[:, None]], axis=1))
            outs["hs"].append(h_new)
        else:
            w_in, w_out = bf(w_in_o[i]), bf(w_out_o[i])
            wg, wu, wd = bf(moe_wg[i]), bf(moe_wu[i]), bf(moe_wd[i])
            tf = MOE_FF_TILE

            qkv, z, gb = _inproj_odd(xp, ln_mix_o[i], w_in, a_log[i], dt_bias[i], tm=tmp)
            seq = lambda a: a.reshape(bp, tp, -1)
            o, s_fin = _delta_prompt(seq(qkv), seq(z), seq(gb), conv_delta_w[i], onorm_g[i],
                                     tc=_tile_rows(tp, 256))
            xp = _outproj(xp, [(o.reshape(mp, -1), w_out)], tm=tmp)
            xp = _moe_layer(xp, ln_ffn_o[i], router_w[i], wg, wu, wd, tm=_tile_rows(mp, MOE_TOKENS), tf=tf)
            outs["dcp"].append(seq(qkv)[:, tp - (CONV_W - 1):])
            outs["dsp"].append(s_fin)

            qkv, z, gb = _inproj_odd(xs, ln_mix_o[i], w_in, a_log[i], dt_bias[i], tm=bs)
            o, s_new = _delta_step(qkv, state_delta_conv[i], z, gb, state_delta_s[i], conv_delta_w[i],
                                   onorm_g[i])
            xs = _outproj(xs, [(o, w_out)], tm=bs)
            xs = _moe_layer(xs, ln_ffn_o[i], router_w[i], wg, wu, wd, tm=bs, tf=tf)
            outs["dcs"].append(jnp.concatenate([state_delta_conv[i][:, 1:], qkv[:, None]], axis=1))
            outs["dss"].append(s_new)

    st = lambda k: jnp.stack(outs[k])
    return (xp.reshape(bp, tp, d), xs.reshape(bs, 1, d),
            st("kp"), st("vp"), st("cp"), st("hp"), st("dcp"), st("dsp"),
            st("ks"), st("vs"), st("cs"), st("hs"), st("dcs"), st("dss"))
```

```python
import functools
import math

import jax
import jax.numpy as jnp
from jax import lax
from jax.experimental import pallas as pl
from jax.experimental.pallas import tpu as pltpu

F32 = jnp.float32
BF16 = jnp.bfloat16
HIGHEST = lax.Precision.HIGHEST

CONV_W = 4
RG_C = 8.0
H_B = 4
H_C = 8
TOP_K = 2
EPS = 1e-6
NEG_INF = -1e30

LANES = 128
SUBLANES = 8
VMEM_LIMIT_BYTES = 52 * 1024 * 1024
MOE_VMEM_LIMIT_BYTES = 60 * 1024 * 1024

LRU_SEQS_PER_STEP = 4
DELTA_CHUNK = 64
DELTA_STEP_SEQS = 4
ATTN_HEADS_PER_STEP = 4
ATTN_FIXED_SHIFT_MAX = 40.0
MOE_ROWS = 128
MOE_TOKENS = 1024
MOE_FF_TILE = 1792
FFN_CHUNK = 512


def _cparams(*sem, vmem_limit_bytes=VMEM_LIMIT_BYTES):
    return pltpu.CompilerParams(dimension_semantics=sem, vmem_limit_bytes=vmem_limit_bytes)


def _full(shape):
    nd = len(shape)
    return pl.BlockSpec(shape, lambda *_: (0,) * nd)


def _rms(x, g):
    return x * lax.rsqrt(jnp.mean(x * x, axis=-1, keepdims=True) + EPS) * g


def _softplus(x):
    return jnp.maximum(x, 0.0) + jnp.log1p(jnp.exp(-jnp.abs(x)))


def _silu(x):
    return x * jax.nn.sigmoid(x)


def _bdot(a, b):
    return jnp.dot(a.astype(BF16), b.astype(BF16), preferred_element_type=F32)


def _bdot_nt(a, b):
    return lax.dot_general(a.astype(BF16), b.astype(BF16), (((1,), (1,)), ((), ())),
                           preferred_element_type=F32)


def _bdot_tn(a, b):
    return lax.dot_general(a.astype(BF16), b.astype(BF16), (((0,), (0,)), ((), ())),
                           preferred_element_type=F32)


def _split_bf16(x):
    hi = x.astype(BF16)
    return hi, (x - hi.astype(F32)).astype(BF16)


def _shifted_rows(prev8, x):
    full = jnp.concatenate([prev8, x], axis=0)
    return [pltpu.roll(full, s, axis=0)[SUBLANES:] for s in (1, 2, 3)]


def _inproj_even_body(x_ref, ln_ref, w_ref, gq_ref, gk_ref, seg_ref,
                      gg_ref, xr_ref, q_ref, k_ref, v_ref, kb_ref, vb_ref, *, width, dh):
    hb = _rms(x_ref[...], ln_ref[...]).astype(BF16)

    def proj(j):
        return jnp.dot(hb, w_ref[:, j * width:(j + 1) * width], preferred_element_type=F32)

    def qk_norm(t, g):
        ms = _bdot(t * t, seg_ref[...]) * (1.0 / dh)
        return t * lax.rsqrt(ms + EPS) * g

    gg_ref[...] = jax.nn.gelu(proj(0)).astype(BF16)
    xr_ref[...] = proj(1)
    q_ref[...] = qk_norm(proj(2), gq_ref[...]).astype(BF16)
    k = qk_norm(proj(3), gk_ref[...])
    kb_ref[...] = k.astype(BF16)
    v = proj(4)
    vb_ref[...] = v.astype(BF16)
    hw = width // H_B
    for h in range(H_B):
        k_ref[:, h, :] = k[:, h * hw:(h + 1) * hw]
        v_ref[:, h, :] = v[:, h * hw:(h + 1) * hw]


def _inproj_even(x, ln, w, gq, gk, *, tm):
    m, d = x.shape
    width = w.shape[1] // 5
    dh = gq.shape[-1]
    reps = width // dh
    seg = jnp.kron(jnp.eye(reps, dtype=F32), jnp.ones((dh, dh), F32)).astype(BF16)
    gq_t = jnp.tile(gq.reshape(1, dh), (1, reps))
    gk_t = jnp.tile(gk.reshape(1, dh), (1, reps))
    row = lambda i: (i, 0)
    blk = pl.BlockSpec((tm, width), row)
    f32o = jax.ShapeDtypeStruct((m, width), F32)
    b16o = jax.ShapeDtypeStruct((m, width), BF16)
    kvblk = pl.BlockSpec((tm, H_B, width // H_B), lambda i: (i, 0, 0))
    kvo = jax.ShapeDtypeStruct((m, H_B, width // H_B), F32)
    return pl.pallas_call(
        functools.partial(_inproj_even_body, width=width, dh=dh),
        grid=(m // tm,),
        in_specs=[pl.BlockSpec((tm, d), row), _full((1, d)), _full(w.shape),
                  _full((1, width)), _full((1, width)), _full((width, width))],
        out_specs=[blk, blk, blk, kvblk, kvblk, blk, blk],
        out_shape=[b16o, f32o, b16o, kvo, kvo, b16o, b16o],
        compiler_params=_cparams("parallel"),
        name="inproj_even",
    )(x, ln.reshape(1, d), w, gq_t, gk_t, seg)


def _lru_coeffs(xc, wx_ref, bx_ref, wa_ref, ba_ref, lam_ref):
    xb = xc.astype(BF16)
    gate_x = jax.nn.sigmoid(jnp.dot(xb, wx_ref[...], preferred_element_type=F32) + bx_ref[...])
    gate_a = jax.nn.sigmoid(jnp.dot(xb, wa_ref[...], preferred_element_type=F32) + ba_ref[...])
    log_a = -RG_C * gate_a * _softplus(-lam_ref[...])
    a = jnp.exp(log_a)
    th = jnp.tanh(log_a)
    one_minus_a2 = -2.0 * th / (1.0 - th)
    b = jnp.sqrt(one_minus_a2) * (gate_x * xc)
    return a, b


def _lru_prompt_body(xr_ref, gg_ref, cw_ref, cb_ref, wx_ref, bx_ref, wa_ref, ba_ref, lam_ref,
                     ya_ref, hlast_ref, prev_s, h_s, a_s, b_s, *, tt, nb):
    t = pl.program_id(1)

    @pl.when(t == 0)
    def _():
        prev_s[...] = jnp.zeros_like(prev_s)
        h_s[...] = jnp.zeros_like(h_s)

    for b in range(nb):
        xr = xr_ref[b]
        x1, x2, x3 = _shifted_rows(prev_s[b], xr)
        xc = (x3 * cw_ref[0:1, :] + x2 * cw_ref[1:2, :] + x1 * cw_ref[2:3, :] + xr * cw_ref[3:4, :]
              + cb_ref[...])
        prev_s[b] = xr[tt - SUBLANES:, :]
        a, bb = _lru_coeffs(xc, wx_ref, bx_ref, wa_ref, ba_ref, lam_ref)
        a_s[b] = a
        b_s[b] = bb

    def step(i, hs):
        out = []
        for b in range(nb):
            h = a_s[b, pl.ds(i, 1), :] * hs[b] + b_s[b, pl.ds(i, 1), :]
            b_s[b, pl.ds(i, 1), :] = h
            out.append(h)
        return tuple(out)

    hs = lax.fori_loop(0, tt, step, tuple(h_s[b] for b in range(nb)), unroll=8)
    for b in range(nb):
        h_s[b] = hs[b]
        ya_ref[b] = (gg_ref[b].astype(F32) * b_s[b]).astype(BF16)
        hlast_ref[b] = hs[b]


def _lru_prompt(xr, gg, cw, cb, wx, bx, wa, ba, lam, *, tt):
    b, t, w = xr.shape
    nb = LRU_SEQS_PER_STEP if b % LRU_SEQS_PER_STEP == 0 else 1
    seq = lambda i, j: (i, j, 0)
    vec = _full((1, w))
    return pl.pallas_call(
        functools.partial(_lru_prompt_body, tt=tt, nb=nb),
        grid=(b // nb, t // tt),
        in_specs=[pl.BlockSpec((nb, tt, w), seq), pl.BlockSpec((nb, tt, w), seq), _full((CONV_W, w)), vec,
                  _full((w, w)), vec, _full((w, w)), vec, vec],
        out_specs=[pl.BlockSpec((nb, tt, w), seq), pl.BlockSpec((nb, 1, w), lambda i, j: (i, 0, 0))],
        out_shape=[jax.ShapeDtypeStruct((b, t, w), BF16), jax.ShapeDtypeStruct((b, 1, w), F32)],
        scratch_shapes=[pltpu.VMEM((nb, SUBLANES, w), F32), pltpu.VMEM((nb, 1, w), F32),
                        pltpu.VMEM((nb, tt, w), F32), pltpu.VMEM((nb, tt, w), F32)],
        compiler_params=_cparams("parallel", "arbitrary"),
        name="lru_prompt",
    )(xr, gg, cw, cb, wx, bx, wa, ba, lam)


def _lru_step_body(xr_ref, gg_ref, b0_ref, b1_ref, b2_ref, h0_ref, cw_ref, cb_ref,
                   wx_ref, bx_ref, wa_ref, ba_ref, lam_ref, ya_ref, h_ref):
    xr = xr_ref[...]
    xc = (b0_ref[...] * cw_ref[0:1, :] + b1_ref[...] * cw_ref[1:2, :] + b2_ref[...] * cw_ref[2:3, :]
          + xr * cw_ref[3:4, :] + cb_ref[...])
    a, b = _lru_coeffs(xc, wx_ref, bx_ref, wa_ref, ba_ref, lam_ref)
    h = a * h0_ref[...] + b
    h_ref[...] = h
    ya_ref[...] = (gg_ref[...].astype(F32) * h).astype(BF16)


def _lru_step(xr, gg, buf, h0, cw, cb, wx, bx, wa, ba, lam):
    n, w = xr.shape
    args = (xr, gg, buf[:, 0], buf[:, 1], buf[:, 2], h0, cw, cb, wx, bx, wa, ba, lam)
    return pl.pallas_call(
        _lru_step_body,
        grid=(1,),
        in_specs=[_full(a.shape) for a in args],
        out_specs=[_full((n, w)), _full((n, w))],
        out_shape=[jax.ShapeDtypeStruct((n, w), BF16), jax.ShapeDtypeStruct((n, w), F32)],
        compiler_params=_cparams("arbitrary"),
        name="lru_step",
    )(*args)


def _diff_lambda(lq1_ref, lk1_ref, lq2_ref, lk2_ref, lam_init):
    s1 = jnp.sum(lq1_ref[...] * lk1_ref[...], axis=-1, keepdims=True)
    s2 = jnp.sum(lq2_ref[...] * lk2_ref[...], axis=-1, keepdims=True)
    return jnp.exp(s1) - jnp.exp(s2) + lam_init


def _attn_prompt_body(lq1_ref, lk1_ref, lq2_ref, lk2_ref, sg_ref, gq_ref, gk_ref, q_ref, k_ref, v_ref, o_ref,
                      *, tq, dh, nh, lam_init):
    i = pl.program_id(2)
    hw = 2 * dh
    heads = range(nh)
    lam = _diff_lambda(lq1_ref, lk1_ref, lq2_ref, lk2_ref, lam_init)
    lane = lax.broadcasted_iota(jnp.int32, (tq, hw), 1)
    qq = []
    for h in heads:
        q = q_ref[0, :, h * hw:(h + 1) * hw] * (dh ** -0.5)
        zero = jnp.zeros_like(q)
        qq.append(jnp.concatenate([jnp.where(lane < dh, q, zero), jnp.where(lane >= dh, q, zero)], axis=0))
    rowi = lax.broadcasted_iota(jnp.int32, (2 * tq, tq), 0)
    coli = lax.broadcasted_iota(jnp.int32, (2 * tq, tq), 1)
    causal = coli <= jnp.where(rowi >= tq, rowi - tq, rowi)

    def scores(j, masked):
        start = pl.multiple_of(j * tq, tq)
        kj = k_ref[0, pl.ds(start, tq), :]
        vj = v_ref[0, pl.ds(start, tq), :]
        s = [_bdot_nt(qq[h], kj[:, h * hw:(h + 1) * hw]) for h in heads]
        if masked:
            s = [jnp.where(causal, s[h], NEG_INF) for h in heads]
        return s, [vj[:, h * hw:(h + 1) * hw] for h in heads]

    def finish(l, acc):
        out = []
        for h in heads:
            o = acc[h] / l[h]
            od = o[:tq] - lam * o[tq:]
            out.append(_rms(od, sg_ref[...]) * (1.0 - lam_init))
        o_ref[0] = jnp.concatenate(out, axis=1).astype(BF16)

    bound = (dh ** 0.5) * 1.05 * jnp.max(jnp.abs(gq_ref[...])) * jnp.max(jnp.abs(gk_ref[...]))
    fixed_ok = bound <= ATTN_FIXED_SHIFT_MAX

    @pl.when(fixed_ok)
    def _():
        def block(j, carry, masked):
            s, vj = scores(j, masked)
            p = [jnp.exp(s[h] - bound) for h in heads]
            l = [carry[h][0] + sum(p[h][:, c0:c0 + LANES] for c0 in range(0, tq, LANES)) for h in heads]
            acc = [carry[h][1] + _bdot(p[h], vj[h]) for h in heads]
            return tuple((l[h], acc[h]) for h in heads)

        init = tuple((jnp.zeros((2 * tq, LANES), F32), jnp.zeros((2 * tq, hw), F32)) for _ in heads)
        carry = lax.fori_loop(0, i, lambda j, c: block(j, c, False), init)
        carry = block(i, carry, True)
        finish([jnp.sum(carry[h][0], axis=-1, keepdims=True) for h in heads], [carry[h][1] for h in heads])

    @pl.when(jnp.logical_not(fixed_ok))
    def _():
        def block(j, carry, masked):
            s, vj = scores(j, masked)
            m_new = [jnp.maximum(carry[h][0], jnp.max(s[h], axis=-1, keepdims=True)) for h in heads]
            alpha = [jnp.exp(carry[h][0] - m_new[h]) for h in heads]
            p = [jnp.exp(s[h] - m_new[h]) for h in heads]
            l = [alpha[h] * carry[h][1] + jnp.sum(p[h], axis=-1, keepdims=True) for h in heads]
            acc = [alpha[h] * carry[h][2] + _bdot(p[h], vj[h]) for h in heads]
            return tuple((m_new[h], l[h], acc[h]) for h in heads)

        init = tuple((jnp.full((2 * tq, 1), NEG_INF, F32), jnp.zeros((2 * tq, 1), F32),
                      jnp.zeros((2 * tq, hw), F32)) for _ in heads)
        carry = lax.fori_loop(0, i, lambda j, c: block(j, c, False), init)
        carry = block(i, carry, True)
        finish([carry[h][1] for h in heads], [carry[h][2] for h in heads])


def _attn_prompt(q, k, v, lq1, lk1, lq2, lk2, sub_g, gq, gk, *, lam_init, tq):
    b, t, w = q.shape
    hw = w // H_B
    dh = hw // 2
    vec = _full((1, dh))
    nh = ATTN_HEADS_PER_STEP
    assert tq % LANES == 0
    qblk = pl.BlockSpec((1, tq, nh * hw), lambda bi, hg, i: (bi, i, hg))
    kvblk = pl.BlockSpec((1, t, nh * hw), lambda bi, hg, i: (bi, 0, hg))
    return pl.pallas_call(
        functools.partial(_attn_prompt_body, tq=tq, dh=dh, nh=nh, lam_init=lam_init),
        grid=(b, H_B // nh, t // tq),
        in_specs=[vec, vec, vec, vec, _full((1, hw)), vec, vec, qblk, kvblk, kvblk],
        out_specs=qblk,
        out_shape=jax.ShapeDtypeStruct((b, t, w), BF16),
        compiler_params=_cparams("parallel", "parallel", "arbitrary"),
        name="attn_prompt",
    )(lq1.reshape(1, dh), lk1.reshape(1, dh), lq2.reshape(1, dh), lk2.reshape(1, dh),
      sub_g.reshape(1, hw), gq.reshape(1, dh), gk.reshape(1, dh), q, k, v)


def _attn_decode_body(pt_ref, lq1_ref, lk1_ref, lq2_ref, lk2_ref, sg_ref, q_ref, kn_ref, vn_ref, *refs,
                      n_pages, dh, lam_init):
    del pt_ref
    k_refs = refs[:n_pages]
    v_refs = refs[n_pages:2 * n_pages]
    o_ref = refs[2 * n_pages]
    hw = 2 * dh
    rows = 2 * H_B
    lam = _diff_lambda(lq1_ref, lk1_ref, lq2_ref, lk2_ref, lam_init)

    row = lax.broadcasted_iota(jnp.int32, (rows, hw), 0)
    lane = lax.broadcasted_iota(jnp.int32, (rows, hw), 1)

    def by_head(vec_ref):
        out = jnp.zeros((rows, hw), F32)
        for h in range(H_B):
            piece = vec_ref[0, :, h * hw:(h + 1) * hw].astype(F32)
            out = jnp.where(row // 2 == h, jnp.broadcast_to(piece, (rows, hw)), out)
        return out

    half = (lane >= dh) == (row % 2 == 1)
    wq = jnp.where(half, by_head(q_ref), 0.0) * (dh ** -0.5)
    kn = by_head(kn_ref)
    vn = by_head(vn_ref)
    s_new = jnp.sum(wq * kn, axis=-1, keepdims=True)

    n_rows = k_refs[0].shape[1]
    col_head = lax.broadcasted_iota(jnp.int32, (rows, n_rows), 1) % H_B
    own = col_head == lax.broadcasted_iota(jnp.int32, (rows, n_rows), 0) // 2
    wq_b = wq.astype(BF16)
    scores = [jnp.where(own, _bdot_nt(wq_b, k_refs[p][0]), NEG_INF) for p in range(n_pages)]
    m = s_new
    for s in scores:
        m = jnp.maximum(m, jnp.max(s, axis=-1, keepdims=True))
    p_new = jnp.exp(s_new - m)
    l = p_new
    acc = p_new * vn
    for p in range(n_pages):
        pr = jnp.exp(scores[p] - m)
        l = l + jnp.sum(pr, axis=-1, keepdims=True)
        acc = acc + _bdot(pr, v_refs[p][0])
    o = acc / l
    r_out = lax.broadcasted_iota(jnp.int32, (SUBLANES, rows), 0)
    r_in = lax.broadcasted_iota(jnp.int32, (SUBLANES, rows), 1)
    sel = (jnp.where(r_in == 2 * r_out, 1.0, 0.0) - lam * jnp.where(r_in == 2 * r_out + 1, 1.0, 0.0))
    od = jnp.dot(sel, o, preferred_element_type=F32, precision=HIGHEST)[:H_B]
    o_ref[0] = _rms(od, sg_ref[...]) * (1.0 - lam_init)


def _attn_decode(q, k_new, v_new, cache_k, cache_v, page_table, lq1, lk1, lq2, lk2, sub_g, *, lam_init):
    n, w = q.shape
    hw = w // H_B
    dh = hw // 2
    n_pages = page_table.shape[1]
    n_pool, page = cache_k.shape[0], cache_k.shape[1]
    ck = cache_k.reshape(n_pool, page * H_B, hw)
    cv = cache_v.reshape(n_pool, page * H_B, hw)
    vec = pl.BlockSpec((1, dh), lambda i, pt: (0, 0))
    tok = pl.BlockSpec((1, 1, w), lambda i, pt: (i, 0, 0))

    def page_spec(p):
        return pl.BlockSpec((1, page * H_B, hw), lambda i, pt: (pt[i, p], 0, 0))

    pages = [page_spec(p) for p in range(n_pages)]
    out = pl.pallas_call(
        functools.partial(_attn_decode_body, n_pages=n_pages, dh=dh, lam_init=lam_init),
        grid_spec=pltpu.PrefetchScalarGridSpec(
            num_scalar_prefetch=1,
            grid=(n,),
            in_specs=[vec, vec, vec, vec, pl.BlockSpec((1, hw), lambda i, pt: (0, 0)), tok, tok, tok]
            + pages + pages,
            out_specs=pl.BlockSpec((1, H_B, hw), lambda i, pt: (i, 0, 0)),
        ),
        out_shape=jax.ShapeDtypeStruct((n, H_B, hw), F32),
        compiler_params=_cparams("arbitrary"),
        name="attn_decode",
    )(page_table, lq1.reshape(1, dh), lk1.reshape(1, dh), lq2.reshape(1, dh), lk2.reshape(1, dh),
      sub_g.reshape(1, hw), q.reshape(n, 1, w), k_new.reshape(n, 1, w), v_new.reshape(n, 1, w),
      *([ck] * n_pages), *([cv] * n_pages))
    return out.reshape(n, w)


def _resident(shape):
    nd = len(shape)
    return pl.BlockSpec(shape, lambda *_: (0,) * nd, pipeline_mode=pl.Buffered(1))


def _mixer_residual(x, pair_refs):
    y = None
    for a_ref, w_ref in zip(pair_refs[0::2], pair_refs[1::2]):
        part = jnp.dot(a_ref[...].astype(BF16), w_ref[...], preferred_element_type=F32)
        y = part if y is None else y + part
    return x if y is None else x + y


def _ffn_body(*refs, chunks, n_pairs):
    x_ref, ln_ref = refs[:2]
    wg_ref, wu_ref, wd_ref, o_ref = refs[2 + 2 * n_pairs:]
    x = _mixer_residual(x_ref[...], refs[2:2 + 2 * n_pairs])
    hb = _rms(x, ln_ref[...]).astype(BF16)
    acc = x
    for lo, n in chunks:
        g = jnp.dot(hb, wg_ref[:, lo:lo + n], preferred_element_type=F32)
        u = jnp.dot(hb, wu_ref[:, lo:lo + n], preferred_element_type=F32)
        acc = acc + jnp.dot((_silu(g) * u).astype(BF16), wd_ref[lo:lo + n, :], preferred_element_type=F32)
    o_ref[...] = acc


def _ffn(x, pairs, ln, wg, wu, wd, *, tm):
    m, d = x.shape
    f = wg.shape[1]
    chunks = [(lo, min(FFN_CHUNK, f - lo)) for lo in range(0, f, FFN_CHUNK)]
    row = lambda i: (i, 0)
    pair_specs, pair_args = [], []
    for a, w in pairs:
        pair_specs += [pl.BlockSpec((tm, a.shape[1]), row), _full(w.shape)]
        pair_args += [a, w]
    return pl.pallas_call(
        functools.partial(_ffn_body, chunks=chunks, n_pairs=len(pairs)),
        grid=(m // tm,),
        in_specs=[pl.BlockSpec((tm, d), row), _full((1, d))] + pair_specs
        + [_resident(wg.shape), _resident(wu.shape), _resident(wd.shape)],
        out_specs=pl.BlockSpec((tm, d), row),
        out_shape=jax.ShapeDtypeStruct((m, d), F32),
        compiler_params=_cparams("parallel"),
        name="ffn",
    )(x, ln.reshape(1, d), *pair_args, wg, wu, wd)


def _inproj_odd_body(x_ref, ln_ref, w_ref, wba_ref, alog_ref, dtb_ref, qkv_ref, z_ref, gb_ref,
                     *, n_qkv, n_z, tn):
    hb = _rms(x_ref[...], ln_ref[...]).astype(BF16)
    for c in range(n_qkv // tn):
        qkv_ref[:, c * tn:(c + 1) * tn] = jnp.dot(hb, w_ref[:, c * tn:(c + 1) * tn],
                                                  preferred_element_type=F32)
    for c in range(n_z // tn):
        z_ref[:, c * tn:(c + 1) * tn] = jnp.dot(
            hb, w_ref[:, n_qkv + c * tn:n_qkv + (c + 1) * tn], preferred_element_type=F32).astype(BF16)
    ba = jnp.dot(hb, wba_ref[...], preferred_element_type=F32)
    lane = lax.broadcasted_iota(jnp.int32, ba.shape, 1)
    beta = jax.nn.sigmoid(ba)
    g = -jnp.exp(alog_ref[...]) * _softplus(ba + dtb_ref[...])
    gb_ref[...] = jnp.where(lane < H_C, beta, g)


def _inproj_odd(x, ln, w, a_log, dt_bias, *, tm):
    m, d = x.shape
    n_z = H_C * (d // H_C)
    n_qkv = w.shape[1] - n_z - 2 * H_C
    wba = jnp.zeros((d, LANES), BF16).at[:, :2 * H_C].set(w[:, n_qkv + n_z:])
    alog = jnp.zeros((1, LANES), F32).at[0, H_C:2 * H_C].set(a_log)
    dtb = jnp.zeros((1, LANES), F32).at[0, H_C:2 * H_C].set(dt_bias)
    row = lambda i: (i, 0)
    return pl.pallas_call(
        functools.partial(_inproj_odd_body, n_qkv=n_qkv, n_z=n_z, tn=512),
        grid=(m // tm,),
        in_specs=[pl.BlockSpec((tm, d), row), _full((1, d)), _full(w.shape), _full(wba.shape),
                  _full((1, LANES)), _full((1, LANES))],
        out_specs=[pl.BlockSpec((tm, n_qkv), row), pl.BlockSpec((tm, n_z), row),
                   pl.BlockSpec((tm, LANES), row)],
        out_shape=[jax.ShapeDtypeStruct((m, n_qkv), F32), jax.ShapeDtypeStruct((m, n_z), BF16),
                   jax.ShapeDtypeStruct((m, LANES), F32)],
        compiler_params=_cparams("parallel"),
        name="inproj_odd",
    )(x, ln.reshape(1, d), w, wba, alog, dtb)


def _l2n(x):
    return x * lax.rsqrt(jnp.sum(x * x, axis=-1, keepdims=True) + EPS)


def _delta_prompt_body(qkv_ref, z_ref, gb_ref, cw_ref, og_ref, o_ref, s_ref, prev_s, xc_s, *, tc, dk):
    t = pl.program_id(1)
    c = DELTA_CHUNK
    nk = H_C * dk

    @pl.when(t == 0)
    def _():
        prev_s[...] = jnp.zeros_like(prev_s)
        s_ref[...] = jnp.zeros_like(s_ref)

    x = qkv_ref[0]
    x1, x2, x3 = _shifted_rows(prev_s[...], x)
    xc_s[...] = _silu(x3 * cw_ref[0:1, :] + x2 * cw_ref[1:2, :] + x1 * cw_ref[2:3, :] + x * cw_ref[3:4, :])
    prev_s[...] = x[tc - SUBLANES:, :]

    ri = lax.broadcasted_iota(jnp.int32, (c, c), 0)
    ci = lax.broadcasted_iota(jnp.int32, (c, c), 1)
    incl = ri >= ci
    strict = ri > ci
    tri = jnp.where(incl, 1.0, 0.0)
    eye = jnp.where(ri == ci, 1.0, 0.0)
    heads = range(H_C)
    n_chunks = tc // c

    q, k, v, beta, e_g, decay, g_end, k_dec = ([] for _ in range(8))
    for ic in range(n_chunks):
        rows = slice(ic * c, (ic + 1) * c)
        gbc = gb_ref[0, rows, :]
        g_all = jnp.dot(tri, gbc, preferred_element_type=F32, precision=HIGHEST)
        g_t = g_all.T
        for h in heads:
            g_col = g_all[:, H_C + h:H_C + h + 1]
            g_row = g_t[H_C + h:H_C + h + 1, :]
            g_last = g_all[c - 1:c, H_C + h:H_C + h + 1]
            kn = _l2n(xc_s[rows, nk + h * dk:nk + (h + 1) * dk])
            q.append(_l2n(xc_s[rows, h * dk:(h + 1) * dk]) * (dk ** -0.5))
            k.append(kn)
            v.append(xc_s[rows, 2 * nk + h * dk:2 * nk + (h + 1) * dk])
            beta.append(gbc[:, h:h + 1])
            e_g.append(jnp.exp(g_col))
            decay.append(jnp.where(incl, jnp.exp(jnp.where(incl, g_col - g_row, 0.0)), 0.0))
            g_end.append(jnp.exp(g_last))
            k_dec.append(kn * jnp.exp(g_last - g_col))
    pairs = range(n_chunks * H_C)
    kb = [k[i] * beta[i] for i in pairs]
    kk = [_bdot_nt(jnp.concatenate([kb[i], q[i]], axis=0), k[i]) for i in pairs]
    qk = [kk[i][c:] * decay[i] for i in pairs]
    pw = [-jnp.where(strict, kk[i][:c] * decay[i], 0.0) for i in pairs]
    inv = [eye + pw[i] for i in pairs]
    for _ in range(int(math.log2(c)) - 1):
        pw = [_bdot(pw[i], pw[i]) for i in pairs]
        inv = [inv[i] + _bdot(inv[i], pw[i]) for i in pairs]
    sol = [_bdot(inv[i], jnp.concatenate([v[i] * beta[i], kb[i] * e_g[i]], axis=1)) for i in pairs]
    wq = [jnp.concatenate([sol[i][:, dk:], q[i] * e_g[i]], axis=0).astype(BF16) for i in pairs]

    s = [s_ref[0, h] for h in heads]
    for ic in range(n_chunks):
        rows = slice(ic * c, (ic + 1) * c)
        idx = [ic * H_C + h for h in heads]
        ws = [_bdot(wq[i], s[h]) for h, i in zip(heads, idx)]
        v_new = [sol[i][:, :dk] - ws[h][:c] for h, i in zip(heads, idx)]
        o = [ws[h][c:] + _bdot(qk[i], v_new[h]) for h, i in zip(heads, idx)]
        s = [s[h] * g_end[i] + _bdot_tn(k_dec[i], v_new[h]) for h, i in zip(heads, idx)]
        zt = z_ref[0, rows, :].astype(F32)
        o_ref[0, rows, :] = jnp.concatenate(
            [_rms(o[h], og_ref[...]) * _silu(zt[:, h * dk:(h + 1) * dk]) for h in heads], axis=1).astype(BF16)
    for h in heads:
        s_ref[0, h] = s[h]


def _delta_prompt(qkv, z, gb, cw, og, *, tc):
    b, t, nq = qkv.shape
    nz = z.shape[2]
    dk = nz // H_C
    seq = lambda i, j: (i, j, 0)
    return pl.pallas_call(
        functools.partial(_delta_prompt_body, tc=tc, dk=dk),
        grid=(b, t // tc),
        in_specs=[pl.BlockSpec((1, tc, nq), seq), pl.BlockSpec((1, tc, nz), seq),
                  pl.BlockSpec((1, tc, LANES), seq), _full((CONV_W, nq)), _full((1, dk))],
        out_specs=[pl.BlockSpec((1, tc, nz), seq),
                   pl.BlockSpec((1, H_C, dk, dk), lambda i, j: (i, 0, 0, 0))],
        out_shape=[jax.ShapeDtypeStruct((b, t, nz), BF16), jax.ShapeDtypeStruct((b, H_C, dk, dk), F32)],
        scratch_shapes=[pltpu.VMEM((SUBLANES, nq), F32), pltpu.VMEM((tc, nq), F32)],
        compiler_params=_cparams("parallel", "arbitrary"),
        name="delta_prompt",
    )(qkv, z, gb, cw, og.reshape(1, dk))


def _delta_prep_step_body(x_ref, b0_ref, b1_ref, b2_ref, cw_ref, q_ref, k_ref, v_ref, *, dk):
    xc = _silu(b0_ref[...] * cw_ref[0:1, :] + b1_ref[...] * cw_ref[1:2, :] + b2_ref[...] * cw_ref[2:3, :]
               + x_ref[...] * cw_ref[3:4, :])
    nk = H_C * dk
    for h in range(H_C):
        sl = slice(h * dk, (h + 1) * dk)
        q_ref[:, sl] = _l2n(xc[:, h * dk:(h + 1) * dk]) * (dk ** -0.5)
        k_ref[:, sl] = _l2n(xc[:, nk + h * dk:nk + (h + 1) * dk])
    v_ref[...] = xc[:, 2 * nk:]


def _delta_step_body(q_ref, k_ref, v_ref, gb_ref, z_ref, og_ref, s_ref, o_ref, sn_ref, *, dk, ns):
    row = lax.broadcasted_iota(jnp.int32, (SUBLANES, dk), 0)
    units = [(j, h) for j in range(ns) for h in range(H_C)]
    rows8 = lambda x: jnp.broadcast_to(x, (SUBLANES, dk))
    q = [q_ref[j, :, h * dk:(h + 1) * dk] for j, h in units]
    k = [k_ref[j, :, h * dk:(h + 1) * dk] for j, h in units]
    v = [v_ref[j, :, h * dk:(h + 1) * dk] for j, h in units]
    beta = [gb_ref[j, :, h:h + 1] for j, h in units]
    e_g = [jnp.exp(gb_ref[j, :, H_C + h:H_C + h + 1]) for j, h in units]
    s = [s_ref[j, h] for j, h in units]
    n = range(len(units))
    kq = [jnp.where(row == 0, rows8(k[u]), jnp.where(row == 1, rows8(q[u] * e_g[u]), 0.0)) for u in n]
    ks = [_bdot(kq[u], s[u]) for u in n]
    v_new = [beta[u] * (v[u] - e_g[u] * ks[u][0:1]) for u in n]
    o = [ks[u][1:2] + jnp.sum(q[u] * k[u], axis=-1, keepdims=True) * v_new[u] for u in n]
    k_hi, k_lo = zip(*[_split_bf16(jnp.where(row == 0, rows8(k[u]), 0.0)) for u in n])
    v_hi, v_lo = zip(*[_split_bf16(jnp.where(row == 0, rows8(v_new[u]), 0.0)) for u in n])
    outer = [_bdot_tn(k_hi[u], v_hi[u]) + _bdot_tn(k_hi[u], v_lo[u]) + _bdot_tn(k_lo[u], v_hi[u]) for u in n]
    for u, (j, h) in enumerate(units):
        sn_ref[j, h] = s[u] * e_g[u] + outer[u]
    for j in range(ns):
        zt = z_ref[j].astype(F32)
        o_ref[j] = jnp.concatenate(
            [_rms(o[j * H_C + h], og_ref[...]) * _silu(zt[:, h * dk:(h + 1) * dk]) for h in range(H_C)],
            axis=1).astype(BF16)


def _delta_step(qkv, buf, z, gb, s0, cw, og):
    n, nq = qkv.shape
    nz = z.shape[1]
    dk = nz // H_C
    args = (qkv, buf[:, 0], buf[:, 1], buf[:, 2], cw)
    q, k, v = pl.pallas_call(
        functools.partial(_delta_prep_step_body, dk=dk),
        grid=(1,),
        in_specs=[_full(a.shape) for a in args],
        out_specs=[_full((n, nz))] * 3,
        out_shape=[jax.ShapeDtypeStruct((n, nz), F32)] * 3,
        compiler_params=_cparams("arbitrary"),
        name="delta_prep_step",
    )(*args)
    ns = DELTA_STEP_SEQS if n % DELTA_STEP_SEQS == 0 else 1
    tok = lambda width: pl.BlockSpec((ns, 1, width), lambda i: (i, 0, 0))
    st = pl.BlockSpec((ns, H_C, dk, dk), lambda i: (i, 0, 0, 0))
    o, s_new = pl.pallas_call(
        functools.partial(_delta_step_body, dk=dk, ns=ns),
        grid=(n // ns,),
        in_specs=[tok(nz), tok(nz), tok(nz), tok(LANES), tok(nz), _full((1, dk)), st],
        out_specs=[tok(nz), st],
        out_shape=[jax.ShapeDtypeStruct((n, 1, nz), BF16), jax.ShapeDtypeStruct(s0.shape, F32)],
        compiler_params=_cparams("parallel"),
        name="delta_step",
    )(q.reshape(n, 1, nz), k.reshape(n, 1, nz), v.reshape(n, 1, nz), gb.reshape(n, 1, LANES),
      z.reshape(n, 1, nz), og.reshape(1, dk), s0)
    return o.reshape(n, nz), s_new


ROUTER_ROWS = 1024
SCATTER_ROWS = 512


def _router_body(x_ref, a_ref, wo_ref, ln_ref, rw_ref, xo_ref, hb_ref, comb_ref, rank_ref, rank_t_ref, cnt_ref,
                 *, n_exp, tm):
    tb = min(ROUTER_ROWS, tm)
    lane = lax.broadcasted_iota(jnp.int32, (tb, LANES), 1).astype(F32)
    ri = lax.broadcasted_iota(jnp.int32, (tb, tb), 0)
    ci = lax.broadcasted_iota(jnp.int32, (tb, tb), 1)
    before = jnp.where(ri > ci, 1.0, 0.0).astype(BF16)
    running = jnp.zeros((1, LANES), F32)
    for t0 in range(0, tm, tb):
        rows = slice(t0, t0 + tb)
        x = x_ref[rows, :] + jnp.dot(a_ref[rows, :], wo_ref[...], preferred_element_type=F32)
        xo_ref[rows, :] = x
        h = _rms(x, ln_ref[...])
        hb_ref[rows, :] = h.astype(BF16)
        logits = jnp.dot(h, rw_ref[...], preferred_element_type=F32, precision=HIGHEST)
        logits = jnp.where(lane < n_exp, logits, -jnp.inf)
        m1 = jnp.max(logits, axis=-1, keepdims=True)
        i1 = jnp.min(jnp.where(logits == m1, lane, float(LANES)), axis=-1, keepdims=True)
        rest = jnp.where(lane == i1, -jnp.inf, logits)
        m2 = jnp.max(rest, axis=-1, keepdims=True)
        i2 = jnp.min(jnp.where(rest == m2, lane, float(LANES)), axis=-1, keepdims=True)
        e = jnp.exp(m2 - m1)
        g1 = 1.0 / (1.0 + e)
        g2 = e / (1.0 + e)
        comb = jnp.where(lane == i1, g1, 0.0) + jnp.where(lane == i2, g2, 0.0)
        comb_ref[rows, :] = comb
        sel = comb != 0.0
        self_f = jnp.where(sel, 1.0, 0.0)
        local = jnp.dot(before, self_f.astype(BF16), preferred_element_type=F32)
        rank = jnp.where(sel, local + running, -1.0)
        rank_ref[rows, :] = rank
        rank_t_ref[:, rows] = rank.T[:SUBLANES, :]
        running = running + jnp.sum(self_f, axis=0, keepdims=True)
    cnt_ref[0] = running.astype(jnp.int32)


def _router(x, a, wo, ln, rw, *, tm):
    m, d = x.shape
    n_exp = rw.shape[1]
    assert n_exp <= SUBLANES
    rwp = jnp.zeros((d, LANES), F32).at[:, :n_exp].set(rw)
    row = lambda i: (i, 0)
    return pl.pallas_call(
        functools.partial(_router_body, n_exp=n_exp, tm=tm),
        grid=(m // tm,),
        in_specs=[pl.BlockSpec((tm, d), row), pl.BlockSpec((tm, a.shape[1]), row), _full(wo.shape),
                  _full((1, d)), _full((d, LANES))],
        out_specs=[pl.BlockSpec((tm, d), row), pl.BlockSpec((tm, d), row), pl.BlockSpec((tm, LANES), row),
                   pl.BlockSpec((tm, LANES), row), pl.BlockSpec((SUBLANES, tm), lambda i: (0, i)),
                   pl.BlockSpec((1, 1, LANES), lambda i: (i, 0, 0))],
        out_shape=[jax.ShapeDtypeStruct((m, d), F32), jax.ShapeDtypeStruct((m, d), BF16),
                   jax.ShapeDtypeStruct((m, LANES), F32), jax.ShapeDtypeStruct((m, LANES), F32),
                   jax.ShapeDtypeStruct((SUBLANES, m), F32),
                   jax.ShapeDtypeStruct((m // tm, 1, LANES), jnp.int32)],
        compiler_params=_cparams("parallel"),
        name="router",
    )(x, a, wo, ln.reshape(1, d), rwp)


def _moe_body(cnt_ref, x_hbm, hb_ref, comb_ref, rank_ref, rank_t_ref, wg_ref, wu_ref, wd_ref, o_ref,
              xs_s, ys_s, x_sem, *, tm):
    i, e, f = pl.program_id(0), pl.program_id(1), pl.program_id(2)
    n_f = pl.num_programs(2)
    count = cnt_ref[i, e]
    r = MOE_ROWS

    @pl.when(jnp.logical_and(e == 0, f == 0))
    def _():
        rows = pl.ds(pl.multiple_of(i * tm, tm), tm)
        copy = pltpu.make_async_copy(x_hbm.at[rows, :], o_ref, x_sem)
        copy.start()
        copy.wait()

    def gather(lo, n):
        want = (lax.broadcasted_iota(jnp.int32, (n, tm), 0) + lo).astype(F32)
        onehot = jnp.where(rank_t_ref[pl.ds(e, 1), :] == want, 1.0, 0.0).astype(BF16)
        xs_s[lo:lo + n, :] = jnp.dot(onehot, hb_ref[...], preferred_element_type=F32).astype(BF16)

    def ffn(lo, n):
        xs = xs_s[lo:lo + n, :]
        g = jnp.dot(xs, wg_ref[0], preferred_element_type=F32)
        u = jnp.dot(xs, wu_ref[0], preferred_element_type=F32)
        y = jnp.dot((_silu(g) * u).astype(BF16), wd_ref[0], preferred_element_type=F32)

        @pl.when(f == 0)
        def _():
            ys_s[lo:lo + n, :] = y

        @pl.when(f > 0)
        def _():
            ys_s[lo:lo + n, :] += y

    def scatter(lo, n):
        tb = min(SCATTER_ROWS, tm)
        pick = lax.broadcasted_iota(jnp.int32, (tb, LANES), 1) == e
        want = (lax.broadcasted_iota(jnp.int32, (tb, n), 1) + lo).astype(F32)
        yb = ys_s[lo:lo + n, :].astype(BF16)
        for t0 in range(0, tm, tb):
            rank_col = jnp.sum(jnp.where(pick, rank_ref[t0:t0 + tb, :], 0.0), axis=-1, keepdims=True)
            gate_col = jnp.sum(jnp.where(pick, comb_ref[t0:t0 + tb, :], 0.0), axis=-1, keepdims=True)
            onehot = jnp.where(rank_col == want, 1.0, 0.0).astype(BF16)
            o_ref[t0:t0 + tb, :] += gate_col * jnp.dot(onehot, yb, preferred_element_type=F32)

    grp = min(2 * r, tm)
    for lo in range(0, tm, grp):
        if grp == 2 * r:
            @pl.when(count > lo + r)
            def _(lo=lo):
                @pl.when(f == 0)
                def _():
                    gather(lo, 2 * r)
                ffn(lo, 2 * r)

            @pl.when(jnp.logical_and(count > lo, count <= lo + r))
            def _(lo=lo):
                @pl.when(f == 0)
                def _():
                    gather(lo, r)
                    ys_s[lo + r:lo + 2 * r, :] = jnp.zeros((r, ys_s.shape[1]), F32)
                ffn(lo, r)
        else:
            @pl.when(count > lo)
            def _(lo=lo):
                @pl.when(f == 0)
                def _():
                    gather(lo, r)
                ffn(lo, r)

        @pl.when(jnp.logical_and(f == n_f - 1, count > lo))
        def _(lo=lo):
            scatter(lo, grp)


def _moe(x, hb, comb, rank, rank_t, cnt, wg, wu, wd, *, tm, tf):
    m, d = x.shape
    n_exp, _, f = wg.shape
    assert tm % MOE_ROWS == 0 and (tm <= MOE_ROWS or tm % (2 * MOE_ROWS) == 0)
    tok = lambda width: pl.BlockSpec((tm, width), lambda i, e, j, cnt: (i, 0))
    return pl.pallas_call(
        functools.partial(_moe_body, tm=tm),
        grid_spec=pltpu.PrefetchScalarGridSpec(
            num_scalar_prefetch=1,
            grid=(m // tm, n_exp, f // tf),
            in_specs=[pl.BlockSpec(memory_space=pl.ANY), tok(d), tok(LANES), tok(LANES),
                      pl.BlockSpec((SUBLANES, tm), lambda i, e, j, cnt: (0, i)),
                      pl.BlockSpec((1, d, tf), lambda i, e, j, cnt: (e, 0, j)),
                      pl.BlockSpec((1, d, tf), lambda i, e, j, cnt: (e, 0, j)),
                      pl.BlockSpec((1, tf, d), lambda i, e, j, cnt: (e, j, 0))],
            out_specs=tok(d),
            scratch_shapes=[pltpu.VMEM((tm, d), BF16), pltpu.VMEM((tm, d), F32), pltpu.SemaphoreType.DMA(())],
        ),
        out_shape=jax.ShapeDtypeStruct((m, d), F32),
        compiler_params=_cparams("parallel", "arbitrary", "arbitrary", vmem_limit_bytes=MOE_VMEM_LIMIT_BYTES),
        name="moe",
    )(cnt, x, hb, comb, rank, rank_t, wg, wu, wd)


def _moe_layer(x, a, wo, ln, rw, wg, wu, wd, *, tm, tf):
    x, hb, comb, rank, rank_t, cnt = _router(x, a, wo, ln, rw, tm=tm)
    return _moe(x, hb, comb, rank, rank_t, cnt.reshape(cnt.shape[0], LANES), wg, wu, wd, tm=tm, tf=tf)


def _block_diag(w):
    n, c, d = w.shape
    eye = jnp.eye(n, dtype=w.dtype)
    return (eye[:, None, :, None] * w[:, :, None, :]).reshape(n * c, n * d)


def _tile_rows(m, pref):
    return pref if m % pref == 0 else m


def kernel(x_prompt, x_sample, cache_k, cache_v, state_lru_conv, state_lru_h, state_delta_conv, state_delta_s, page_table, ln_mix_e, w_in_e, conv_lru_w, conv_lru_b, lru_wx, lru_bx, lru_wa, lru_ba, lru_lambda, qk_gain_q, qk_gain_k, lam_q1, lam_k1, lam_q2, lam_k2, subln_g, w_out_e, ln_ffn_e, ffn_wg, ffn_wu, ffn_wd, ln_mix_o, w_in_o, conv_delta_w, a_log, dt_bias, onorm_g, w_out_o, ln_ffn_o, router_w, moe_wg, moe_wu, moe_wd):
    bp, tp, d = x_prompt.shape
    bs = x_sample.shape[0]
    n_even, n_odd = w_in_e.shape[0], w_in_o.shape[0]
    lru_w = conv_lru_w.shape[-1]
    att_w = cache_k.shape[-1] * cache_k.shape[-2]
    qkv_c = conv_delta_w.shape[-1]
    dk = onorm_g.shape[-1]
    assert x_sample.shape[1] == 1 and lru_w == att_w and w_in_e.shape[-1] == 5 * lru_w

    xp = x_prompt.reshape(bp * tp, d)
    xs = x_sample.reshape(bs, d)
    mp = bp * tp
    tmp = _tile_rows(mp, 512)
    outs = {k: [] for k in ("kp", "vp", "cp", "hp", "dcp", "dsp", "ks", "vs", "cs", "hs", "dcs", "dss")}
    bf = lambda a: a.astype(BF16)
    vec = lambda a: a.reshape(1, -1)

    for layer in range(n_even + n_odd):
        i = layer // 2
        if layer % 2 == 0:
            lam_init = 0.8 - 0.6 * math.exp(-0.3 * layer)
            w_in, w_out = bf(w_in_e[i]), bf(w_out_e[i])
            lru_args = (conv_lru_w[i], vec(conv_lru_b[i]), bf(_block_diag(lru_wx[i])), vec(lru_bx[i]),
                        bf(_block_diag(lru_wa[i])), vec(lru_ba[i]), vec(lru_lambda[i]))
            lam_args = (lam_q1[i], lam_k1[i], lam_q2[i], lam_k2[i], subln_g[i])
            wg, wu, wd = bf(ffn_wg[i]), bf(ffn_wu[i]), bf(ffn_wd[i])

            gg, xr, qn, k_rows, v_rows, kb, vb = _inproj_even(xp, ln_mix_e[i], w_in, qk_gain_q[i],
                                                              qk_gain_k[i], tm=tmp)
            seq = lambda a: a.reshape(bp, tp, -1)
            ya, h_last = _lru_prompt(seq(xr), seq(gg), *lru_args, tt=_tile_rows(tp, 512))
            o = _attn_prompt(seq(qn), seq(kb), seq(vb), *lam_args, qk_gain_q[i], qk_gain_k[i],
                             lam_init=lam_init, tq=_tile_rows(tp, 256))
            xp = _ffn(xp, [(ya.reshape(mp, lru_w), w_out[:lru_w]), (o.reshape(mp, att_w), w_out[lru_w:])],
                      ln_ffn_e[i], wg, wu, wd, tm=tmp)
            outs["kp"].append(k_rows.reshape(bp, tp, H_B, att_w // H_B))
            outs["vp"].append(v_rows.reshape(bp, tp, H_B, att_w // H_B))
            outs["cp"].append(seq(xr)[:, tp - (CONV_W - 1):])
            outs["hp"].append(h_last.reshape(bp, lru_w))

            gg, xr, qn, k_rows, v_rows, _, _ = _inproj_even(xs, ln_mix_e[i], w_in, qk_gain_q[i],
                                                            qk_gain_k[i], tm=bs)
            ya, h_new = _lru_step(xr, gg, state_lru_conv[i], state_lru_h[i], *lru_args)
            o = _attn_decode(qn, k_rows.reshape(bs, att_w), v_rows.reshape(bs, att_w), cache_k[i], cache_v[i],
                             page_table, *lam_args,
                             lam_init=lam_init)
            xs = _ffn(xs, [(ya, w_out[:lru_w]), (o, w_out[lru_w:])], ln_ffn_e[i], wg, wu, wd, tm=bs)
            outs["ks"].append(k_rows.reshape(bs, 1, H_B, att_w // H_B))
            outs["vs"].append(v_rows.reshape(bs, 1, H_B, att_w // H_B))
            outs["cs"].append(jnp.concatenate([state_lru_conv[i][:, 1:], xr[:, None]], axis=1))
            outs["hs"].append(h_new)
        else:
            w_in, w_out = bf(w_in_o[i]), bf(w_out_o[i])
            wg, wu, wd = bf(moe_wg[i]), bf(moe_wu[i]), bf(moe_wd[i])
            tf = MOE_FF_TILE

            qkv, z, gb = _inproj_odd(xp, ln_mix_o[i], w_in, a_log[i], dt_bias[i], tm=tmp)
            seq = lambda a: a.reshape(bp, tp, -1)
            o, s_fin = _delta_prompt(seq(qkv), seq(z), seq(gb), conv_delta_w[i], onorm_g[i],
                                     tc=_tile_rows(tp, 256))
            xp = _moe_layer(xp, o.reshape(mp, -1), w_out, ln_ffn_o[i], router_w[i], wg, wu, wd,
                            tm=_tile_rows(mp, MOE_TOKENS), tf=tf)
            outs["dcp"].append(seq(qkv)[:, tp - (CONV_W - 1):])
            outs["dsp"].append(s_fin)

            qkv, z, gb = _inproj_odd(xs, ln_mix_o[i], w_in, a_log[i], dt_bias[i], tm=bs)
            o, s_new = _delta_step(qkv, state_delta_conv[i], z, gb, state_delta_s[i], conv_delta_w[i],
                                   onorm_g[i])
            xs = _moe_layer(xs, o, w_out, ln_ffn_o[i], router_w[i], wg, wu, wd, tm=bs, tf=tf)
            outs["dcs"].append(jnp.concatenate([state_delta_conv[i][:, 1:], qkv[:, None]], axis=1))
            outs["dss"].append(s_new)

    st = lambda k: jnp.stack(outs[k])
    return (xp.reshape(bp, tp, d), xs.reshape(bs, 1, d),
            st("kp"), st("vp"), st("cp"), st("hp"), st("dcp"), st("dsp"),
            st("ks"), st("vs"), st("cs"), st("hs"), st("dcs"), st("dss"))
```

```python
import functools
import math

import jax
import jax.numpy as jnp
from jax import lax
from jax.experimental import pallas as pl
from jax.experimental.pallas import tpu as pltpu

F32 = jnp.float32
BF16 = jnp.bfloat16
HIGHEST = lax.Precision.HIGHEST

CONV_W = 4
RG_C = 8.0
H_B = 4
H_C = 8
TOP_K = 2
EPS = 1e-6
NEG_INF = -1e30

LANES = 128
SUBLANES = 8
VMEM_LIMIT_BYTES = 52 * 1024 * 1024
MOE_VMEM_LIMIT_BYTES = 60 * 1024 * 1024

LRU_SEQS_PER_STEP = 4
DELTA_CHUNK = 64
DELTA_STEP_SEQS = 4
ATTN_HEADS_PER_STEP = 4
ATTN_FIXED_SHIFT_MAX = 40.0
MOE_ROWS = 128
MOE_HEAD_ROWS = 512
MOE_ROW_STEP = 64
MOE_TOKENS = 1024
MOE_FF_TILE = 1792
FFN_CHUNK = 512


def _cparams(*sem, vmem_limit_bytes=VMEM_LIMIT_BYTES):
    return pltpu.CompilerParams(dimension_semantics=sem, vmem_limit_bytes=vmem_limit_bytes)


def _full(shape):
    nd = len(shape)
    return pl.BlockSpec(shape, lambda *_: (0,) * nd)


def _rms(x, g):
    return x * lax.rsqrt(jnp.mean(x * x, axis=-1, keepdims=True) + EPS) * g


def _softplus(x):
    return jnp.maximum(x, 0.0) + jnp.log1p(jnp.exp(-jnp.abs(x)))


def _silu(x):
    return x * jax.nn.sigmoid(x)


def _bdot(a, b):
    return jnp.dot(a.astype(BF16), b.astype(BF16), preferred_element_type=F32)


def _bdot_nt(a, b):
    return lax.dot_general(a.astype(BF16), b.astype(BF16), (((1,), (1,)), ((), ())),
                           preferred_element_type=F32)


def _bdot_tn(a, b):
    return lax.dot_general(a.astype(BF16), b.astype(BF16), (((0,), (0,)), ((), ())),
                           preferred_element_type=F32)


def _split_bf16(x):
    hi = x.astype(BF16)
    return hi, (x - hi.astype(F32)).astype(BF16)


def _shifted_rows(prev8, x):
    full = jnp.concatenate([prev8, x], axis=0)
    return [pltpu.roll(full, s, axis=0)[SUBLANES:] for s in (1, 2, 3)]


def _inproj_even_body(x_ref, ln_ref, w_ref, gq_ref, gk_ref, seg_ref,
                      gg_ref, xr_ref, q_ref, k_ref, v_ref, kb_ref, vb_ref, *, width, dh):
    hb = _rms(x_ref[...], ln_ref[...]).astype(BF16)

    def proj(j):
        return jnp.dot(hb, w_ref[:, j * width:(j + 1) * width], preferred_element_type=F32)

    def qk_norm(t, g):
        ms = _bdot(t * t, seg_ref[...]) * (1.0 / dh)
        return t * lax.rsqrt(ms + EPS) * g

    gg_ref[...] = jax.nn.gelu(proj(0)).astype(BF16)
    xr_ref[...] = proj(1)
    q_ref[...] = qk_norm(proj(2), gq_ref[...]).astype(BF16)
    k = qk_norm(proj(3), gk_ref[...])
    kb_ref[...] = k.astype(BF16)
    v = proj(4)
    vb_ref[...] = v.astype(BF16)
    hw = width // H_B
    for h in range(H_B):
        k_ref[:, h, :] = k[:, h * hw:(h + 1) * hw]
        v_ref[:, h, :] = v[:, h * hw:(h + 1) * hw]


def _inproj_even(x, ln, w, gq, gk, *, tm):
    m, d = x.shape
    width = w.shape[1] // 5
    dh = gq.shape[-1]
    reps = width // dh
    seg = jnp.kron(jnp.eye(reps, dtype=F32), jnp.ones((dh, dh), F32)).astype(BF16)
    gq_t = jnp.tile(gq.reshape(1, dh), (1, reps))
    gk_t = jnp.tile(gk.reshape(1, dh), (1, reps))
    row = lambda i: (i, 0)
    blk = pl.BlockSpec((tm, width), row)
    f32o = jax.ShapeDtypeStruct((m, width), F32)
    b16o = jax.ShapeDtypeStruct((m, width), BF16)
    kvblk = pl.BlockSpec((tm, H_B, width // H_B), lambda i: (i, 0, 0))
    kvo = jax.ShapeDtypeStruct((m, H_B, width // H_B), F32)
    return pl.pallas_call(
        functools.partial(_inproj_even_body, width=width, dh=dh),
        grid=(m // tm,),
        in_specs=[pl.BlockSpec((tm, d), row), _full((1, d)), _full(w.shape),
                  _full((1, width)), _full((1, width)), _full((width, width))],
        out_specs=[blk, blk, blk, kvblk, kvblk, blk, blk],
        out_shape=[b16o, f32o, b16o, kvo, kvo, b16o, b16o],
        compiler_params=_cparams("parallel"),
        name="inproj_even",
    )(x, ln.reshape(1, d), w, gq_t, gk_t, seg)


def _lru_coeffs(xc, wx_ref, bx_ref, wa_ref, ba_ref, lam_ref):
    xb = xc.astype(BF16)
    gate_x = jax.nn.sigmoid(jnp.dot(xb, wx_ref[...], preferred_element_type=F32) + bx_ref[...])
    gate_a = jax.nn.sigmoid(jnp.dot(xb, wa_ref[...], preferred_element_type=F32) + ba_ref[...])
    log_a = -RG_C * gate_a * _softplus(-lam_ref[...])
    a = jnp.exp(log_a)
    th = jnp.tanh(log_a)
    one_minus_a2 = -2.0 * th / (1.0 - th)
    b = jnp.sqrt(one_minus_a2) * (gate_x * xc)
    return a, b


def _lru_prompt_body(xr_ref, gg_ref, cw_ref, cb_ref, wx_ref, bx_ref, wa_ref, ba_ref, lam_ref,
                     ya_ref, hlast_ref, prev_s, h_s, a_s, b_s, *, tt, nb):
    t = pl.program_id(1)

    @pl.when(t == 0)
    def _():
        prev_s[...] = jnp.zeros_like(prev_s)
        h_s[...] = jnp.zeros_like(h_s)

    for b in range(nb):
        xr = xr_ref[b]
        x1, x2, x3 = _shifted_rows(prev_s[b], xr)
        xc = (x3 * cw_ref[0:1, :] + x2 * cw_ref[1:2, :] + x1 * cw_ref[2:3, :] + xr * cw_ref[3:4, :]
              + cb_ref[...])
        prev_s[b] = xr[tt - SUBLANES:, :]
        a, bb = _lru_coeffs(xc, wx_ref, bx_ref, wa_ref, ba_ref, lam_ref)
        a_s[b] = a
        b_s[b] = bb

    def step(i, hs):
        out = []
        for b in range(nb):
            h = a_s[b, pl.ds(i, 1), :] * hs[b] + b_s[b, pl.ds(i, 1), :]
            b_s[b, pl.ds(i, 1), :] = h
            out.append(h)
        return tuple(out)

    hs = lax.fori_loop(0, tt, step, tuple(h_s[b] for b in range(nb)), unroll=8)
    for b in range(nb):
        h_s[b] = hs[b]
        ya_ref[b] = (gg_ref[b].astype(F32) * b_s[b]).astype(BF16)
        hlast_ref[b] = hs[b]


def _lru_prompt(xr, gg, cw, cb, wx, bx, wa, ba, lam, *, tt):
    b, t, w = xr.shape
    nb = LRU_SEQS_PER_STEP if b % LRU_SEQS_PER_STEP == 0 else 1
    seq = lambda i, j: (i, j, 0)
    vec = _full((1, w))
    return pl.pallas_call(
        functools.partial(_lru_prompt_body, tt=tt, nb=nb),
        grid=(b // nb, t // tt),
        in_specs=[pl.BlockSpec((nb, tt, w), seq), pl.BlockSpec((nb, tt, w), seq), _full((CONV_W, w)), vec,
                  _full((w, w)), vec, _full((w, w)), vec, vec],
        out_specs=[pl.BlockSpec((nb, tt, w), seq), pl.BlockSpec((nb, 1, w), lambda i, j: (i, 0, 0))],
        out_shape=[jax.ShapeDtypeStruct((b, t, w), BF16), jax.ShapeDtypeStruct((b, 1, w), F32)],
        scratch_shapes=[pltpu.VMEM((nb, SUBLANES, w), F32), pltpu.VMEM((nb, 1, w), F32),
                        pltpu.VMEM((nb, tt, w), F32), pltpu.VMEM((nb, tt, w), F32)],
        compiler_params=_cparams("parallel", "arbitrary"),
        name="lru_prompt",
    )(xr, gg, cw, cb, wx, bx, wa, ba, lam)


def _lru_step_body(xr_ref, gg_ref, b0_ref, b1_ref, b2_ref, h0_ref, cw_ref, cb_ref,
                   wx_ref, bx_ref, wa_ref, ba_ref, lam_ref, ya_ref, h_ref):
    xr = xr_ref[...]
    xc = (b0_ref[...] * cw_ref[0:1, :] + b1_ref[...] * cw_ref[1:2, :] + b2_ref[...] * cw_ref[2:3, :]
          + xr * cw_ref[3:4, :] + cb_ref[...])
    a, b = _lru_coeffs(xc, wx_ref, bx_ref, wa_ref, ba_ref, lam_ref)
    h = a * h0_ref[...] + b
    h_ref[...] = h
    ya_ref[...] = (gg_ref[...].astype(F32) * h).astype(BF16)


def _lru_step(xr, gg, buf, h0, cw, cb, wx, bx, wa, ba, lam):
    n, w = xr.shape
    args = (xr, gg, buf[:, 0], buf[:, 1], buf[:, 2], h0, cw, cb, wx, bx, wa, ba, lam)
    return pl.pallas_call(
        _lru_step_body,
        grid=(1,),
        in_specs=[_full(a.shape) for a in args],
        out_specs=[_full((n, w)), _full((n, w))],
        out_shape=[jax.ShapeDtypeStruct((n, w), BF16), jax.ShapeDtypeStruct((n, w), F32)],
        compiler_params=_cparams("arbitrary"),
        name="lru_step",
    )(*args)


def _diff_lambda(lq1_ref, lk1_ref, lq2_ref, lk2_ref, lam_init):
    s1 = jnp.sum(lq1_ref[...] * lk1_ref[...], axis=-1, keepdims=True)
    s2 = jnp.sum(lq2_ref[...] * lk2_ref[...], axis=-1, keepdims=True)
    return jnp.exp(s1) - jnp.exp(s2) + lam_init


def _attn_prompt_body(lq1_ref, lk1_ref, lq2_ref, lk2_ref, sg_ref, gq_ref, gk_ref, q_ref, k_ref, v_ref, o_ref,
                      *, tq, dh, nh, lam_init):
    i = pl.program_id(2)
    hw = 2 * dh
    heads = range(nh)
    lam = _diff_lambda(lq1_ref, lk1_ref, lq2_ref, lk2_ref, lam_init)
    lane = lax.broadcasted_iota(jnp.int32, (tq, hw), 1)
    qq = []
    for h in heads:
        q = q_ref[0, :, h * hw:(h + 1) * hw] * (dh ** -0.5)
        zero = jnp.zeros_like(q)
        qq.append(jnp.concatenate([jnp.where(lane < dh, q, zero), jnp.where(lane >= dh, q, zero)], axis=0))
    rowi = lax.broadcasted_iota(jnp.int32, (2 * tq, tq), 0)
    coli = lax.broadcasted_iota(jnp.int32, (2 * tq, tq), 1)
    causal = coli <= jnp.where(rowi >= tq, rowi - tq, rowi)

    def scores(j, masked):
        start = pl.multiple_of(j * tq, tq)
        kj = k_ref[0, pl.ds(start, tq), :]
        vj = v_ref[0, pl.ds(start, tq), :]
        s = [_bdot_nt(qq[h], kj[:, h * hw:(h + 1) * hw]) for h in heads]
        if masked:
            s = [jnp.where(causal, s[h], NEG_INF) for h in heads]
        return s, [vj[:, h * hw:(h + 1) * hw] for h in heads]

    def finish(l, acc):
        out = []
        for h in heads:
            o = acc[h] / l[h]
            od = o[:tq] - lam * o[tq:]
            out.append(_rms(od, sg_ref[...]) * (1.0 - lam_init))
        o_ref[0] = jnp.concatenate(out, axis=1).astype(BF16)

    bound = (dh ** 0.5) * 1.05 * jnp.max(jnp.abs(gq_ref[...])) * jnp.max(jnp.abs(gk_ref[...]))
    fixed_ok = bound <= ATTN_FIXED_SHIFT_MAX

    @pl.when(fixed_ok)
    def _():
        def block(j, carry, masked):
            s, vj = scores(j, masked)
            p = [jnp.exp(s[h] - bound) for h in heads]
            l = [carry[h][0] + sum(p[h][:, c0:c0 + LANES] for c0 in range(0, tq, LANES)) for h in heads]
            acc = [carry[h][1] + _bdot(p[h], vj[h]) for h in heads]
            return tuple((l[h], acc[h]) for h in heads)

        init = tuple((jnp.zeros((2 * tq, LANES), F32), jnp.zeros((2 * tq, hw), F32)) for _ in heads)
        carry = lax.fori_loop(0, i, lambda j, c: block(j, c, False), init)
        carry = block(i, carry, True)
        finish([jnp.sum(carry[h][0], axis=-1, keepdims=True) for h in heads], [carry[h][1] for h in heads])

    @pl.when(jnp.logical_not(fixed_ok))
    def _():
        def block(j, carry, masked):
            s, vj = scores(j, masked)
            m_new = [jnp.maximum(carry[h][0], jnp.max(s[h], axis=-1, keepdims=True)) for h in heads]
            alpha = [jnp.exp(carry[h][0] - m_new[h]) for h in heads]
            p = [jnp.exp(s[h] - m_new[h]) for h in heads]
            l = [alpha[h] * carry[h][1] + jnp.sum(p[h], axis=-1, keepdims=True) for h in heads]
            acc = [alpha[h] * carry[h][2] + _bdot(p[h], vj[h]) for h in heads]
            return tuple((m_new[h], l[h], acc[h]) for h in heads)

        init = tuple((jnp.full((2 * tq, 1), NEG_INF, F32), jnp.zeros((2 * tq, 1), F32),
                      jnp.zeros((2 * tq, hw), F32)) for _ in heads)
        carry = lax.fori_loop(0, i, lambda j, c: block(j, c, False), init)
        carry = block(i, carry, True)
        finish([carry[h][1] for h in heads], [carry[h][2] for h in heads])


def _attn_prompt(q, k, v, lq1, lk1, lq2, lk2, sub_g, gq, gk, *, lam_init, tq):
    b, t, w = q.shape
    hw = w // H_B
    dh = hw // 2
    vec = _full((1, dh))
    nh = ATTN_HEADS_PER_STEP
    assert tq % LANES == 0
    qblk = pl.BlockSpec((1, tq, nh * hw), lambda bi, hg, i: (bi, i, hg))
    kvblk = pl.BlockSpec((1, t, nh * hw), lambda bi, hg, i: (bi, 0, hg))
    return pl.pallas_call(
        functools.partial(_attn_prompt_body, tq=tq, dh=dh, nh=nh, lam_init=lam_init),
        grid=(b, H_B // nh, t // tq),
        in_specs=[vec, vec, vec, vec, _full((1, hw)), vec, vec, qblk, kvblk, kvblk],
        out_specs=qblk,
        out_shape=jax.ShapeDtypeStruct((b, t, w), BF16),
        compiler_params=_cparams("parallel", "parallel", "arbitrary"),
        name="attn_prompt",
    )(lq1.reshape(1, dh), lk1.reshape(1, dh), lq2.reshape(1, dh), lk2.reshape(1, dh),
      sub_g.reshape(1, hw), gq.reshape(1, dh), gk.reshape(1, dh), q, k, v)


def _attn_decode_body(pt_ref, lq1_ref, lk1_ref, lq2_ref, lk2_ref, sg_ref, q_ref, kn_ref, vn_ref, *refs,
                      n_pages, dh, lam_init):
    del pt_ref
    k_refs = refs[:n_pages]
    v_refs = refs[n_pages:2 * n_pages]
    o_ref = refs[2 * n_pages]
    hw = 2 * dh
    rows = 2 * H_B
    lam = _diff_lambda(lq1_ref, lk1_ref, lq2_ref, lk2_ref, lam_init)

    row = lax.broadcasted_iota(jnp.int32, (rows, hw), 0)
    lane = lax.broadcasted_iota(jnp.int32, (rows, hw), 1)

    def by_head(vec_ref):
        out = jnp.zeros((rows, hw), F32)
        for h in range(H_B):
            piece = vec_ref[0, :, h * hw:(h + 1) * hw].astype(F32)
            out = jnp.where(row // 2 == h, jnp.broadcast_to(piece, (rows, hw)), out)
        return out

    half = (lane >= dh) == (row % 2 == 1)
    wq = jnp.where(half, by_head(q_ref), 0.0) * (dh ** -0.5)
    kn = by_head(kn_ref)
    vn = by_head(vn_ref)
    s_new = jnp.sum(wq * kn, axis=-1, keepdims=True)

    n_rows = k_refs[0].shape[1]
    col_head = lax.broadcasted_iota(jnp.int32, (rows, n_rows), 1) % H_B
    own = col_head == lax.broadcasted_iota(jnp.int32, (rows, n_rows), 0) // 2
    wq_b = wq.astype(BF16)
    scores = [jnp.where(own, _bdot_nt(wq_b, k_refs[p][0]), NEG_INF) for p in range(n_pages)]
    m = s_new
    for s in scores:
        m = jnp.maximum(m, jnp.max(s, axis=-1, keepdims=True))
    p_new = jnp.exp(s_new - m)
    l = p_new
    acc = p_new * vn
    for p in range(n_pages):
        pr = jnp.exp(scores[p] - m)
        l = l + jnp.sum(pr, axis=-1, keepdims=True)
        acc = acc + _bdot(pr, v_refs[p][0])
    o = acc / l
    r_out = lax.broadcasted_iota(jnp.int32, (SUBLANES, rows), 0)
    r_in = lax.broadcasted_iota(jnp.int32, (SUBLANES, rows), 1)
    sel = (jnp.where(r_in == 2 * r_out, 1.0, 0.0) - lam * jnp.where(r_in == 2 * r_out + 1, 1.0, 0.0))
    od = jnp.dot(sel, o, preferred_element_type=F32, precision=HIGHEST)[:H_B]
    o_ref[0] = _rms(od, sg_ref[...]) * (1.0 - lam_init)


def _attn_decode(q, k_new, v_new, cache_k, cache_v, page_table, lq1, lk1, lq2, lk2, sub_g, *, lam_init):
    n, w = q.shape
    hw = w // H_B
    dh = hw // 2
    n_pages = page_table.shape[1]
    n_pool, page = cache_k.shape[0], cache_k.shape[1]
    ck = cache_k.reshape(n_pool, page * H_B, hw)
    cv = cache_v.reshape(n_pool, page * H_B, hw)
    vec = pl.BlockSpec((1, dh), lambda i, pt: (0, 0))
    tok = pl.BlockSpec((1, 1, w), lambda i, pt: (i, 0, 0))

    def page_spec(p):
        return pl.BlockSpec((1, page * H_B, hw), lambda i, pt: (pt[i, p], 0, 0))

    pages = [page_spec(p) for p in range(n_pages)]
    out = pl.pallas_call(
        functools.partial(_attn_decode_body, n_pages=n_pages, dh=dh, lam_init=lam_init),
        grid_spec=pltpu.PrefetchScalarGridSpec(
            num_scalar_prefetch=1,
            grid=(n,),
            in_specs=[vec, vec, vec, vec, pl.BlockSpec((1, hw), lambda i, pt: (0, 0)), tok, tok, tok]
            + pages + pages,
            out_specs=pl.BlockSpec((1, H_B, hw), lambda i, pt: (i, 0, 0)),
        ),
        out_shape=jax.ShapeDtypeStruct((n, H_B, hw), F32),
        compiler_params=_cparams("arbitrary"),
        name="attn_decode",
    )(page_table, lq1.reshape(1, dh), lk1.reshape(1, dh), lq2.reshape(1, dh), lk2.reshape(1, dh),
      sub_g.reshape(1, hw), q.reshape(n, 1, w), k_new.reshape(n, 1, w), v_new.reshape(n, 1, w),
      *([ck] * n_pages), *([cv] * n_pages))
    return out.reshape(n, w)


def _resident(shape):
    nd = len(shape)
    return pl.BlockSpec(shape, lambda *_: (0,) * nd, pipeline_mode=pl.Buffered(1))


def _mixer_residual(x, pair_refs):
    y = None
    for a_ref, w_ref in zip(pair_refs[0::2], pair_refs[1::2]):
        part = jnp.dot(a_ref[...].astype(BF16), w_ref[...], preferred_element_type=F32)
        y = part if y is None else y + part
    return x if y is None else x + y


def _ffn_body(*refs, chunks, n_pairs):
    x_ref, ln_ref = refs[:2]
    wg_ref, wu_ref, wd_ref, o_ref = refs[2 + 2 * n_pairs:]
    x = _mixer_residual(x_ref[...], refs[2:2 + 2 * n_pairs])
    hb = _rms(x, ln_ref[...]).astype(BF16)
    acc = x
    for lo, n in chunks:
        g = jnp.dot(hb, wg_ref[:, lo:lo + n], preferred_element_type=F32)
        u = jnp.dot(hb, wu_ref[:, lo:lo + n], preferred_element_type=F32)
        acc = acc + jnp.dot((_silu(g) * u).astype(BF16), wd_ref[lo:lo + n, :], preferred_element_type=F32)
    o_ref[...] = acc


def _ffn(x, pairs, ln, wg, wu, wd, *, tm):
    m, d = x.shape
    f = wg.shape[1]
    chunks = [(lo, min(FFN_CHUNK, f - lo)) for lo in range(0, f, FFN_CHUNK)]
    row = lambda i: (i, 0)
    pair_specs, pair_args = [], []
    for a, w in pairs:
        pair_specs += [pl.BlockSpec((tm, a.shape[1]), row), _full(w.shape)]
        pair_args += [a, w]
    return pl.pallas_call(
        functools.partial(_ffn_body, chunks=chunks, n_pairs=len(pairs)),
        grid=(m // tm,),
        in_specs=[pl.BlockSpec((tm, d), row), _full((1, d))] + pair_specs
        + [_resident(wg.shape), _resident(wu.shape), _resident(wd.shape)],
        out_specs=pl.BlockSpec((tm, d), row),
        out_shape=jax.ShapeDtypeStruct((m, d), F32),
        compiler_params=_cparams("parallel"),
        name="ffn",
    )(x, ln.reshape(1, d), *pair_args, wg, wu, wd)


def _inproj_odd_body(x_ref, ln_ref, w_ref, wba_ref, alog_ref, dtb_ref, qkv_ref, z_ref, gb_ref,
                     *, n_qkv, n_z, tn):
    hb = _rms(x_ref[...], ln_ref[...]).astype(BF16)
    for c in range(n_qkv // tn):
        qkv_ref[:, c * tn:(c + 1) * tn] = jnp.dot(hb, w_ref[:, c * tn:(c + 1) * tn],
                                                  preferred_element_type=F32)
    for c in range(n_z // tn):
        z_ref[:, c * tn:(c + 1) * tn] = jnp.dot(
            hb, w_ref[:, n_qkv + c * tn:n_qkv + (c + 1) * tn], preferred_element_type=F32).astype(BF16)
    ba = jnp.dot(hb, wba_ref[...], preferred_element_type=F32)
    lane = lax.broadcasted_iota(jnp.int32, ba.shape, 1)
    beta = jax.nn.sigmoid(ba)
    g = -jnp.exp(alog_ref[...]) * _softplus(ba + dtb_ref[...])
    gb_ref[...] = jnp.where(lane < H_C, beta, g)


def _inproj_odd(x, ln, w, a_log, dt_bias, *, tm):
    m, d = x.shape
    n_z = H_C * (d // H_C)
    n_qkv = w.shape[1] - n_z - 2 * H_C
    wba = jnp.zeros((d, LANES), BF16).at[:, :2 * H_C].set(w[:, n_qkv + n_z:])
    alog = jnp.zeros((1, LANES), F32).at[0, H_C:2 * H_C].set(a_log)
    dtb = jnp.zeros((1, LANES), F32).at[0, H_C:2 * H_C].set(dt_bias)
    row = lambda i: (i, 0)
    return pl.pallas_call(
        functools.partial(_inproj_odd_body, n_qkv=n_qkv, n_z=n_z, tn=512),
        grid=(m // tm,),
        in_specs=[pl.BlockSpec((tm, d), row), _full((1, d)), _full(w.shape), _full(wba.shape),
                  _full((1, LANES)), _full((1, LANES))],
        out_specs=[pl.BlockSpec((tm, n_qkv), row), pl.BlockSpec((tm, n_z), row),
                   pl.BlockSpec((tm, LANES), row)],
        out_shape=[jax.ShapeDtypeStruct((m, n_qkv), F32), jax.ShapeDtypeStruct((m, n_z), BF16),
                   jax.ShapeDtypeStruct((m, LANES), F32)],
        compiler_params=_cparams("parallel"),
        name="inproj_odd",
    )(x, ln.reshape(1, d), w, wba, alog, dtb)


def _l2n(x):
    return x * lax.rsqrt(jnp.sum(x * x, axis=-1, keepdims=True) + EPS)


def _delta_prompt_body(qkv_ref, z_ref, gb_ref, cw_ref, og_ref, o_ref, s_ref, prev_s, xc_s, *, tc, dk):
    t = pl.program_id(1)
    c = DELTA_CHUNK
    nk = H_C * dk

    @pl.when(t == 0)
    def _():
        prev_s[...] = jnp.zeros_like(prev_s)
        s_ref[...] = jnp.zeros_like(s_ref)

    x = qkv_ref[0]
    x1, x2, x3 = _shifted_rows(prev_s[...], x)
    xc_s[...] = _silu(x3 * cw_ref[0:1, :] + x2 * cw_ref[1:2, :] + x1 * cw_ref[2:3, :] + x * cw_ref[3:4, :])
    prev_s[...] = x[tc - SUBLANES:, :]

    ri = lax.broadcasted_iota(jnp.int32, (c, c), 0)
    ci = lax.broadcasted_iota(jnp.int32, (c, c), 1)
    incl = ri >= ci
    strict = ri > ci
    tri = jnp.where(incl, 1.0, 0.0)
    eye = jnp.where(ri == ci, 1.0, 0.0)
    heads = range(H_C)
    n_chunks = tc // c

    q, k, v, beta, e_g, decay, g_end, k_dec = ([] for _ in range(8))
    for ic in range(n_chunks):
        rows = slice(ic * c, (ic + 1) * c)
        gbc = gb_ref[0, rows, :]
        g_all = jnp.dot(tri, gbc, preferred_element_type=F32, precision=HIGHEST)
        g_t = g_all.T
        for h in heads:
            g_col = g_all[:, H_C + h:H_C + h + 1]
            g_row = g_t[H_C + h:H_C + h + 1, :]
            g_last = g_all[c - 1:c, H_C + h:H_C + h + 1]
            kn = _l2n(xc_s[rows, nk + h * dk:nk + (h + 1) * dk])
            q.append(_l2n(xc_s[rows, h * dk:(h + 1) * dk]) * (dk ** -0.5))
            k.append(kn)
            v.append(xc_s[rows, 2 * nk + h * dk:2 * nk + (h + 1) * dk])
            beta.append(gbc[:, h:h + 1])
            e_g.append(jnp.exp(g_col))
            decay.append(jnp.where(incl, jnp.exp(jnp.where(incl, g_col - g_row, 0.0)), 0.0))
            g_end.append(jnp.exp(g_last))
            k_dec.append(kn * jnp.exp(g_last - g_col))
    pairs = range(n_chunks * H_C)
    kb = [k[i] * beta[i] for i in pairs]
    kk = [_bdot_nt(jnp.concatenate([kb[i], q[i]], axis=0), k[i]) for i in pairs]
    qk = [kk[i][c:] * decay[i] for i in pairs]
    pw = [-jnp.where(strict, kk[i][:c] * decay[i], 0.0) for i in pairs]
    inv = [eye + pw[i] for i in pairs]
    for _ in range(int(math.log2(c)) - 1):
        pw = [_bdot(pw[i], pw[i]) for i in pairs]
        inv = [inv[i] + _bdot(inv[i], pw[i]) for i in pairs]
    sol = [_bdot(inv[i], jnp.concatenate([v[i] * beta[i], kb[i] * e_g[i]], axis=1)) for i in pairs]
    wq = [jnp.concatenate([sol[i][:, dk:], q[i] * e_g[i]], axis=0).astype(BF16) for i in pairs]

    s = [s_ref[0, h] for h in heads]
    for ic in range(n_chunks):
        rows = slice(ic * c, (ic + 1) * c)
        idx = [ic * H_C + h for h in heads]
        ws = [_bdot(wq[i], s[h]) for h, i in zip(heads, idx)]
        v_new = [sol[i][:, :dk] - ws[h][:c] for h, i in zip(heads, idx)]
        o = [ws[h][c:] + _bdot(qk[i], v_new[h]) for h, i in zip(heads, idx)]
        s = [s[h] * g_end[i] + _bdot_tn(k_dec[i], v_new[h]) for h, i in zip(heads, idx)]
        zt = z_ref[0, rows, :].astype(F32)
        o_ref[0, rows, :] = jnp.concatenate(
            [_rms(o[h], og_ref[...]) * _silu(zt[:, h * dk:(h + 1) * dk]) for h in heads], axis=1).astype(BF16)
    for h in heads:
        s_ref[0, h] = s[h]


def _delta_prompt(qkv, z, gb, cw, og, *, tc):
    b, t, nq = qkv.shape
    nz = z.shape[2]
    dk = nz // H_C
    seq = lambda i, j: (i, j, 0)
    return pl.pallas_call(
        functools.partial(_delta_prompt_body, tc=tc, dk=dk),
        grid=(b, t // tc),
        in_specs=[pl.BlockSpec((1, tc, nq), seq), pl.BlockSpec((1, tc, nz), seq),
                  pl.BlockSpec((1, tc, LANES), seq), _full((CONV_W, nq)), _full((1, dk))],
        out_specs=[pl.BlockSpec((1, tc, nz), seq),
                   pl.BlockSpec((1, H_C, dk, dk), lambda i, j: (i, 0, 0, 0))],
        out_shape=[jax.ShapeDtypeStruct((b, t, nz), BF16), jax.ShapeDtypeStruct((b, H_C, dk, dk), F32)],
        scratch_shapes=[pltpu.VMEM((SUBLANES, nq), F32), pltpu.VMEM((tc, nq), F32)],
        compiler_params=_cparams("parallel", "arbitrary"),
        name="delta_prompt",
    )(qkv, z, gb, cw, og.reshape(1, dk))


def _delta_prep_step_body(x_ref, b0_ref, b1_ref, b2_ref, cw_ref, q_ref, k_ref, v_ref, *, dk):
    xc = _silu(b0_ref[...] * cw_ref[0:1, :] + b1_ref[...] * cw_ref[1:2, :] + b2_ref[...] * cw_ref[2:3, :]
               + x_ref[...] * cw_ref[3:4, :])
    nk = H_C * dk
    for h in range(H_C):
        sl = slice(h * dk, (h + 1) * dk)
        q_ref[:, sl] = _l2n(xc[:, h * dk:(h + 1) * dk]) * (dk ** -0.5)
        k_ref[:, sl] = _l2n(xc[:, nk + h * dk:nk + (h + 1) * dk])
    v_ref[...] = xc[:, 2 * nk:]


def _delta_step_body(q_ref, k_ref, v_ref, gb_ref, z_ref, og_ref, s_ref, o_ref, sn_ref, *, dk, ns):
    row = lax.broadcasted_iota(jnp.int32, (SUBLANES, dk), 0)
    units = [(j, h) for j in range(ns) for h in range(H_C)]
    rows8 = lambda x: jnp.broadcast_to(x, (SUBLANES, dk))
    q = [q_ref[j, :, h * dk:(h + 1) * dk] for j, h in units]
    k = [k_ref[j, :, h * dk:(h + 1) * dk] for j, h in units]
    v = [v_ref[j, :, h * dk:(h + 1) * dk] for j, h in units]
    beta = [gb_ref[j, :, h:h + 1] for j, h in units]
    e_g = [jnp.exp(gb_ref[j, :, H_C + h:H_C + h + 1]) for j, h in units]
    s = [s_ref[j, h] for j, h in units]
    n = range(len(units))
    kq = [jnp.where(row == 0, rows8(k[u]), jnp.where(row == 1, rows8(q[u] * e_g[u]), 0.0)) for u in n]
    ks = [_bdot(kq[u], s[u]) for u in n]
    v_new = [beta[u] * (v[u] - e_g[u] * ks[u][0:1]) for u in n]
    o = [ks[u][1:2] + jnp.sum(q[u] * k[u], axis=-1, keepdims=True) * v_new[u] for u in n]
    k_hi, k_lo = zip(*[_split_bf16(jnp.where(row == 0, rows8(k[u]), 0.0)) for u in n])
    v_hi, v_lo = zip(*[_split_bf16(jnp.where(row == 0, rows8(v_new[u]), 0.0)) for u in n])
    outer = [_bdot_tn(k_hi[u], v_hi[u]) + _bdot_tn(k_hi[u], v_lo[u]) + _bdot_tn(k_lo[u], v_hi[u]) for u in n]
    for u, (j, h) in enumerate(units):
        sn_ref[j, h] = s[u] * e_g[u] + outer[u]
    for j in range(ns):
        zt = z_ref[j].astype(F32)
        o_ref[j] = jnp.concatenate(
            [_rms(o[j * H_C + h], og_ref[...]) * _silu(zt[:, h * dk:(h + 1) * dk]) for h in range(H_C)],
            axis=1).astype(BF16)


def _delta_step(qkv, buf, z, gb, s0, cw, og):
    n, nq = qkv.shape
    nz = z.shape[1]
    dk = nz // H_C
    args = (qkv, buf[:, 0], buf[:, 1], buf[:, 2], cw)
    q, k, v = pl.pallas_call(
        functools.partial(_delta_prep_step_body, dk=dk),
        grid=(1,),
        in_specs=[_full(a.shape) for a in args],
        out_specs=[_full((n, nz))] * 3,
        out_shape=[jax.ShapeDtypeStruct((n, nz), F32)] * 3,
        compiler_params=_cparams("arbitrary"),
        name="delta_prep_step",
    )(*args)
    ns = DELTA_STEP_SEQS if n % DELTA_STEP_SEQS == 0 else 1
    tok = lambda width: pl.BlockSpec((ns, 1, width), lambda i: (i, 0, 0))
    st = pl.BlockSpec((ns, H_C, dk, dk), lambda i: (i, 0, 0, 0))
    o, s_new = pl.pallas_call(
        functools.partial(_delta_step_body, dk=dk, ns=ns),
        grid=(n // ns,),
        in_specs=[tok(nz), tok(nz), tok(nz), tok(LANES), tok(nz), _full((1, dk)), st],
        out_specs=[tok(nz), st],
        out_shape=[jax.ShapeDtypeStruct((n, 1, nz), BF16), jax.ShapeDtypeStruct(s0.shape, F32)],
        compiler_params=_cparams("parallel"),
        name="delta_step",
    )(q.reshape(n, 1, nz), k.reshape(n, 1, nz), v.reshape(n, 1, nz), gb.reshape(n, 1, LANES),
      z.reshape(n, 1, nz), og.reshape(1, dk), s0)
    return o.reshape(n, nz), s_new


ROUTER_ROWS = 1024
SCATTER_ROWS = 512


def _router_body(x_ref, a_ref, wo_ref, ln_ref, rw_ref, xo_ref, hb_ref, comb_ref, rank_ref, rank_t_ref, cnt_ref,
                 *, n_exp, tm):
    tb = min(ROUTER_ROWS, tm)
    lane = lax.broadcasted_iota(jnp.int32, (tb, LANES), 1).astype(F32)
    ri = lax.broadcasted_iota(jnp.int32, (tb, tb), 0)
    ci = lax.broadcasted_iota(jnp.int32, (tb, tb), 1)
    before = jnp.where(ri > ci, 1.0, 0.0).astype(BF16)
    running = jnp.zeros((1, LANES), F32)
    for t0 in range(0, tm, tb):
        rows = slice(t0, t0 + tb)
        x = x_ref[rows, :] + jnp.dot(a_ref[rows, :], wo_ref[...], preferred_element_type=F32)
        xo_ref[rows, :] = x
        h = _rms(x, ln_ref[...])
        hb_ref[rows, :] = h.astype(BF16)
        h_hi, h_lo = _split_bf16(h)
        w_hi, w_lo = _split_bf16(rw_ref[...])
        logits = _bdot(h_hi, w_hi) + _bdot(h_lo, w_hi) + _bdot(h_hi, w_lo)
        logits = jnp.where(lane < n_exp, logits, -jnp.inf)
        m1 = jnp.max(logits, axis=-1, keepdims=True)
        i1 = jnp.min(jnp.where(logits == m1, lane, float(LANES)), axis=-1, keepdims=True)
        rest = jnp.where(lane == i1, -jnp.inf, logits)
        m2 = jnp.max(rest, axis=-1, keepdims=True)
        i2 = jnp.min(jnp.where(rest == m2, lane, float(LANES)), axis=-1, keepdims=True)
        e = jnp.exp(m2 - m1)
        g1 = 1.0 / (1.0 + e)
        g2 = e / (1.0 + e)
        comb = jnp.where(lane == i1, g1, 0.0) + jnp.where(lane == i2, g2, 0.0)
        comb_ref[rows, :] = comb
        sel = comb != 0.0
        self_f = jnp.where(sel, 1.0, 0.0)
        local = jnp.dot(before, self_f.astype(BF16), preferred_element_type=F32)
        rank = jnp.where(sel, local + running, -1.0)
        rank_ref[rows, :] = rank
        rank_t_ref[:, rows] = rank.T[:SUBLANES, :]
        running = running + jnp.sum(self_f, axis=0, keepdims=True)
    cnt_ref[0] = running.astype(jnp.int32)


def _router(x, a, wo, ln, rw, *, tm):
    m, d = x.shape
    n_exp = rw.shape[1]
    assert n_exp <= SUBLANES
    rwp = jnp.zeros((d, LANES), F32).at[:, :n_exp].set(rw)
    row = lambda i: (i, 0)
    return pl.pallas_call(
        functools.partial(_router_body, n_exp=n_exp, tm=tm),
        grid=(m // tm,),
        in_specs=[pl.BlockSpec((tm, d), row), pl.BlockSpec((tm, a.shape[1]), row), _full(wo.shape),
                  _full((1, d)), _full((d, LANES))],
        out_specs=[pl.BlockSpec((tm, d), row), pl.BlockSpec((tm, d), row), pl.BlockSpec((tm, LANES), row),
                   pl.BlockSpec((tm, LANES), row), pl.BlockSpec((SUBLANES, tm), lambda i: (0, i)),
                   pl.BlockSpec((1, 1, LANES), lambda i: (i, 0, 0))],
        out_shape=[jax.ShapeDtypeStruct((m, d), F32), jax.ShapeDtypeStruct((m, d), BF16),
                   jax.ShapeDtypeStruct((m, LANES), F32), jax.ShapeDtypeStruct((m, LANES), F32),
                   jax.ShapeDtypeStruct((SUBLANES, m), F32),
                   jax.ShapeDtypeStruct((m // tm, 1, LANES), jnp.int32)],
        compiler_params=_cparams("parallel"),
        name="router",
    )(x, a, wo, ln.reshape(1, d), rwp)


def _moe_body(cnt_ref, x_hbm, hb_ref, comb_ref, rank_ref, rank_t_ref, wg_ref, wu_ref, wd_ref, o_ref,
              xs_s, ys_s, x_sem, *, tm):
    i, e, f = pl.program_id(0), pl.program_id(1), pl.program_id(2)
    n_f = pl.num_programs(2)
    count = cnt_ref[i, e]
    r = MOE_ROWS

    @pl.when(jnp.logical_and(e == 0, f == 0))
    def _():
        rows = pl.ds(pl.multiple_of(i * tm, tm), tm)
        copy = pltpu.make_async_copy(x_hbm.at[rows, :], o_ref, x_sem)
        copy.start()
        copy.wait()

    def gather(lo, n):
        want = (lax.broadcasted_iota(jnp.int32, (n, tm), 0) + lo).astype(F32)
        onehot = jnp.where(rank_t_ref[pl.ds(e, 1), :] == want, 1.0, 0.0).astype(BF16)
        xs_s[lo:lo + n, :] = jnp.dot(onehot, hb_ref[...], preferred_element_type=F32).astype(BF16)

    def ffn(lo, n):
        xs = xs_s[lo:lo + n, :]
        g = jnp.dot(xs, wg_ref[0], preferred_element_type=F32)
        u = jnp.dot(xs, wu_ref[0], preferred_element_type=F32)
        y = jnp.dot((_silu(g) * u).astype(BF16), wd_ref[0], preferred_element_type=F32)

        @pl.when(f == 0)
        def _():
            ys_s[lo:lo + n, :] = y

        @pl.when(f > 0)
        def _():
            ys_s[lo:lo + n, :] += y

    def scatter(lo, n):
        tb = min(SCATTER_ROWS, tm)
        pick = lax.broadcasted_iota(jnp.int32, (tb, LANES), 1) == e
        want = (lax.broadcasted_iota(jnp.int32, (tb, n), 1) + lo).astype(F32)
        yb = ys_s[lo:lo + n, :].astype(BF16)
        for t0 in range(0, tm, tb):
            rank_col = jnp.sum(jnp.where(pick, rank_ref[t0:t0 + tb, :], 0.0), axis=-1, keepdims=True)
            gate_col = jnp.sum(jnp.where(pick, comb_ref[t0:t0 + tb, :], 0.0), axis=-1, keepdims=True)
            onehot = jnp.where(rank_col == want, 1.0, 0.0).astype(BF16)
            o_ref[t0:t0 + tb, :] += gate_col * jnp.dot(onehot, yb, preferred_element_type=F32)

    head = MOE_HEAD_ROWS if tm >= MOE_HEAD_ROWS else 0
    head_count = jnp.minimum(count, head)
    for n in range(MOE_ROW_STEP, head + 1, MOE_ROW_STEP):
        @pl.when(jnp.logical_and(head_count > n - MOE_ROW_STEP, head_count <= n))
        def _(n=n):
            @pl.when(f == 0)
            def _():
                gather(0, n)
            ffn(0, n)

            @pl.when(f == n_f - 1)
            def _():
                scatter(0, n)

    grp = min(2 * r, tm)
    for lo in range(head, tm, grp):
        if grp == 2 * r:
            @pl.when(count > lo + r)
            def _(lo=lo):
                @pl.when(f == 0)
                def _():
                    gather(lo, 2 * r)
                ffn(lo, 2 * r)

            @pl.when(jnp.logical_and(count > lo, count <= lo + r))
            def _(lo=lo):
                @pl.when(f == 0)
                def _():
                    gather(lo, r)
                    ys_s[lo + r:lo + 2 * r, :] = jnp.zeros((r, ys_s.shape[1]), F32)
                ffn(lo, r)
        else:
            @pl.when(count > lo)
            def _(lo=lo):
                @pl.when(f == 0)
                def _():
                    gather(lo, r)
                ffn(lo, r)

        @pl.when(jnp.logical_and(f == n_f - 1, count > lo))
        def _(lo=lo):
            scatter(lo, grp)


def _moe(x, hb, comb, rank, rank_t, cnt, wg, wu, wd, *, tm, tf):
    m, d = x.shape
    n_exp, _, f = wg.shape
    assert tm % MOE_ROWS == 0 and (tm <= MOE_ROWS or tm % (2 * MOE_ROWS) == 0)
    tok = lambda width: pl.BlockSpec((tm, width), lambda i, e, j, cnt: (i, 0))
    return pl.pallas_call(
        functools.partial(_moe_body, tm=tm),
        grid_spec=pltpu.PrefetchScalarGridSpec(
            num_scalar_prefetch=1,
            grid=(m // tm, n_exp, f // tf),
            in_specs=[pl.BlockSpec(memory_space=pl.ANY), tok(d), tok(LANES), tok(LANES),
                      pl.BlockSpec((SUBLANES, tm), lambda i, e, j, cnt: (0, i)),
                      pl.BlockSpec((1, d, tf), lambda i, e, j, cnt: (e, 0, j)),
                      pl.BlockSpec((1, d, tf), lambda i, e, j, cnt: (e, 0, j)),
                      pl.BlockSpec((1, tf, d), lambda i, e, j, cnt: (e, j, 0))],
            out_specs=tok(d),
            scratch_shapes=[pltpu.VMEM((tm, d), BF16), pltpu.VMEM((tm, d), F32), pltpu.SemaphoreType.DMA(())],
        ),
        out_shape=jax.ShapeDtypeStruct((m, d), F32),
        compiler_params=_cparams("parallel", "arbitrary", "arbitrary", vmem_limit_bytes=MOE_VMEM_LIMIT_BYTES),
        name="moe",
    )(cnt, x, hb, comb, rank, rank_t, wg, wu, wd)


def _moe_layer(x, a, wo, ln, rw, wg, wu, wd, *, tm, tf):
    x, hb, comb, rank, rank_t, cnt = _router(x, a, wo, ln, rw, tm=tm)
    return _moe(x, hb, comb, rank, rank_t, cnt.reshape(cnt.shape[0], LANES), wg, wu, wd, tm=tm, tf=tf)


def _block_diag(w):
    n, c, d = w.shape
    eye = jnp.eye(n, dtype=w.dtype)
    return (eye[:, None, :, None] * w[:, :, None, :]).reshape(n * c, n * d)


def _tile_rows(m, pref):
    return pref if m % pref == 0 else m


def kernel(x_prompt, x_sample, cache_k, cache_v, state_lru_conv, state_lru_h, state_delta_conv, state_delta_s, page_table, ln_mix_e, w_in_e, conv_lru_w, conv_lru_b, lru_wx, lru_bx, lru_wa, lru_ba, lru_lambda, qk_gain_q, qk_gain_k, lam_q1, lam_k1, lam_q2, lam_k2, subln_g, w_out_e, ln_ffn_e, ffn_wg, ffn_wu, ffn_wd, ln_mix_o, w_in_o, conv_delta_w, a_log, dt_bias, onorm_g, w_out_o, ln_ffn_o, router_w, moe_wg, moe_wu, moe_wd):
    bp, tp, d = x_prompt.shape
    bs = x_sample.shape[0]
    n_even, n_odd = w_in_e.shape[0], w_in_o.shape[0]
    lru_w = conv_lru_w.shape[-1]
    att_w = cache_k.shape[-1] * cache_k.shape[-2]
    qkv_c = conv_delta_w.shape[-1]
    dk = onorm_g.shape[-1]
    assert x_sample.shape[1] == 1 and lru_w == att_w and w_in_e.shape[-1] == 5 * lru_w

    xp = x_prompt.reshape(bp * tp, d)
    xs = x_sample.reshape(bs, d)
    mp = bp * tp
    tmp = _tile_rows(mp, 512)
    outs = {k: [] for k in ("kp", "vp", "cp", "hp", "dcp", "dsp", "ks", "vs", "cs", "hs", "dcs", "dss")}
    bf = lambda a: a.astype(BF16)
    vec = lambda a: a.reshape(1, -1)

    for layer in range(n_even + n_odd):
        i = layer // 2
        if layer % 2 == 0:
            lam_init = 0.8 - 0.6 * math.exp(-0.3 * layer)
            w_in, w_out = bf(w_in_e[i]), bf(w_out_e[i])
            lru_args = (conv_lru_w[i], vec(conv_lru_b[i]), bf(_block_diag(lru_wx[i])), vec(lru_bx[i]),
                        bf(_block_diag(lru_wa[i])), vec(lru_ba[i]), vec(lru_lambda[i]))
            lam_args = (lam_q1[i], lam_k1[i], lam_q2[i], lam_k2[i], subln_g[i])
            wg, wu, wd = bf(ffn_wg[i]), bf(ffn_wu[i]), bf(ffn_wd[i])

            gg, xr, qn, k_rows, v_rows, kb, vb = _inproj_even(xp, ln_mix_e[i], w_in, qk_gain_q[i],
                                                              qk_gain_k[i], tm=tmp)
            seq = lambda a: a.reshape(bp, tp, -1)
            ya, h_last = _lru_prompt(seq(xr), seq(gg), *lru_args, tt=_tile_rows(tp, 512))
            o = _attn_prompt(seq(qn), seq(kb), seq(vb), *lam_args, qk_gain_q[i], qk_gain_k[i],
                             lam_init=lam_init, tq=_tile_rows(tp, 256))
            xp = _ffn(xp, [(ya.reshape(mp, lru_w), w_out[:lru_w]), (o.reshape(mp, att_w), w_out[lru_w:])],
                      ln_ffn_e[i], wg, wu, wd, tm=tmp)
            outs["kp"].append(k_rows.reshape(bp, tp, H_B, att_w // H_B))
            outs["vp"].append(v_rows.reshape(bp, tp, H_B, att_w // H_B))
            outs["cp"].append(seq(xr)[:, tp - (CONV_W - 1):])
            outs["hp"].append(h_last.reshape(bp, lru_w))

            gg, xr, qn, k_rows, v_rows, _, _ = _inproj_even(xs, ln_mix_e[i], w_in, qk_gain_q[i],
                                                            qk_gain_k[i], tm=bs)
            ya, h_new = _lru_step(xr, gg, state_lru_conv[i], state_lru_h[i], *lru_args)
            o = _attn_decode(qn, k_rows.reshape(bs, att_w), v_rows.reshape(bs, att_w), cache_k[i], cache_v[i],
                             page_table, *lam_args,
                             lam_init=lam_init)
            xs = _ffn(xs, [(ya, w_out[:lru_w]), (o, w_out[lru_w:])], ln_ffn_e[i], wg, wu, wd, tm=bs)
            outs["ks"].append(k_rows.reshape(bs, 1, H_B, att_w // H_B))
            outs["vs"].append(v_rows.reshape(bs, 1, H_B, att_w // H_B))
            outs["cs"].append(jnp.concatenate([state_lru_conv[i][:, 1:], xr[:, None]], axis=1))
            outs["hs"].append(h_new)
        else:
            w_in, w_out = bf(w_in_o[i]), bf(w_out_o[i])
            wg, wu, wd = bf(moe_wg[i]), bf(moe_wu[i]), bf(moe_wd[i])
            tf = MOE_FF_TILE

            qkv, z, gb = _inproj_odd(xp, ln_mix_o[i], w_in, a_log[i], dt_bias[i], tm=tmp)
            seq = lambda a: a.reshape(bp, tp, -1)
            o, s_fin = _delta_prompt(seq(qkv), seq(z), seq(gb), conv_delta_w[i], onorm_g[i],
                                     tc=_tile_rows(tp, 256))
            xp = _moe_layer(xp, o.reshape(mp, -1), w_out, ln_ffn_o[i], router_w[i], wg, wu, wd,
                            tm=_tile_rows(mp, MOE_TOKENS), tf=tf)
            outs["dcp"].append(seq(qkv)[:, tp - (CONV_W - 1):])
            outs["dsp"].append(s_fin)

            qkv, z, gb = _inproj_odd(xs, ln_mix_o[i], w_in, a_log[i], dt_bias[i], tm=bs)
            o, s_new = _delta_step(qkv, state_delta_conv[i], z, gb, state_delta_s[i], conv_delta_w[i],
                                   onorm_g[i])
            xs = _moe_layer(xs, o, w_out, ln_ffn_o[i], router_w[i], wg, wu, wd, tm=bs, tf=tf)
            outs["dcs"].append(jnp.concatenate([state_delta_conv[i][:, 1:], qkv[:, None]], axis=1))
            outs["dss"].append(s_new)

    st = lambda k: jnp.stack(outs[k])
    return (xp.reshape(bp, tp, d), xs.reshape(bs, 1, d),
            st("kp"), st("vp"), st("cp"), st("hp"), st("dcp"), st("dsp"),
            st("ks"), st("vs"), st("cs"), st("hs"), st("dcs"), st("dss"))
```

```python
import functools
import math

import jax
import jax.numpy as jnp
from jax import lax
from jax.experimental import pallas as pl
from jax.experimental.pallas import tpu as pltpu

F32 = jnp.float32
BF16 = jnp.bfloat16
HIGHEST = lax.Precision.HIGHEST

CONV_W = 4
RG_C = 8.0
H_B = 4
H_C = 8
TOP_K = 2
EPS = 1e-6
NEG_INF = -1e30

LANES = 128
SUBLANES = 8
VMEM_LIMIT_BYTES = 52 * 1024 * 1024
MOE_VMEM_LIMIT_BYTES = 60 * 1024 * 1024

LRU_SEQS_PER_STEP = 4
DELTA_CHUNK = 64
DELTA_STEP_SEQS = 4
ATTN_HEADS_PER_STEP = 4
ATTN_FIXED_SHIFT_MAX = 40.0
MOE_ROWS = 128
MOE_HEAD_SIZES = (128, 256, 320, 384)
MOE_TOKENS = 1024
MOE_FF_TILE = 1792
FFN_CHUNK = 512


def _cparams(*sem, vmem_limit_bytes=VMEM_LIMIT_BYTES):
    return pltpu.CompilerParams(dimension_semantics=sem, vmem_limit_bytes=vmem_limit_bytes)


def _full(shape):
    nd = len(shape)
    return pl.BlockSpec(shape, lambda *_: (0,) * nd)


def _rms(x, g):
    return x * lax.rsqrt(jnp.mean(x * x, axis=-1, keepdims=True) + EPS) * g


def _softplus(x):
    return jnp.maximum(x, 0.0) + jnp.log1p(jnp.exp(-jnp.abs(x)))


def _silu(x):
    return x * jax.nn.sigmoid(x)


def _bdot(a, b):
    return jnp.dot(a.astype(BF16), b.astype(BF16), preferred_element_type=F32)


def _bdot_nt(a, b):
    return lax.dot_general(a.astype(BF16), b.astype(BF16), (((1,), (1,)), ((), ())),
                           preferred_element_type=F32)


def _bdot_tn(a, b):
    return lax.dot_general(a.astype(BF16), b.astype(BF16), (((0,), (0,)), ((), ())),
                           preferred_element_type=F32)


def _split_bf16(x):
    hi = x.astype(BF16)
    return hi, (x - hi.astype(F32)).astype(BF16)


def _shifted_rows(prev8, x):
    full = jnp.concatenate([prev8, x], axis=0)
    return [pltpu.roll(full, s, axis=0)[SUBLANES:] for s in (1, 2, 3)]


def _inproj_even_body(x_ref, ln_ref, w_ref, gq_ref, gk_ref, seg_ref,
                      gg_ref, xr_ref, q_ref, k_ref, v_ref, kb_ref, vb_ref, *, width, dh):
    hb = _rms(x_ref[...], ln_ref[...]).astype(BF16)

    def proj(j):
        return jnp.dot(hb, w_ref[:, j * width:(j + 1) * width], preferred_element_type=F32)

    def qk_norm(t, g):
        ms = _bdot(t * t, seg_ref[...]) * (1.0 / dh)
        return t * lax.rsqrt(ms + EPS) * g

    gg_ref[...] = jax.nn.gelu(proj(0)).astype(BF16)
    xr_ref[...] = proj(1)
    q_ref[...] = qk_norm(proj(2), gq_ref[...]).astype(BF16)
    k = qk_norm(proj(3), gk_ref[...])
    kb_ref[...] = k.astype(BF16)
    v = proj(4)
    vb_ref[...] = v.astype(BF16)
    hw = width // H_B
    for h in range(H_B):
        k_ref[:, h, :] = k[:, h * hw:(h + 1) * hw]
        v_ref[:, h, :] = v[:, h * hw:(h + 1) * hw]


def _inproj_even(x, ln, w, gq, gk, *, tm):
    m, d = x.shape
    width = w.shape[1] // 5
    dh = gq.shape[-1]
    reps = width // dh
    seg = jnp.kron(jnp.eye(reps, dtype=F32), jnp.ones((dh, dh), F32)).astype(BF16)
    gq_t = jnp.tile(gq.reshape(1, dh), (1, reps))
    gk_t = jnp.tile(gk.reshape(1, dh), (1, reps))
    row = lambda i: (i, 0)
    blk = pl.BlockSpec((tm, width), row)
    f32o = jax.ShapeDtypeStruct((m, width), F32)
    b16o = jax.ShapeDtypeStruct((m, width), BF16)
    kvblk = pl.BlockSpec((tm, H_B, width // H_B), lambda i: (i, 0, 0))
    kvo = jax.ShapeDtypeStruct((m, H_B, width // H_B), F32)
    return pl.pallas_call(
        functools.partial(_inproj_even_body, width=width, dh=dh),
        grid=(m // tm,),
        in_specs=[pl.BlockSpec((tm, d), row), _full((1, d)), _full(w.shape),
                  _full((1, width)), _full((1, width)), _full((width, width))],
        out_specs=[blk, blk, blk, kvblk, kvblk, blk, blk],
        out_shape=[b16o, f32o, b16o, kvo, kvo, b16o, b16o],
        compiler_params=_cparams("parallel"),
        name="inproj_even",
    )(x, ln.reshape(1, d), w, gq_t, gk_t, seg)


def _lru_coeffs(xc, wx_ref, bx_ref, wa_ref, ba_ref, lam_ref):
    xb = xc.astype(BF16)
    gate_x = jax.nn.sigmoid(jnp.dot(xb, wx_ref[...], preferred_element_type=F32) + bx_ref[...])
    gate_a = jax.nn.sigmoid(jnp.dot(xb, wa_ref[...], preferred_element_type=F32) + ba_ref[...])
    log_a = -RG_C * gate_a * _softplus(-lam_ref[...])
    a = jnp.exp(log_a)
    th = jnp.tanh(log_a)
    one_minus_a2 = -2.0 * th / (1.0 - th)
    b = jnp.sqrt(one_minus_a2) * (gate_x * xc)
    return a, b


def _lru_prompt_body(xr_ref, gg_ref, cw_ref, cb_ref, wx_ref, bx_ref, wa_ref, ba_ref, lam_ref,
                     ya_ref, hlast_ref, prev_s, h_s, a_s, b_s, *, tt, nb):
    t = pl.program_id(1)

    @pl.when(t == 0)
    def _():
        prev_s[...] = jnp.zeros_like(prev_s)
        h_s[...] = jnp.zeros_like(h_s)

    for b in range(nb):
        xr = xr_ref[b]
        x1, x2, x3 = _shifted_rows(prev_s[b], xr)
        xc = (x3 * cw_ref[0:1, :] + x2 * cw_ref[1:2, :] + x1 * cw_ref[2:3, :] + xr * cw_ref[3:4, :]
              + cb_ref[...])
        prev_s[b] = xr[tt - SUBLANES:, :]
        a, bb = _lru_coeffs(xc, wx_ref, bx_ref, wa_ref, ba_ref, lam_ref)
        a_s[b] = a
        b_s[b] = bb

    def step(i, hs):
        out = []
        for b in range(nb):
            h = a_s[b, pl.ds(i, 1), :] * hs[b] + b_s[b, pl.ds(i, 1), :]
            b_s[b, pl.ds(i, 1), :] = h
            out.append(h)
        return tuple(out)

    hs = lax.fori_loop(0, tt, step, tuple(h_s[b] for b in range(nb)), unroll=8)
    for b in range(nb):
        h_s[b] = hs[b]
        ya_ref[b] = (gg_ref[b].astype(F32) * b_s[b]).astype(BF16)
        hlast_ref[b] = hs[b]


def _lru_prompt(xr, gg, cw, cb, wx, bx, wa, ba, lam, *, tt):
    b, t, w = xr.shape
    nb = LRU_SEQS_PER_STEP if b % LRU_SEQS_PER_STEP == 0 else 1
    seq = lambda i, j: (i, j, 0)
    vec = _full((1, w))
    return pl.pallas_call(
        functools.partial(_lru_prompt_body, tt=tt, nb=nb),
        grid=(b // nb, t // tt),
        in_specs=[pl.BlockSpec((nb, tt, w), seq), pl.BlockSpec((nb, tt, w), seq), _full((CONV_W, w)), vec,
                  _full((w, w)), vec, _full((w, w)), vec, vec],
        out_specs=[pl.BlockSpec((nb, tt, w), seq), pl.BlockSpec((nb, 1, w), lambda i, j: (i, 0, 0))],
        out_shape=[jax.ShapeDtypeStruct((b, t, w), BF16), jax.ShapeDtypeStruct((b, 1, w), F32)],
        scratch_shapes=[pltpu.VMEM((nb, SUBLANES, w), F32), pltpu.VMEM((nb, 1, w), F32),
                        pltpu.VMEM((nb, tt, w), F32), pltpu.VMEM((nb, tt, w), F32)],
        compiler_params=_cparams("parallel", "arbitrary"),
        name="lru_prompt",
    )(xr, gg, cw, cb, wx, bx, wa, ba, lam)


def _lru_step_body(xr_ref, gg_ref, b0_ref, b1_ref, b2_ref, h0_ref, cw_ref, cb_ref,
                   wx_ref, bx_ref, wa_ref, ba_ref, lam_ref, ya_ref, h_ref):
    xr = xr_ref[...]
    xc = (b0_ref[...] * cw_ref[0:1, :] + b1_ref[...] * cw_ref[1:2, :] + b2_ref[...] * cw_ref[2:3, :]
          + xr * cw_ref[3:4, :] + cb_ref[...])
    a, b = _lru_coeffs(xc, wx_ref, bx_ref, wa_ref, ba_ref, lam_ref)
    h = a * h0_ref[...] + b
    h_ref[...] = h
    ya_ref[...] = (gg_ref[...].astype(F32) * h).astype(BF16)


def _lru_step(xr, gg, buf, h0, cw, cb, wx, bx, wa, ba, lam):
    n, w = xr.shape
    args = (xr, gg, buf[:, 0], buf[:, 1], buf[:, 2], h0, cw, cb, wx, bx, wa, ba, lam)
    return pl.pallas_call(
        _lru_step_body,
        grid=(1,),
        in_specs=[_full(a.shape) for a in args],
        out_specs=[_full((n, w)), _full((n, w))],
        out_shape=[jax.ShapeDtypeStruct((n, w), BF16), jax.ShapeDtypeStruct((n, w), F32)],
        compiler_params=_cparams("arbitrary"),
        name="lru_step",
    )(*args)


def _diff_lambda(lq1_ref, lk1_ref, lq2_ref, lk2_ref, lam_init):
    s1 = jnp.sum(lq1_ref[...] * lk1_ref[...], axis=-1, keepdims=True)
    s2 = jnp.sum(lq2_ref[...] * lk2_ref[...], axis=-1, keepdims=True)
    return jnp.exp(s1) - jnp.exp(s2) + lam_init


def _attn_prompt_body(lq1_ref, lk1_ref, lq2_ref, lk2_ref, sg_ref, gq_ref, gk_ref, q_ref, k_ref, v_ref, o_ref,
                      *, tq, dh, nh, lam_init):
    i = pl.program_id(2)
    hw = 2 * dh
    heads = range(nh)
    lam = _diff_lambda(lq1_ref, lk1_ref, lq2_ref, lk2_ref, lam_init)
    lane = lax.broadcasted_iota(jnp.int32, (tq, hw), 1)
    qq = []
    for h in heads:
        q = q_ref[0, :, h * hw:(h + 1) * hw] * (dh ** -0.5)
        zero = jnp.zeros_like(q)
        qq.append(jnp.concatenate([jnp.where(lane < dh, q, zero), jnp.where(lane >= dh, q, zero)], axis=0))
    rowi = lax.broadcasted_iota(jnp.int32, (2 * tq, tq), 0)
    coli = lax.broadcasted_iota(jnp.int32, (2 * tq, tq), 1)
    causal = coli <= jnp.where(rowi >= tq, rowi - tq, rowi)

    def scores(j, masked):
        start = pl.multiple_of(j * tq, tq)
        kj = k_ref[0, pl.ds(start, tq), :]
        vj = v_ref[0, pl.ds(start, tq), :]
        s = [_bdot_nt(qq[h], kj[:, h * hw:(h + 1) * hw]) for h in heads]
        if masked:
            s = [jnp.where(causal, s[h], NEG_INF) for h in heads]
        return s, [vj[:, h * hw:(h + 1) * hw] for h in heads]

    def finish(l, acc):
        out = []
        for h in heads:
            o = acc[h] / l[h]
            od = o[:tq] - lam * o[tq:]
            out.append(_rms(od, sg_ref[...]) * (1.0 - lam_init))
        o_ref[0] = jnp.concatenate(out, axis=1).astype(BF16)

    bound = (dh ** 0.5) * 1.05 * jnp.max(jnp.abs(gq_ref[...])) * jnp.max(jnp.abs(gk_ref[...]))
    fixed_ok = bound <= ATTN_FIXED_SHIFT_MAX

    @pl.when(fixed_ok)
    def _():
        def block(j, carry, masked):
            s, vj = scores(j, masked)
            p = [jnp.exp(s[h] - bound) for h in heads]
            l = [carry[h][0] + sum(p[h][:, c0:c0 + LANES] for c0 in range(0, tq, LANES)) for h in heads]
            acc = [carry[h][1] + _bdot(p[h], vj[h]) for h in heads]
            return tuple((l[h], acc[h]) for h in heads)

        init = tuple((jnp.zeros((2 * tq, LANES), F32), jnp.zeros((2 * tq, hw), F32)) for _ in heads)
        carry = lax.fori_loop(0, i, lambda j, c: block(j, c, False), init)
        carry = block(i, carry, True)
        finish([jnp.sum(carry[h][0], axis=-1, keepdims=True) for h in heads], [carry[h][1] for h in heads])

    @pl.when(jnp.logical_not(fixed_ok))
    def _():
        def block(j, carry, masked):
            s, vj = scores(j, masked)
            m_new = [jnp.maximum(carry[h][0], jnp.max(s[h], axis=-1, keepdims=True)) for h in heads]
            alpha = [jnp.exp(carry[h][0] - m_new[h]) for h in heads]
            p = [jnp.exp(s[h] - m_new[h]) for h in heads]
            l = [alpha[h] * carry[h][1] + jnp.sum(p[h], axis=-1, keepdims=True) for h in heads]
            acc = [alpha[h] * carry[h][2] + _bdot(p[h], vj[h]) for h in heads]
            return tuple((m_new[h], l[h], acc[h]) for h in heads)

        init = tuple((jnp.full((2 * tq, 1), NEG_INF, F32), jnp.zeros((2 * tq, 1), F32),
                      jnp.zeros((2 * tq, hw), F32)) for _ in heads)
        carry = lax.fori_loop(0, i, lambda j, c: block(j, c, False), init)
        carry = block(i, carry, True)
        finish([carry[h][1] for h in heads], [carry[h][2] for h in heads])


def _attn_prompt(q, k, v, lq1, lk1, lq2, lk2, sub_g, gq, gk, *, lam_init, tq):
    b, t, w = q.shape
    hw = w // H_B
    dh = hw // 2
    vec = _full((1, dh))
    nh = ATTN_HEADS_PER_STEP
    assert tq % LANES == 0
    qblk = pl.BlockSpec((1, tq, nh * hw), lambda bi, hg, i: (bi, i, hg))
    kvblk = pl.BlockSpec((1, t, nh * hw), lambda bi, hg, i: (bi, 0, hg))
    return pl.pallas_call(
        functools.partial(_attn_prompt_body, tq=tq, dh=dh, nh=nh, lam_init=lam_init),
        grid=(b, H_B // nh, t // tq),
        in_specs=[vec, vec, vec, vec, _full((1, hw)), vec, vec, qblk, kvblk, kvblk],
        out_specs=qblk,
        out_shape=jax.ShapeDtypeStruct((b, t, w), BF16),
        compiler_params=_cparams("parallel", "parallel", "arbitrary"),
        name="attn_prompt",
    )(lq1.reshape(1, dh), lk1.reshape(1, dh), lq2.reshape(1, dh), lk2.reshape(1, dh),
      sub_g.reshape(1, hw), gq.reshape(1, dh), gk.reshape(1, dh), q, k, v)


def _attn_decode_body(pt_ref, lq1_ref, lk1_ref, lq2_ref, lk2_ref, sg_ref, q_ref, kn_ref, vn_ref, *refs,
                      n_pages, dh, lam_init):
    del pt_ref
    k_refs = refs[:n_pages]
    v_refs = refs[n_pages:2 * n_pages]
    o_ref = refs[2 * n_pages]
    hw = 2 * dh
    rows = 2 * H_B
    lam = _diff_lambda(lq1_ref, lk1_ref, lq2_ref, lk2_ref, lam_init)

    row = lax.broadcasted_iota(jnp.int32, (rows, hw), 0)
    lane = lax.broadcasted_iota(jnp.int32, (rows, hw), 1)

    def by_head(vec_ref):
        out = jnp.zeros((rows, hw), F32)
        for h in range(H_B):
            piece = vec_ref[0, :, h * hw:(h + 1) * hw].astype(F32)
            out = jnp.where(row // 2 == h, jnp.broadcast_to(piece, (rows, hw)), out)
        return out

    half = (lane >= dh) == (row % 2 == 1)
    wq = jnp.where(half, by_head(q_ref), 0.0) * (dh ** -0.5)
    kn = by_head(kn_ref)
    vn = by_head(vn_ref)
    s_new = jnp.sum(wq * kn, axis=-1, keepdims=True)

    n_rows = k_refs[0].shape[1]
    col_head = lax.broadcasted_iota(jnp.int32, (rows, n_rows), 1) % H_B
    own = col_head == lax.broadcasted_iota(jnp.int32, (rows, n_rows), 0) // 2
    wq_b = wq.astype(BF16)
    scores = [jnp.where(own, _bdot_nt(wq_b, k_refs[p][0]), NEG_INF) for p in range(n_pages)]
    m = s_new
    for s in scores:
        m = jnp.maximum(m, jnp.max(s, axis=-1, keepdims=True))
    p_new = jnp.exp(s_new - m)
    l = p_new
    acc = p_new * vn
    for p in range(n_pages):
        pr = jnp.exp(scores[p] - m)
        l = l + jnp.sum(pr, axis=-1, keepdims=True)
        acc = acc + _bdot(pr, v_refs[p][0])
    o = acc / l
    r_out = lax.broadcasted_iota(jnp.int32, (SUBLANES, rows), 0)
    r_in = lax.broadcasted_iota(jnp.int32, (SUBLANES, rows), 1)
    sel = (jnp.where(r_in == 2 * r_out, 1.0, 0.0) - lam * jnp.where(r_in == 2 * r_out + 1, 1.0, 0.0))
    od = jnp.dot(sel, o, preferred_element_type=F32, precision=HIGHEST)[:H_B]
    o_ref[0] = _rms(od, sg_ref[...]) * (1.0 - lam_init)


def _attn_decode(q, k_new, v_new, cache_k, cache_v, page_table, lq1, lk1, lq2, lk2, sub_g, *, lam_init):
    n, w = q.shape
    hw = w // H_B
    dh = hw // 2
    n_pages = page_table.shape[1]
    n_pool, page = cache_k.shape[0], cache_k.shape[1]
    ck = cache_k.reshape(n_pool, page * H_B, hw)
    cv = cache_v.reshape(n_pool, page * H_B, hw)
    vec = pl.BlockSpec((1, dh), lambda i, pt: (0, 0))
    tok = pl.BlockSpec((1, 1, w), lambda i, pt: (i, 0, 0))

    def page_spec(p):
        return pl.BlockSpec((1, page * H_B, hw), lambda i, pt: (pt[i, p], 0, 0))

    pages = [page_spec(p) for p in range(n_pages)]
    out = pl.pallas_call(
        functools.partial(_attn_decode_body, n_pages=n_pages, dh=dh, lam_init=lam_init),
        grid_spec=pltpu.PrefetchScalarGridSpec(
            num_scalar_prefetch=1,
            grid=(n,),
            in_specs=[vec, vec, vec, vec, pl.BlockSpec((1, hw), lambda i, pt: (0, 0)), tok, tok, tok]
            + pages + pages,
            out_specs=pl.BlockSpec((1, H_B, hw), lambda i, pt: (i, 0, 0)),
        ),
        out_shape=jax.ShapeDtypeStruct((n, H_B, hw), F32),
        compiler_params=_cparams("arbitrary"),
        name="attn_decode",
    )(page_table, lq1.reshape(1, dh), lk1.reshape(1, dh), lq2.reshape(1, dh), lk2.reshape(1, dh),
      sub_g.reshape(1, hw), q.reshape(n, 1, w), k_new.reshape(n, 1, w), v_new.reshape(n, 1, w),
      *([ck] * n_pages), *([cv] * n_pages))
    return out.reshape(n, w)


def _resident(shape):
    nd = len(shape)
    return pl.BlockSpec(shape, lambda *_: (0,) * nd, pipeline_mode=pl.Buffered(1))


def _mixer_residual(x, pair_refs):
    y = None
    for a_ref, w_ref in zip(pair_refs[0::2], pair_refs[1::2]):
        part = jnp.dot(a_ref[...].astype(BF16), w_ref[...], preferred_element_type=F32)
        y = part if y is None else y + part
    return x if y is None else x + y


def _ffn_body(*refs, chunks, n_pairs):
    x_ref, ln_ref = refs[:2]
    wg_ref, wu_ref, wd_ref, o_ref = refs[2 + 2 * n_pairs:]
    x = _mixer_residual(x_ref[...], refs[2:2 + 2 * n_pairs])
    hb = _rms(x, ln_ref[...]).astype(BF16)
    acc = x
    for lo, n in chunks:
        g = jnp.dot(hb, wg_ref[:, lo:lo + n], preferred_element_type=F32)
        u = jnp.dot(hb, wu_ref[:, lo:lo + n], preferred_element_type=F32)
        acc = acc + jnp.dot((_silu(g) * u).astype(BF16), wd_ref[lo:lo + n, :], preferred_element_type=F32)
    o_ref[...] = acc


def _ffn(x, pairs, ln, wg, wu, wd, *, tm):
    m, d = x.shape
    f = wg.shape[1]
    chunks = [(lo, min(FFN_CHUNK, f - lo)) for lo in range(0, f, FFN_CHUNK)]
    row = lambda i: (i, 0)
    pair_specs, pair_args = [], []
    for a, w in pairs:
        pair_specs += [pl.BlockSpec((tm, a.shape[1]), row), _full(w.shape)]
        pair_args += [a, w]
    return pl.pallas_call(
        functools.partial(_ffn_body, chunks=chunks, n_pairs=len(pairs)),
        grid=(m // tm,),
        in_specs=[pl.BlockSpec((tm, d), row), _full((1, d))] + pair_specs
        + [_resident(wg.shape), _resident(wu.shape), _resident(wd.shape)],
        out_specs=pl.BlockSpec((tm, d), row),
        out_shape=jax.ShapeDtypeStruct((m, d), F32),
        compiler_params=_cparams("parallel"),
        name="ffn",
    )(x, ln.reshape(1, d), *pair_args, wg, wu, wd)


def _inproj_odd_body(x_ref, ln_ref, w_ref, wba_ref, alog_ref, dtb_ref, qkv_ref, z_ref, gb_ref,
                     *, n_qkv, n_z, tn):
    hb = _rms(x_ref[...], ln_ref[...]).astype(BF16)
    for c in range(n_qkv // tn):
        qkv_ref[:, c * tn:(c + 1) * tn] = jnp.dot(hb, w_ref[:, c * tn:(c + 1) * tn],
                                                  preferred_element_type=F32)
    for c in range(n_z // tn):
        z_ref[:, c * tn:(c + 1) * tn] = jnp.dot(
            hb, w_ref[:, n_qkv + c * tn:n_qkv + (c + 1) * tn], preferred_element_type=F32).astype(BF16)
    ba = jnp.dot(hb, wba_ref[...], preferred_element_type=F32)
    lane = lax.broadcasted_iota(jnp.int32, ba.shape, 1)
    beta = jax.nn.sigmoid(ba)
    g = -jnp.exp(alog_ref[...]) * _softplus(ba + dtb_ref[...])
    gb_ref[...] = jnp.where(lane < H_C, beta, g)


def _inproj_odd(x, ln, w, a_log, dt_bias, *, tm):
    m, d = x.shape
    n_z = H_C * (d // H_C)
    n_qkv = w.shape[1] - n_z - 2 * H_C
    wba = jnp.zeros((d, LANES), BF16).at[:, :2 * H_C].set(w[:, n_qkv + n_z:])
    alog = jnp.zeros((1, LANES), F32).at[0, H_C:2 * H_C].set(a_log)
    dtb = jnp.zeros((1, LANES), F32).at[0, H_C:2 * H_C].set(dt_bias)
    row = lambda i: (i, 0)
    return pl.pallas_call(
        functools.partial(_inproj_odd_body, n_qkv=n_qkv, n_z=n_z, tn=512),
        grid=(m // tm,),
        in_specs=[pl.BlockSpec((tm, d), row), _full((1, d)), _full(w.shape), _full(wba.shape),
                  _full((1, LANES)), _full((1, LANES))],
        out_specs=[pl.BlockSpec((tm, n_qkv), row), pl.BlockSpec((tm, n_z), row),
                   pl.BlockSpec((tm, LANES), row)],
        out_shape=[jax.ShapeDtypeStruct((m, n_qkv), F32), jax.ShapeDtypeStruct((m, n_z), BF16),
                   jax.ShapeDtypeStruct((m, LANES), F32)],
        compiler_params=_cparams("parallel"),
        name="inproj_odd",
    )(x, ln.reshape(1, d), w, wba, alog, dtb)


def _l2n(x):
    return x * lax.rsqrt(jnp.sum(x * x, axis=-1, keepdims=True) + EPS)


def _delta_prompt_body(qkv_ref, z_ref, gb_ref, cw_ref, og_ref, o_ref, s_ref, prev_s, xc_s, *, tc, dk):
    t = pl.program_id(1)
    c = DELTA_CHUNK
    nk = H_C * dk

    @pl.when(t == 0)
    def _():
        prev_s[...] = jnp.zeros_like(prev_s)
        s_ref[...] = jnp.zeros_like(s_ref)

    x = qkv_ref[0]
    x1, x2, x3 = _shifted_rows(prev_s[...], x)
    xc_s[...] = _silu(x3 * cw_ref[0:1, :] + x2 * cw_ref[1:2, :] + x1 * cw_ref[2:3, :] + x * cw_ref[3:4, :])
    prev_s[...] = x[tc - SUBLANES:, :]

    ri = lax.broadcasted_iota(jnp.int32, (c, c), 0)
    ci = lax.broadcasted_iota(jnp.int32, (c, c), 1)
    incl = ri >= ci
    strict = ri > ci
    tri = jnp.where(incl, 1.0, 0.0)
    eye = jnp.where(ri == ci, 1.0, 0.0)
    heads = range(H_C)
    n_chunks = tc // c

    q, k, v, beta, e_g, decay, g_end, k_dec = ([] for _ in range(8))
    for ic in range(n_chunks):
        rows = slice(ic * c, (ic + 1) * c)
        gbc = gb_ref[0, rows, :]
        g_all = jnp.dot(tri, gbc, preferred_element_type=F32, precision=HIGHEST)
        g_t = g_all.T
        for h in heads:
            g_col = g_all[:, H_C + h:H_C + h + 1]
            g_row = g_t[H_C + h:H_C + h + 1, :]
            g_last = g_all[c - 1:c, H_C + h:H_C + h + 1]
            kn = _l2n(xc_s[rows, nk + h * dk:nk + (h + 1) * dk])
            q.append(_l2n(xc_s[rows, h * dk:(h + 1) * dk]) * (dk ** -0.5))
            k.append(kn)
            v.append(xc_s[rows, 2 * nk + h * dk:2 * nk + (h + 1) * dk])
            beta.append(gbc[:, h:h + 1])
            e_g.append(jnp.exp(g_col))
            decay.append(jnp.where(incl, jnp.exp(jnp.where(incl, g_col - g_row, 0.0)), 0.0))
            g_end.append(jnp.exp(g_last))
            k_dec.append(kn * jnp.exp(g_last - g_col))
    pairs = range(n_chunks * H_C)
    kb = [k[i] * beta[i] for i in pairs]
    kk = [_bdot_nt(jnp.concatenate([kb[i], q[i]], axis=0), k[i]) for i in pairs]
    qk = [kk[i][c:] * decay[i] for i in pairs]
    pw = [-jnp.where(strict, kk[i][:c] * decay[i], 0.0) for i in pairs]
    inv = [eye + pw[i] for i in pairs]
    for _ in range(int(math.log2(c)) - 1):
        pw = [_bdot(pw[i], pw[i]) for i in pairs]
        inv = [inv[i] + _bdot(inv[i], pw[i]) for i in pairs]
    sol = [_bdot(inv[i], jnp.concatenate([v[i] * beta[i], kb[i] * e_g[i]], axis=1)) for i in pairs]
    wq = [jnp.concatenate([sol[i][:, dk:], q[i] * e_g[i]], axis=0).astype(BF16) for i in pairs]

    s = [s_ref[0, h] for h in heads]
    for ic in range(n_chunks):
        rows = slice(ic * c, (ic + 1) * c)
        idx = [ic * H_C + h for h in heads]
        ws = [_bdot(wq[i], s[h]) for h, i in zip(heads, idx)]
        v_new = [sol[i][:, :dk] - ws[h][:c] for h, i in zip(heads, idx)]
        o = [ws[h][c:] + _bdot(qk[i], v_new[h]) for h, i in zip(heads, idx)]
        s = [s[h] * g_end[i] + _bdot_tn(k_dec[i], v_new[h]) for h, i in zip(heads, idx)]
        zt = z_ref[0, rows, :].astype(F32)
        o_ref[0, rows, :] = jnp.concatenate(
            [_rms(o[h], og_ref[...]) * _silu(zt[:, h * dk:(h + 1) * dk]) for h in heads], axis=1).astype(BF16)
    for h in heads:
        s_ref[0, h] = s[h]


def _delta_prompt(qkv, z, gb, cw, og, *, tc):
    b, t, nq = qkv.shape
    nz = z.shape[2]
    dk = nz // H_C
    seq = lambda i, j: (i, j, 0)
    return pl.pallas_call(
        functools.partial(_delta_prompt_body, tc=tc, dk=dk),
        grid=(b, t // tc),
        in_specs=[pl.BlockSpec((1, tc, nq), seq), pl.BlockSpec((1, tc, nz), seq),
                  pl.BlockSpec((1, tc, LANES), seq), _full((CONV_W, nq)), _full((1, dk))],
        out_specs=[pl.BlockSpec((1, tc, nz), seq),
                   pl.BlockSpec((1, H_C, dk, dk), lambda i, j: (i, 0, 0, 0))],
        out_shape=[jax.ShapeDtypeStruct((b, t, nz), BF16), jax.ShapeDtypeStruct((b, H_C, dk, dk), F32)],
        scratch_shapes=[pltpu.VMEM((SUBLANES, nq), F32), pltpu.VMEM((tc, nq), F32)],
        compiler_params=_cparams("parallel", "arbitrary"),
        name="delta_prompt",
    )(qkv, z, gb, cw, og.reshape(1, dk))


def _delta_prep_step_body(x_ref, b0_ref, b1_ref, b2_ref, cw_ref, q_ref, k_ref, v_ref, *, dk):
    xc = _silu(b0_ref[...] * cw_ref[0:1, :] + b1_ref[...] * cw_ref[1:2, :] + b2_ref[...] * cw_ref[2:3, :]
               + x_ref[...] * cw_ref[3:4, :])
    nk = H_C * dk
    for h in range(H_C):
        sl = slice(h * dk, (h + 1) * dk)
        q_ref[:, sl] = _l2n(xc[:, h * dk:(h + 1) * dk]) * (dk ** -0.5)
        k_ref[:, sl] = _l2n(xc[:, nk + h * dk:nk + (h + 1) * dk])
    v_ref[...] = xc[:, 2 * nk:]


def _delta_step_body(q_ref, k_ref, v_ref, gb_ref, z_ref, og_ref, s_ref, o_ref, sn_ref, *, dk, ns):
    row = lax.broadcasted_iota(jnp.int32, (SUBLANES, dk), 0)
    units = [(j, h) for j in range(ns) for h in range(H_C)]
    rows8 = lambda x: jnp.broadcast_to(x, (SUBLANES, dk))
    q = [q_ref[j, :, h * dk:(h + 1) * dk] for j, h in units]
    k = [k_ref[j, :, h * dk:(h + 1) * dk] for j, h in units]
    v = [v_ref[j, :, h * dk:(h + 1) * dk] for j, h in units]
    beta = [gb_ref[j, :, h:h + 1] for j, h in units]
    e_g = [jnp.exp(gb_ref[j, :, H_C + h:H_C + h + 1]) for j, h in units]
    s = [s_ref[j, h] for j, h in units]
    n = range(len(units))
    kq = [jnp.where(row == 0, rows8(k[u]), jnp.where(row == 1, rows8(q[u] * e_g[u]), 0.0)) for u in n]
    ks = [_bdot(kq[u], s[u]) for u in n]
    v_new = [beta[u] * (v[u] - e_g[u] * ks[u][0:1]) for u in n]
    o = [ks[u][1:2] + jnp.sum(q[u] * k[u], axis=-1, keepdims=True) * v_new[u] for u in n]
    k_hi, k_lo = zip(*[_split_bf16(jnp.where(row == 0, rows8(k[u]), 0.0)) for u in n])
    v_hi, v_lo = zip(*[_split_bf16(jnp.where(row == 0, rows8(v_new[u]), 0.0)) for u in n])
    outer = [_bdot_tn(k_hi[u], v_hi[u]) + _bdot_tn(k_hi[u], v_lo[u]) + _bdot_tn(k_lo[u], v_hi[u]) for u in n]
    for u, (j, h) in enumerate(units):
        sn_ref[j, h] = s[u] * e_g[u] + outer[u]
    for j in range(ns):
        zt = z_ref[j].astype(F32)
        o_ref[j] = jnp.concatenate(
            [_rms(o[j * H_C + h], og_ref[...]) * _silu(zt[:, h * dk:(h + 1) * dk]) for h in range(H_C)],
            axis=1).astype(BF16)


def _delta_step(qkv, buf, z, gb, s0, cw, og):
    n, nq = qkv.shape
    nz = z.shape[1]
    dk = nz // H_C
    args = (qkv, buf[:, 0], buf[:, 1], buf[:, 2], cw)
    q, k, v = pl.pallas_call(
        functools.partial(_delta_prep_step_body, dk=dk),
        grid=(1,),
        in_specs=[_full(a.shape) for a in args],
        out_specs=[_full((n, nz))] * 3,
        out_shape=[jax.ShapeDtypeStruct((n, nz), F32)] * 3,
        compiler_params=_cparams("arbitrary"),
        name="delta_prep_step",
    )(*args)
    ns = DELTA_STEP_SEQS if n % DELTA_STEP_SEQS == 0 else 1
    tok = lambda width: pl.BlockSpec((ns, 1, width), lambda i: (i, 0, 0))
    st = pl.BlockSpec((ns, H_C, dk, dk), lambda i: (i, 0, 0, 0))
    o, s_new = pl.pallas_call(
        functools.partial(_delta_step_body, dk=dk, ns=ns),
        grid=(n // ns,),
        in_specs=[tok(nz), tok(nz), tok(nz), tok(LANES), tok(nz), _full((1, dk)), st],
        out_specs=[tok(nz), st],
        out_shape=[jax.ShapeDtypeStruct((n, 1, nz), BF16), jax.ShapeDtypeStruct(s0.shape, F32)],
        compiler_params=_cparams("parallel"),
        name="delta_step",
    )(q.reshape(n, 1, nz), k.reshape(n, 1, nz), v.reshape(n, 1, nz), gb.reshape(n, 1, LANES),
      z.reshape(n, 1, nz), og.reshape(1, dk), s0)
    return o.reshape(n, nz), s_new


ROUTER_ROWS = 1024
SCATTER_ROWS = 512


def _router_body(x_ref, a_ref, wo_ref, ln_ref, rw_ref, xo_ref, hb_ref, comb_ref, rank_ref, rank_t_ref, cnt_ref,
                 *, n_exp, tm):
    tb = min(ROUTER_ROWS, tm)
    lane = lax.broadcasted_iota(jnp.int32, (tb, LANES), 1).astype(F32)
    ri = lax.broadcasted_iota(jnp.int32, (tb, tb), 0)
    ci = lax.broadcasted_iota(jnp.int32, (tb, tb), 1)
    before = jnp.where(ri > ci, 1.0, 0.0).astype(BF16)
    running = jnp.zeros((1, LANES), F32)
    for t0 in range(0, tm, tb):
        rows = slice(t0, t0 + tb)
        x = x_ref[rows, :] + jnp.dot(a_ref[rows, :], wo_ref[...], preferred_element_type=F32)
        xo_ref[rows, :] = x
        h = _rms(x, ln_ref[...])
        hb_ref[rows, :] = h.astype(BF16)
        h_hi, h_lo = _split_bf16(h)
        w_hi, w_lo = _split_bf16(rw_ref[...])
        logits = _bdot(h_hi, w_hi) + _bdot(h_lo, w_hi) + _bdot(h_hi, w_lo)
        logits = jnp.where(lane < n_exp, logits, -jnp.inf)
        m1 = jnp.max(logits, axis=-1, keepdims=True)
        i1 = jnp.min(jnp.where(logits == m1, lane, float(LANES)), axis=-1, keepdims=True)
        rest = jnp.where(lane == i1, -jnp.inf, logits)
        m2 = jnp.max(rest, axis=-1, keepdims=True)
        i2 = jnp.min(jnp.where(rest == m2, lane, float(LANES)), axis=-1, keepdims=True)
        e = jnp.exp(m2 - m1)
        g1 = 1.0 / (1.0 + e)
        g2 = e / (1.0 + e)
        comb = jnp.where(lane == i1, g1, 0.0) + jnp.where(lane == i2, g2, 0.0)
        comb_ref[rows, :] = comb
        sel = comb != 0.0
        self_f = jnp.where(sel, 1.0, 0.0)
        local = jnp.dot(before, self_f.astype(BF16), preferred_element_type=F32)
        rank = jnp.where(sel, local + running, -1.0)
        rank_ref[rows, :] = rank
        rank_t_ref[:, rows] = rank.T[:SUBLANES, :]
        running = running + jnp.sum(self_f, axis=0, keepdims=True)
    cnt_ref[0] = running.astype(jnp.int32)


def _router(x, a, wo, ln, rw, *, tm):
    m, d = x.shape
    n_exp = rw.shape[1]
    assert n_exp <= SUBLANES
    rwp = jnp.zeros((d, LANES), F32).at[:, :n_exp].set(rw)
    row = lambda i: (i, 0)
    return pl.pallas_call(
        functools.partial(_router_body, n_exp=n_exp, tm=tm),
        grid=(m // tm,),
        in_specs=[pl.BlockSpec((tm, d), row), pl.BlockSpec((tm, a.shape[1]), row), _full(wo.shape),
                  _full((1, d)), _full((d, LANES))],
        out_specs=[pl.BlockSpec((tm, d), row), pl.BlockSpec((tm, d), row), pl.BlockSpec((tm, LANES), row),
                   pl.BlockSpec((tm, LANES), row), pl.BlockSpec((SUBLANES, tm), lambda i: (0, i)),
                   pl.BlockSpec((1, 1, LANES), lambda i: (i, 0, 0))],
        out_shape=[jax.ShapeDtypeStruct((m, d), F32), jax.ShapeDtypeStruct((m, d), BF16),
                   jax.ShapeDtypeStruct((m, LANES), F32), jax.ShapeDtypeStruct((m, LANES), F32),
                   jax.ShapeDtypeStruct((SUBLANES, m), F32),
                   jax.ShapeDtypeStruct((m // tm, 1, LANES), jnp.int32)],
        compiler_params=_cparams("parallel"),
        name="router",
    )(x, a, wo, ln.reshape(1, d), rwp)


def _moe_body(cnt_ref, x_hbm, hb_ref, comb_ref, rank_ref, rank_t_ref, wg_ref, wu_ref, wd_ref, o_ref,
              xs_s, ys_s, x_sem, *, tm):
    i, e, f = pl.program_id(0), pl.program_id(1), pl.program_id(2)
    n_f = pl.num_programs(2)
    count = cnt_ref[i, e]
    r = MOE_ROWS

    @pl.when(jnp.logical_and(e == 0, f == 0))
    def _():
        rows = pl.ds(pl.multiple_of(i * tm, tm), tm)
        copy = pltpu.make_async_copy(x_hbm.at[rows, :], o_ref, x_sem)
        copy.start()
        copy.wait()

    def group(lo, n):
        rows = pl.ds(lo, n)

        @pl.when(f == 0)
        def _():
            want = (lax.broadcasted_iota(jnp.int32, (n, tm), 0) + lo).astype(F32)
            onehot = jnp.where(rank_t_ref[pl.ds(e, 1), :] == want, 1.0, 0.0).astype(BF16)
            xs_s[rows, :] = jnp.dot(onehot, hb_ref[...], preferred_element_type=F32).astype(BF16)

        xs = xs_s[rows, :]
        g = jnp.dot(xs, wg_ref[0], preferred_element_type=F32)
        u = jnp.dot(xs, wu_ref[0], preferred_element_type=F32)
        y = jnp.dot((_silu(g) * u).astype(BF16), wd_ref[0], preferred_element_type=F32)

        @pl.when(f == 0)
        def _():
            ys_s[rows, :] = y

        @pl.when(f > 0)
        def _():
            ys_s[rows, :] += y

        @pl.when(f == n_f - 1)
        def _():
            tb = min(SCATTER_ROWS, tm)
            pick = lax.broadcasted_iota(jnp.int32, (tb, LANES), 1) == e
            want = (lax.broadcasted_iota(jnp.int32, (tb, n), 1) + lo).astype(F32)
            yb = ys_s[rows, :].astype(BF16)
            for t0 in range(0, tm, tb):
                rank_col = jnp.sum(jnp.where(pick, rank_ref[t0:t0 + tb, :], 0.0), axis=-1, keepdims=True)
                gate_col = jnp.sum(jnp.where(pick, comb_ref[t0:t0 + tb, :], 0.0), axis=-1, keepdims=True)
                onehot = jnp.where(rank_col == want, 1.0, 0.0).astype(BF16)
                o_ref[t0:t0 + tb, :] += gate_col * jnp.dot(onehot, yb, preferred_element_type=F32)

    sizes = [n for n in MOE_HEAD_SIZES if n <= tm]
    head = sizes[-1] if sizes else 0
    head_count = jnp.minimum(count, head)
    for below, n in zip([0] + sizes[:-1], sizes):
        @pl.when(jnp.logical_and(head_count > below, head_count <= n))
        def _(n=n):
            group(0, n)

    def rest(j, carry):
        group(pl.multiple_of(head + j * r, r), r)
        return carry

    lax.fori_loop(0, (jnp.maximum(count - head, 0) + r - 1) // r, rest, 0)


def _moe(x, hb, comb, rank, rank_t, cnt, wg, wu, wd, *, tm, tf):
    m, d = x.shape
    n_exp, _, f = wg.shape
    assert tm % MOE_ROWS == 0
    tok = lambda width: pl.BlockSpec((tm, width), lambda i, e, j, cnt: (i, 0))
    return pl.pallas_call(
        functools.partial(_moe_body, tm=tm),
        grid_spec=pltpu.PrefetchScalarGridSpec(
            num_scalar_prefetch=1,
            grid=(m // tm, n_exp, f // tf),
            in_specs=[pl.BlockSpec(memory_space=pl.ANY), tok(d), tok(LANES), tok(LANES),
                      pl.BlockSpec((SUBLANES, tm), lambda i, e, j, cnt: (0, i)),
                      pl.BlockSpec((1, d, tf), lambda i, e, j, cnt: (e, 0, j)),
                      pl.BlockSpec((1, d, tf), lambda i, e, j, cnt: (e, 0, j)),
                      pl.BlockSpec((1, tf, d), lambda i, e, j, cnt: (e, j, 0))],
            out_specs=tok(d),
            scratch_shapes=[pltpu.VMEM((tm, d), BF16), pltpu.VMEM((tm, d), F32), pltpu.SemaphoreType.DMA(())],
        ),
        out_shape=jax.ShapeDtypeStruct((m, d), F32),
        compiler_params=_cparams("parallel", "arbitrary", "arbitrary", vmem_limit_bytes=MOE_VMEM_LIMIT_BYTES),
        name="moe",
    )(cnt, x, hb, comb, rank, rank_t, wg, wu, wd)


def _moe_layer(x, a, wo, ln, rw, wg, wu, wd, *, tm, tf):
    x, hb, comb, rank, rank_t, cnt = _router(x, a, wo, ln, rw, tm=tm)
    return _moe(x, hb, comb, rank, rank_t, cnt.reshape(cnt.shape[0], LANES), wg, wu, wd, tm=tm, tf=tf)


def _block_diag(w):
    n, c, d = w.shape
    eye = jnp.eye(n, dtype=w.dtype)
    return (eye[:, None, :, None] * w[:, :, None, :]).reshape(n * c, n * d)


def _tile_rows(m, pref):
    return pref if m % pref == 0 else m


def kernel(x_prompt, x_sample, cache_k, cache_v, state_lru_conv, state_lru_h, state_delta_conv, state_delta_s, page_table, ln_mix_e, w_in_e, conv_lru_w, conv_lru_b, lru_wx, lru_bx, lru_wa, lru_ba, lru_lambda, qk_gain_q, qk_gain_k, lam_q1, lam_k1, lam_q2, lam_k2, subln_g, w_out_e, ln_ffn_e, ffn_wg, ffn_wu, ffn_wd, ln_mix_o, w_in_o, conv_delta_w, a_log, dt_bias, onorm_g, w_out_o, ln_ffn_o, router_w, moe_wg, moe_wu, moe_wd):
    bp, tp, d = x_prompt.shape
    bs = x_sample.shape[0]
    n_even, n_odd = w_in_e.shape[0], w_in_o.shape[0]
    lru_w = conv_lru_w.shape[-1]
    att_w = cache_k.shape[-1] * cache_k.shape[-2]
    qkv_c = conv_delta_w.shape[-1]
    dk = onorm_g.shape[-1]
    assert x_sample.shape[1] == 1 and lru_w == att_w and w_in_e.shape[-1] == 5 * lru_w

    xp = x_prompt.reshape(bp * tp, d)
    xs = x_sample.reshape(bs, d)
    mp = bp * tp
    tmp = _tile_rows(mp, 512)
    outs = {k: [] for k in ("kp", "vp", "cp", "hp", "dcp", "dsp", "ks", "vs", "cs", "hs", "dcs", "dss")}
    bf = lambda a: a.astype(BF16)
    vec = lambda a: a.reshape(1, -1)

    for layer in range(n_even + n_odd):
        i = layer // 2
        if layer % 2 == 0:
            lam_init = 0.8 - 0.6 * math.exp(-0.3 * layer)
            w_in, w_out = bf(w_in_e[i]), bf(w_out_e[i])
            lru_args = (conv_lru_w[i], vec(conv_lru_b[i]), bf(_block_diag(lru_wx[i])), vec(lru_bx[i]),
                        bf(_block_diag(lru_wa[i])), vec(lru_ba[i]), vec(lru_lambda[i]))
            lam_args = (lam_q1[i], lam_k1[i], lam_q2[i], lam_k2[i], subln_g[i])
            wg, wu, wd = bf(ffn_wg[i]), bf(ffn_wu[i]), bf(ffn_wd[i])

            gg, xr, qn, k_rows, v_rows, kb, vb = _inproj_even(xp, ln_mix_e[i], w_in, qk_gain_q[i],
                                                              qk_gain_k[i], tm=tmp)
            seq = lambda a: a.reshape(bp, tp, -1)
            ya, h_last = _lru_prompt(seq(xr), seq(gg), *lru_args, tt=_tile_rows(tp, 512))
            o = _attn_prompt(seq(qn), seq(kb), seq(vb), *lam_args, qk_gain_q[i], qk_gain_k[i],
                             lam_init=lam_init, tq=_tile_rows(tp, 256))
            xp = _ffn(xp, [(ya.reshape(mp, lru_w), w_out[:lru_w]), (o.reshape(mp, att_w), w_out[lru_w:])],
                      ln_ffn_e[i], wg, wu, wd, tm=tmp)
            outs["kp"].append(k_rows.reshape(bp, tp, H_B, att_w // H_B))
            outs["vp"].append(v_rows.reshape(bp, tp, H_B, att_w // H_B))
            outs["cp"].append(seq(xr)[:, tp - (CONV_W - 1):])
            outs["hp"].append(h_last.reshape(bp, lru_w))

            gg, xr, qn, k_rows, v_rows, _, _ = _inproj_even(xs, ln_mix_e[i], w_in, qk_gain_q[i],
                                                            qk_gain_k[i], tm=bs)
            ya, h_new = _lru_step(xr, gg, state_lru_conv[i], state_lru_h[i], *lru_args)
            o = _attn_decode(qn, k_rows.reshape(bs, att_w), v_rows.reshape(bs, att_w), cache_k[i], cache_v[i],
                             page_table, *lam_args,
                             lam_init=lam_init)
            xs = _ffn(xs, [(ya, w_out[:lru_w]), (o, w_out[lru_w:])], ln_ffn_e[i], wg, wu, wd, tm=bs)
            outs["ks"].append(k_rows.reshape(bs, 1, H_B, att_w // H_B))
            outs["vs"].append(v_rows.reshape(bs, 1, H_B, att_w // H_B))
            outs["cs"].append(jnp.concatenate([state_lru_conv[i][:, 1:], xr[:, None]], axis=1))
            outs["hs"].append(h_new)
        else:
            w_in, w_out = bf(w_in_o[i]), bf(w_out_o[i])
            wg, wu, wd = bf(moe_wg[i]), bf(moe_wu[i]), bf(moe_wd[i])
            tf = MOE_FF_TILE

            qkv, z, gb = _inproj_odd(xp, ln_mix_o[i], w_in, a_log[i], dt_bias[i], tm=tmp)
            seq = lambda a: a.reshape(bp, tp, -1)
            o, s_fin = _delta_prompt(seq(qkv), seq(z), seq(gb), conv_delta_w[i], onorm_g[i],
                                     tc=_tile_rows(tp, 256))
            xp = _moe_layer(xp, o.reshape(mp, -1), w_out, ln_ffn_o[i], router_w[i], wg, wu, wd,
                            tm=_tile_rows(mp, MOE_TOKENS), tf=tf)
            outs["dcp"].append(seq(qkv)[:, tp - (CONV_W - 1):])
            outs["dsp"].append(s_fin)

            qkv, z, gb = _inproj_odd(xs, ln_mix_o[i], w_in, a_log[i], dt_bias[i], tm=bs)
            o, s_new = _delta_step(qkv, state_delta_conv[i], z, gb, state_delta_s[i], conv_delta_w[i],
                                   onorm_g[i])
            xs = _moe_layer(xs, o, w_out, ln_ffn_o[i], router_w[i], wg, wu, wd, tm=bs, tf=tf)
            outs["dcs"].append(jnp.concatenate([state_delta_conv[i][:, 1:], qkv[:, None]], axis=1))
            outs["dss"].append(s_new)

    st = lambda k: jnp.stack(outs[k])
    return (xp.reshape(bp, tp, d), xs.reshape(bs, 1, d),
            st("kp"), st("vp"), st("cp"), st("hp"), st("dcp"), st("dsp"),
            st("ks"), st("vs"), st("cs"), st("hs"), st("dcs"), st("dss"))
```

```python
import functools
import math

import jax
import jax.numpy as jnp
from jax import lax
from jax.experimental import pallas as pl
from jax.experimental.pallas import tpu as pltpu

F32 = jnp.float32
BF16 = jnp.bfloat16
HIGHEST = lax.Precision.HIGHEST

CONV_W = 4
RG_C = 8.0
H_B = 4
H_C = 8
TOP_K = 2
EPS = 1e-6
NEG_INF = -1e30

LANES = 128
SUBLANES = 8
VMEM_LIMIT_BYTES = 52 * 1024 * 1024
MOE_VMEM_LIMIT_BYTES = 60 * 1024 * 1024

LRU_SEQS_PER_STEP = 4
DELTA_CHUNK = 64
DELTA_STEP_SEQS = 4
ATTN_HEADS_PER_STEP = 4
ATTN_FIXED_SHIFT_MAX = 40.0
MOE_ROWS = 128
MOE_HEAD_SIZES = (128, 256, 288, 320, 384)
MOE_TOKENS = 1024
MOE_FF_TILE = 1792
FFN_CHUNK = 512


def _cparams(*sem, vmem_limit_bytes=VMEM_LIMIT_BYTES):
    return pltpu.CompilerParams(dimension_semantics=sem, vmem_limit_bytes=vmem_limit_bytes)


def _full(shape):
    nd = len(shape)
    return pl.BlockSpec(shape, lambda *_: (0,) * nd)


def _rms(x, g):
    return x * lax.rsqrt(jnp.mean(x * x, axis=-1, keepdims=True) + EPS) * g


def _softplus(x):
    return jnp.maximum(x, 0.0) + jnp.log1p(jnp.exp(-jnp.abs(x)))


def _silu(x):
    return x * jax.nn.sigmoid(x)


def _bdot(a, b):
    return jnp.dot(a.astype(BF16), b.astype(BF16), preferred_element_type=F32)


def _bdot_nt(a, b):
    return lax.dot_general(a.astype(BF16), b.astype(BF16), (((1,), (1,)), ((), ())),
                           preferred_element_type=F32)


def _bdot_tn(a, b):
    return lax.dot_general(a.astype(BF16), b.astype(BF16), (((0,), (0,)), ((), ())),
                           preferred_element_type=F32)


def _split_bf16(x):
    hi = x.astype(BF16)
    return hi, (x - hi.astype(F32)).astype(BF16)


def _shifted_rows(prev8, x):
    full = jnp.concatenate([prev8, x], axis=0)
    return [pltpu.roll(full, s, axis=0)[SUBLANES:] for s in (1, 2, 3)]


def _inproj_even_body(x_ref, ln_ref, w_ref, gq_ref, gk_ref, seg_ref,
                      gg_ref, xr_ref, q_ref, k_ref, v_ref, kb_ref, vb_ref, *, width, dh):
    hb = _rms(x_ref[...], ln_ref[...]).astype(BF16)

    def proj(j):
        return jnp.dot(hb, w_ref[:, j * width:(j + 1) * width], preferred_element_type=F32)

    def qk_norm(t, g):
        ms = _bdot(t * t, seg_ref[...]) * (1.0 / dh)
        return t * lax.rsqrt(ms + EPS) * g

    gg_ref[...] = jax.nn.gelu(proj(0)).astype(BF16)
    xr_ref[...] = proj(1)
    q_ref[...] = qk_norm(proj(2), gq_ref[...]).astype(BF16)
    k = qk_norm(proj(3), gk_ref[...])
    kb_ref[...] = k.astype(BF16)
    v = proj(4)
    vb_ref[...] = v.astype(BF16)
    hw = width // H_B
    for h in range(H_B):
        k_ref[:, h, :] = k[:, h * hw:(h + 1) * hw]
        v_ref[:, h, :] = v[:, h * hw:(h + 1) * hw]


def _inproj_even(x, ln, w, gq, gk, *, tm):
    m, d = x.shape
    width = w.shape[1] // 5
    dh = gq.shape[-1]
    reps = width // dh
    seg = jnp.kron(jnp.eye(reps, dtype=F32), jnp.ones((dh, dh), F32)).astype(BF16)
    gq_t = jnp.tile(gq.reshape(1, dh), (1, reps))
    gk_t = jnp.tile(gk.reshape(1, dh), (1, reps))
    row = lambda i: (i, 0)
    blk = pl.BlockSpec((tm, width), row)
    f32o = jax.ShapeDtypeStruct((m, width), F32)
    b16o = jax.ShapeDtypeStruct((m, width), BF16)
    kvblk = pl.BlockSpec((tm, H_B, width // H_B), lambda i: (i, 0, 0))
    kvo = jax.ShapeDtypeStruct((m, H_B, width // H_B), F32)
    return pl.pallas_call(
        functools.partial(_inproj_even_body, width=width, dh=dh),
        grid=(m // tm,),
        in_specs=[pl.BlockSpec((tm, d), row), _full((1, d)), _full(w.shape),
                  _full((1, width)), _full((1, width)), _full((width, width))],
        out_specs=[blk, blk, blk, kvblk, kvblk, blk, blk],
        out_shape=[b16o, f32o, b16o, kvo, kvo, b16o, b16o],
        compiler_params=_cparams("parallel"),
        name="inproj_even",
    )(x, ln.reshape(1, d), w, gq_t, gk_t, seg)


def _lru_coeffs(xc, wx_ref, bx_ref, wa_ref, ba_ref, lam_ref):
    xb = xc.astype(BF16)
    gate_x = jax.nn.sigmoid(jnp.dot(xb, wx_ref[...], preferred_element_type=F32) + bx_ref[...])
    gate_a = jax.nn.sigmoid(jnp.dot(xb, wa_ref[...], preferred_element_type=F32) + ba_ref[...])
    log_a = -RG_C * gate_a * _softplus(-lam_ref[...])
    a = jnp.exp(log_a)
    th = jnp.tanh(log_a)
    one_minus_a2 = -2.0 * th / (1.0 - th)
    b = jnp.sqrt(one_minus_a2) * (gate_x * xc)
    return a, b


def _lru_prompt_body(xr_ref, gg_ref, cw_ref, cb_ref, wx_ref, bx_ref, wa_ref, ba_ref, lam_ref,
                     ya_ref, hlast_ref, prev_s, h_s, a_s, b_s, *, tt, nb):
    t = pl.program_id(1)

    @pl.when(t == 0)
    def _():
        prev_s[...] = jnp.zeros_like(prev_s)
        h_s[...] = jnp.zeros_like(h_s)

    for b in range(nb):
        xr = xr_ref[b]
        x1, x2, x3 = _shifted_rows(prev_s[b], xr)
        xc = (x3 * cw_ref[0:1, :] + x2 * cw_ref[1:2, :] + x1 * cw_ref[2:3, :] + xr * cw_ref[3:4, :]
              + cb_ref[...])
        prev_s[b] = xr[tt - SUBLANES:, :]
        a, bb = _lru_coeffs(xc, wx_ref, bx_ref, wa_ref, ba_ref, lam_ref)
        a_s[b] = a
        b_s[b] = bb

    def step(i, hs):
        out = []
        for b in range(nb):
            h = a_s[b, pl.ds(i, 1), :] * hs[b] + b_s[b, pl.ds(i, 1), :]
            b_s[b, pl.ds(i, 1), :] = h
            out.append(h)
        return tuple(out)

    hs = lax.fori_loop(0, tt, step, tuple(h_s[b] for b in range(nb)), unroll=8)
    for b in range(nb):
        h_s[b] = hs[b]
        ya_ref[b] = (gg_ref[b].astype(F32) * b_s[b]).astype(BF16)
        hlast_ref[b] = hs[b]


def _lru_prompt(xr, gg, cw, cb, wx, bx, wa, ba, lam, *, tt):
    b, t, w = xr.shape
    nb = LRU_SEQS_PER_STEP if b % LRU_SEQS_PER_STEP == 0 else 1
    seq = lambda i, j: (i, j, 0)
    vec = _full((1, w))
    return pl.pallas_call(
        functools.partial(_lru_prompt_body, tt=tt, nb=nb),
        grid=(b // nb, t // tt),
        in_specs=[pl.BlockSpec((nb, tt, w), seq), pl.BlockSpec((nb, tt, w), seq), _full((CONV_W, w)), vec,
                  _full((w, w)), vec, _full((w, w)), vec, vec],
        out_specs=[pl.BlockSpec((nb, tt, w), seq), pl.BlockSpec((nb, 1, w), lambda i, j: (i, 0, 0))],
        out_shape=[jax.ShapeDtypeStruct((b, t, w), BF16), jax.ShapeDtypeStruct((b, 1, w), F32)],
        scratch_shapes=[pltpu.VMEM((nb, SUBLANES, w), F32), pltpu.VMEM((nb, 1, w), F32),
                        pltpu.VMEM((nb, tt, w), F32), pltpu.VMEM((nb, tt, w), F32)],
        compiler_params=_cparams("parallel", "arbitrary"),
        name="lru_prompt",
    )(xr, gg, cw, cb, wx, bx, wa, ba, lam)


def _lru_step_body(xr_ref, gg_ref, b0_ref, b1_ref, b2_ref, h0_ref, cw_ref, cb_ref,
                   wx_ref, bx_ref, wa_ref, ba_ref, lam_ref, ya_ref, h_ref):
    xr = xr_ref[...]
    xc = (b0_ref[...] * cw_ref[0:1, :] + b1_ref[...] * cw_ref[1:2, :] + b2_ref[...] * cw_ref[2:3, :]
          + xr * cw_ref[3:4, :] + cb_ref[...])
    a, b = _lru_coeffs(xc, wx_ref, bx_ref, wa_ref, ba_ref, lam_ref)
    h = a * h0_ref[...] + b
    h_ref[...] = h
    ya_ref[...] = (gg_ref[...].astype(F32) * h).astype(BF16)


def _lru_step(xr, gg, buf, h0, cw, cb, wx, bx, wa, ba, lam):
    n, w = xr.shape
    args = (xr, gg, buf[:, 0], buf[:, 1], buf[:, 2], h0, cw, cb, wx, bx, wa, ba, lam)
    return pl.pallas_call(
        _lru_step_body,
        grid=(1,),
        in_specs=[_full(a.shape) for a in args],
        out_specs=[_full((n, w)), _full((n, w))],
        out_shape=[jax.ShapeDtypeStruct((n, w), BF16), jax.ShapeDtypeStruct((n, w), F32)],
        compiler_params=_cparams("arbitrary"),
        name="lru_step",
    )(*args)


def _diff_lambda(lq1_ref, lk1_ref, lq2_ref, lk2_ref, lam_init):
    s1 = jnp.sum(lq1_ref[...] * lk1_ref[...], axis=-1, keepdims=True)
    s2 = jnp.sum(lq2_ref[...] * lk2_ref[...], axis=-1, keepdims=True)
    return jnp.exp(s1) - jnp.exp(s2) + lam_init


def _attn_prompt_body(lq1_ref, lk1_ref, lq2_ref, lk2_ref, sg_ref, gq_ref, gk_ref, q_ref, k_ref, v_ref, o_ref,
                      *, tq, dh, nh, lam_init):
    i = pl.program_id(2)
    hw = 2 * dh
    heads = range(nh)
    lam = _diff_lambda(lq1_ref, lk1_ref, lq2_ref, lk2_ref, lam_init)
    lane = lax.broadcasted_iota(jnp.int32, (tq, hw), 1)
    qq = []
    for h in heads:
        q = q_ref[0, :, h * hw:(h + 1) * hw] * (dh ** -0.5)
        zero = jnp.zeros_like(q)
        qq.append(jnp.concatenate([jnp.where(lane < dh, q, zero), jnp.where(lane >= dh, q, zero)], axis=0))
    rowi = lax.broadcasted_iota(jnp.int32, (2 * tq, tq), 0)
    coli = lax.broadcasted_iota(jnp.int32, (2 * tq, tq), 1)
    causal = coli <= jnp.where(rowi >= tq, rowi - tq, rowi)

    def scores(j, masked):
        start = pl.multiple_of(j * tq, tq)
        kj = k_ref[0, pl.ds(start, tq), :]
        vj = v_ref[0, pl.ds(start, tq), :]
        s = [_bdot_nt(qq[h], kj[:, h * hw:(h + 1) * hw]) for h in heads]
        if masked:
            s = [jnp.where(causal, s[h], NEG_INF) for h in heads]
        return s, [vj[:, h * hw:(h + 1) * hw] for h in heads]

    def finish(l, acc):
        out = []
        for h in heads:
            o = acc[h] / l[h]
            od = o[:tq] - lam * o[tq:]
            out.append(_rms(od, sg_ref[...]) * (1.0 - lam_init))
        o_ref[0] = jnp.concatenate(out, axis=1).astype(BF16)

    bound = (dh ** 0.5) * 1.05 * jnp.max(jnp.abs(gq_ref[...])) * jnp.max(jnp.abs(gk_ref[...]))
    fixed_ok = bound <= ATTN_FIXED_SHIFT_MAX

    @pl.when(fixed_ok)
    def _():
        def block(j, carry, masked):
            s, vj = scores(j, masked)
            p = [jnp.exp(s[h] - bound) for h in heads]
            l = [carry[h][0] + sum(p[h][:, c0:c0 + LANES] for c0 in range(0, tq, LANES)) for h in heads]
            acc = [carry[h][1] + _bdot(p[h], vj[h]) for h in heads]
            return tuple((l[h], acc[h]) for h in heads)

        init = tuple((jnp.zeros((2 * tq, LANES), F32), jnp.zeros((2 * tq, hw), F32)) for _ in heads)
        carry = lax.fori_loop(0, i, lambda j, c: block(j, c, False), init)
        carry = block(i, carry, True)
        finish([jnp.sum(carry[h][0], axis=-1, keepdims=True) for h in heads], [carry[h][1] for h in heads])

    @pl.when(jnp.logical_not(fixed_ok))
    def _():
        def block(j, carry, masked):
            s, vj = scores(j, masked)
            m_new = [jnp.maximum(carry[h][0], jnp.max(s[h], axis=-1, keepdims=True)) for h in heads]
            alpha = [jnp.exp(carry[h][0] - m_new[h]) for h in heads]
            p = [jnp.exp(s[h] - m_new[h]) for h in heads]
            l = [alpha[h] * carry[h][1] + jnp.sum(p[h], axis=-1, keepdims=True) for h in heads]
            acc = [alpha[h] * carry[h][2] + _bdot(p[h], vj[h]) for h in heads]
            return tuple((m_new[h], l[h], acc[h]) for h in heads)

        init = tuple((jnp.full((2 * tq, 1), NEG_INF, F32), jnp.zeros((2 * tq, 1), F32),
                      jnp.zeros((2 * tq, hw), F32)) for _ in heads)
        carry = lax.fori_loop(0, i, lambda j, c: block(j, c, False), init)
        carry = block(i, carry, True)
        finish([carry[h][1] for h in heads], [carry[h][2] for h in heads])


def _attn_prompt(q, k, v, lq1, lk1, lq2, lk2, sub_g, gq, gk, *, lam_init, tq):
    b, t, w = q.shape
    hw = w // H_B
    dh = hw // 2
    vec = _full((1, dh))
    nh = ATTN_HEADS_PER_STEP
    assert tq % LANES == 0
    qblk = pl.BlockSpec((1, tq, nh * hw), lambda bi, hg, i: (bi, i, hg))
    kvblk = pl.BlockSpec((1, t, nh * hw), lambda bi, hg, i: (bi, 0, hg))
    return pl.pallas_call(
        functools.partial(_attn_prompt_body, tq=tq, dh=dh, nh=nh, lam_init=lam_init),
        grid=(b, H_B // nh, t // tq),
        in_specs=[vec, vec, vec, vec, _full((1, hw)), vec, vec, qblk, kvblk, kvblk],
        out_specs=qblk,
        out_shape=jax.ShapeDtypeStruct((b, t, w), BF16),
        compiler_params=_cparams("parallel", "parallel", "arbitrary"),
        name="attn_prompt",
    )(lq1.reshape(1, dh), lk1.reshape(1, dh), lq2.reshape(1, dh), lk2.reshape(1, dh),
      sub_g.reshape(1, hw), gq.reshape(1, dh), gk.reshape(1, dh), q, k, v)


def _attn_decode_body(pt_ref, lq1_ref, lk1_ref, lq2_ref, lk2_ref, sg_ref, q_ref, kn_ref, vn_ref, *refs,
                      n_pages, dh, lam_init):
    del pt_ref
    k_refs = refs[:n_pages]
    v_refs = refs[n_pages:2 * n_pages]
    o_ref = refs[2 * n_pages]
    hw = 2 * dh
    rows = 2 * H_B
    lam = _diff_lambda(lq1_ref, lk1_ref, lq2_ref, lk2_ref, lam_init)

    row = lax.broadcasted_iota(jnp.int32, (rows, hw), 0)
    lane = lax.broadcasted_iota(jnp.int32, (rows, hw), 1)

    def by_head(vec_ref):
        out = jnp.zeros((rows, hw), F32)
        for h in range(H_B):
            piece = vec_ref[0, :, h * hw:(h + 1) * hw].astype(F32)
            out = jnp.where(row // 2 == h, jnp.broadcast_to(piece, (rows, hw)), out)
        return out

    half = (lane >= dh) == (row % 2 == 1)
    wq = jnp.where(half, by_head(q_ref), 0.0) * (dh ** -0.5)
    kn = by_head(kn_ref)
    vn = by_head(vn_ref)
    s_new = jnp.sum(wq * kn, axis=-1, keepdims=True)

    n_rows = k_refs[0].shape[1]
    col_head = lax.broadcasted_iota(jnp.int32, (rows, n_rows), 1) % H_B
    own = col_head == lax.broadcasted_iota(jnp.int32, (rows, n_rows), 0) // 2
    wq_b = wq.astype(BF16)
    scores = [jnp.where(own, _bdot_nt(wq_b, k_refs[p][0]), NEG_INF) for p in range(n_pages)]
    m = s_new
    for s in scores:
        m = jnp.maximum(m, jnp.max(s, axis=-1, keepdims=True))
    p_new = jnp.exp(s_new - m)
    l = p_new
    acc = p_new * vn
    for p in range(n_pages):
        pr = jnp.exp(scores[p] - m)
        l = l + jnp.sum(pr, axis=-1, keepdims=True)
        acc = acc + _bdot(pr, v_refs[p][0])
    o = acc / l
    r_out = lax.broadcasted_iota(jnp.int32, (SUBLANES, rows), 0)
    r_in = lax.broadcasted_iota(jnp.int32, (SUBLANES, rows), 1)
    sel = (jnp.where(r_in == 2 * r_out, 1.0, 0.0) - lam * jnp.where(r_in == 2 * r_out + 1, 1.0, 0.0))
    od = jnp.dot(sel, o, preferred_element_type=F32, precision=HIGHEST)[:H_B]
    o_ref[0] = _rms(od, sg_ref[...]) * (1.0 - lam_init)


def _attn_decode(q, k_new, v_new, cache_k, cache_v, page_table, lq1, lk1, lq2, lk2, sub_g, *, lam_init):
    n, w = q.shape
    hw = w // H_B
    dh = hw // 2
    n_pages = page_table.shape[1]
    n_pool, page = cache_k.shape[0], cache_k.shape[1]
    ck = cache_k.reshape(n_pool, page * H_B, hw)
    cv = cache_v.reshape(n_pool, page * H_B, hw)
    vec = pl.BlockSpec((1, dh), lambda i, pt: (0, 0))
    tok = pl.BlockSpec((1, 1, w), lambda i, pt: (i, 0, 0))

    def page_spec(p):
        return pl.BlockSpec((1, page * H_B, hw), lambda i, pt: (pt[i, p], 0, 0))

    pages = [page_spec(p) for p in range(n_pages)]
    out = pl.pallas_call(
        functools.partial(_attn_decode_body, n_pages=n_pages, dh=dh, lam_init=lam_init),
        grid_spec=pltpu.PrefetchScalarGridSpec(
            num_scalar_prefetch=1,
            grid=(n,),
            in_specs=[vec, vec, vec, vec, pl.BlockSpec((1, hw), lambda i, pt: (0, 0)), tok, tok, tok]
            + pages + pages,
            out_specs=pl.BlockSpec((1, H_B, hw), lambda i, pt: (i, 0, 0)),
        ),
        out_shape=jax.ShapeDtypeStruct((n, H_B, hw), F32),
        compiler_params=_cparams("arbitrary"),
        name="attn_decode",
    )(page_table, lq1.reshape(1, dh), lk1.reshape(1, dh), lq2.reshape(1, dh), lk2.reshape(1, dh),
      sub_g.reshape(1, hw), q.reshape(n, 1, w), k_new.reshape(n, 1, w), v_new.reshape(n, 1, w),
      *([ck] * n_pages), *([cv] * n_pages))
    return out.reshape(n, w)


def _resident(shape):
    nd = len(shape)
    return pl.BlockSpec(shape, lambda *_: (0,) * nd, pipeline_mode=pl.Buffered(1))


def _mixer_residual(x, pair_refs):
    y = None
    for a_ref, w_ref in zip(pair_refs[0::2], pair_refs[1::2]):
        part = jnp.dot(a_ref[...].astype(BF16), w_ref[...], preferred_element_type=F32)
        y = part if y is None else y + part
    return x if y is None else x + y


def _ffn_body(*refs, chunks, n_pairs):
    x_ref, ln_ref = refs[:2]
    wg_ref, wu_ref, wd_ref, o_ref = refs[2 + 2 * n_pairs:]
    x = _mixer_residual(x_ref[...], refs[2:2 + 2 * n_pairs])
    hb = _rms(x, ln_ref[...]).astype(BF16)
    acc = x
    for lo, n in chunks:
        g = jnp.dot(hb, wg_ref[:, lo:lo + n], preferred_element_type=F32)
        u = jnp.dot(hb, wu_ref[:, lo:lo + n], preferred_element_type=F32)
        acc = acc + jnp.dot((_silu(g) * u).astype(BF16), wd_ref[lo:lo + n, :], preferred_element_type=F32)
    o_ref[...] = acc


def _ffn(x, pairs, ln, wg, wu, wd, *, tm):
    m, d = x.shape
    f = wg.shape[1]
    chunks = [(lo, min(FFN_CHUNK, f - lo)) for lo in range(0, f, FFN_CHUNK)]
    row = lambda i: (i, 0)
    pair_specs, pair_args = [], []
    for a, w in pairs:
        pair_specs += [pl.BlockSpec((tm, a.shape[1]), row), _full(w.shape)]
        pair_args += [a, w]
    return pl.pallas_call(
        functools.partial(_ffn_body, chunks=chunks, n_pairs=len(pairs)),
        grid=(m // tm,),
        in_specs=[pl.BlockSpec((tm, d), row), _full((1, d))] + pair_specs
        + [_resident(wg.shape), _resident(wu.shape), _resident(wd.shape)],
        out_specs=pl.BlockSpec((tm, d), row),
        out_shape=jax.ShapeDtypeStruct((m, d), F32),
        compiler_params=_cparams("parallel"),
        name="ffn",
    )(x, ln.reshape(1, d), *pair_args, wg, wu, wd)


def _inproj_odd_body(x_ref, ln_ref, w_ref, wba_ref, alog_ref, dtb_ref, qkv_ref, z_ref, gb_ref,
                     *, n_qkv, n_z, tn):
    hb = _rms(x_ref[...], ln_ref[...]).astype(BF16)
    for c in range(n_qkv // tn):
        qkv_ref[:, c * tn:(c + 1) * tn] = jnp.dot(hb, w_ref[:, c * tn:(c + 1) * tn],
                                                  preferred_element_type=F32)
    for c in range(n_z // tn):
        z_ref[:, c * tn:(c + 1) * tn] = jnp.dot(
            hb, w_ref[:, n_qkv + c * tn:n_qkv + (c + 1) * tn], preferred_element_type=F32).astype(BF16)
    ba = jnp.dot(hb, wba_ref[...], preferred_element_type=F32)
    lane = lax.broadcasted_iota(jnp.int32, ba.shape, 1)
    beta = jax.nn.sigmoid(ba)
    g = -jnp.exp(alog_ref[...]) * _softplus(ba + dtb_ref[...])
    gb_ref[...] = jnp.where(lane < H_C, beta, g)


def _inproj_odd(x, ln, w, a_log, dt_bias, *, tm):
    m, d = x.shape
    n_z = H_C * (d // H_C)
    n_qkv = w.shape[1] - n_z - 2 * H_C
    wba = jnp.zeros((d, LANES), BF16).at[:, :2 * H_C].set(w[:, n_qkv + n_z:])
    alog = jnp.zeros((1, LANES), F32).at[0, H_C:2 * H_C].set(a_log)
    dtb = jnp.zeros((1, LANES), F32).at[0, H_C:2 * H_C].set(dt_bias)
    row = lambda i: (i, 0)
    return pl.pallas_call(
        functools.partial(_inproj_odd_body, n_qkv=n_qkv, n_z=n_z, tn=512),
        grid=(m // tm,),
        in_specs=[pl.BlockSpec((tm, d), row), _full((1, d)), _full(w.shape), _full(wba.shape),
                  _full((1, LANES)), _full((1, LANES))],
        out_specs=[pl.BlockSpec((tm, n_qkv), row), pl.BlockSpec((tm, n_z), row),
                   pl.BlockSpec((tm, LANES), row)],
        out_shape=[jax.ShapeDtypeStruct((m, n_qkv), F32), jax.ShapeDtypeStruct((m, n_z), BF16),
                   jax.ShapeDtypeStruct((m, LANES), F32)],
        compiler_params=_cparams("parallel"),
        name="inproj_odd",
    )(x, ln.reshape(1, d), w, wba, alog, dtb)


def _l2n(x):
    return x * lax.rsqrt(jnp.sum(x * x, axis=-1, keepdims=True) + EPS)


def _delta_prompt_body(qkv_ref, z_ref, gb_ref, cw_ref, og_ref, o_ref, s_ref, prev_s, xc_s, *, tc, dk):
    t = pl.program_id(1)
    c = DELTA_CHUNK
    nk = H_C * dk

    @pl.when(t == 0)
    def _():
        prev_s[...] = jnp.zeros_like(prev_s)
        s_ref[...] = jnp.zeros_like(s_ref)

    x = qkv_ref[0]
    x1, x2, x3 = _shifted_rows(prev_s[...], x)
    xc_s[...] = _silu(x3 * cw_ref[0:1, :] + x2 * cw_ref[1:2, :] + x1 * cw_ref[2:3, :] + x * cw_ref[3:4, :])
    prev_s[...] = x[tc - SUBLANES:, :]

    ri = lax.broadcasted_iota(jnp.int32, (c, c), 0)
    ci = lax.broadcasted_iota(jnp.int32, (c, c), 1)
    incl = ri >= ci
    strict = ri > ci
    tri = jnp.where(incl, 1.0, 0.0)
    eye = jnp.where(ri == ci, 1.0, 0.0)
    heads = range(H_C)
    n_chunks = tc // c

    q, k, v, beta, e_g, decay, g_end, k_dec = ([] for _ in range(8))
    for ic in range(n_chunks):
        rows = slice(ic * c, (ic + 1) * c)
        gbc = gb_ref[0, rows, :]
        g_all = jnp.dot(tri, gbc, preferred_element_type=F32, precision=HIGHEST)
        g_t = g_all.T
        for h in heads:
            g_col = g_all[:, H_C + h:H_C + h + 1]
            g_row = g_t[H_C + h:H_C + h + 1, :]
            g_last = g_all[c - 1:c, H_C + h:H_C + h + 1]
            kn = _l2n(xc_s[rows, nk + h * dk:nk + (h + 1) * dk])
            q.append(_l2n(xc_s[rows, h * dk:(h + 1) * dk]) * (dk ** -0.5))
            k.append(kn)
            v.append(xc_s[rows, 2 * nk + h * dk:2 * nk + (h + 1) * dk])
            beta.append(gbc[:, h:h + 1])
            e_g.append(jnp.exp(g_col))
            decay.append(jnp.where(incl, jnp.exp(jnp.where(incl, g_col - g_row, 0.0)), 0.0))
            g_end.append(jnp.exp(g_last))
            k_dec.append(kn * jnp.exp(g_last - g_col))
    pairs = range(n_chunks * H_C)
    kb = [k[i] * beta[i] for i in pairs]
    kk = [_bdot_nt(jnp.concatenate([kb[i], q[i]], axis=0), k[i]) for i in pairs]
    qk = [kk[i][c:] * decay[i] for i in pairs]
    pw = [-jnp.where(strict, kk[i][:c] * decay[i], 0.0) for i in pairs]
    inv = [eye + pw[i] for i in pairs]
    for _ in range(int(math.log2(c)) - 1):
        pw = [_bdot(pw[i], pw[i]) for i in pairs]
        inv = [inv[i] + _bdot(inv[i], pw[i]) for i in pairs]
    sol = [_bdot(inv[i], jnp.concatenate([v[i] * beta[i], kb[i] * e_g[i]], axis=1)) for i in pairs]
    wq = [jnp.concatenate([sol[i][:, dk:], q[i] * e_g[i]], axis=0).astype(BF16) for i in pairs]

    s = [s_ref[0, h] for h in heads]
    for ic in range(n_chunks):
        rows = slice(ic * c, (ic + 1) * c)
        idx = [ic * H_C + h for h in heads]
        ws = [_bdot(wq[i], s[h]) for h, i in zip(heads, idx)]
        v_new = [sol[i][:, :dk] - ws[h][:c] for h, i in zip(heads, idx)]
        o = [ws[h][c:] + _bdot(qk[i], v_new[h]) for h, i in zip(heads, idx)]
        s = [s[h] * g_end[i] + _bdot_tn(k_dec[i], v_new[h]) for h, i in zip(heads, idx)]
        zt = z_ref[0, rows, :].astype(F32)
        o_ref[0, rows, :] = jnp.concatenate(
            [_rms(o[h], og_ref[...]) * _silu(zt[:, h * dk:(h + 1) * dk]) for h in heads], axis=1).astype(BF16)
    for h in heads:
        s_ref[0, h] = s[h]


def _delta_prompt(qkv, z, gb, cw, og, *, tc):
    b, t, nq = qkv.shape
    nz = z.shape[2]
    dk = nz // H_C
    seq = lambda i, j: (i, j, 0)
    return pl.pallas_call(
        functools.partial(_delta_prompt_body, tc=tc, dk=dk),
        grid=(b, t // tc),
        in_specs=[pl.BlockSpec((1, tc, nq), seq), pl.BlockSpec((1, tc, nz), seq),
                  pl.BlockSpec((1, tc, LANES), seq), _full((CONV_W, nq)), _full((1, dk))],
        out_specs=[pl.BlockSpec((1, tc, nz), seq),
                   pl.BlockSpec((1, H_C, dk, dk), lambda i, j: (i, 0, 0, 0))],
        out_shape=[jax.ShapeDtypeStruct((b, t, nz), BF16), jax.ShapeDtypeStruct((b, H_C, dk, dk), F32)],
        scratch_shapes=[pltpu.VMEM((SUBLANES, nq), F32), pltpu.VMEM((tc, nq), F32)],
        compiler_params=_cparams("parallel", "arbitrary"),
        name="delta_prompt",
    )(qkv, z, gb, cw, og.reshape(1, dk))


def _delta_prep_step_body(x_ref, b0_ref, b1_ref, b2_ref, cw_ref, q_ref, k_ref, v_ref, *, dk):
    xc = _silu(b0_ref[...] * cw_ref[0:1, :] + b1_ref[...] * cw_ref[1:2, :] + b2_ref[...] * cw_ref[2:3, :]
               + x_ref[...] * cw_ref[3:4, :])
    nk = H_C * dk
    for h in range(H_C):
        sl = slice(h * dk, (h + 1) * dk)
        q_ref[:, sl] = _l2n(xc[:, h * dk:(h + 1) * dk]) * (dk ** -0.5)
        k_ref[:, sl] = _l2n(xc[:, nk + h * dk:nk + (h + 1) * dk])
    v_ref[...] = xc[:, 2 * nk:]


def _delta_step_body(q_ref, k_ref, v_ref, gb_ref, z_ref, og_ref, s_ref, o_ref, sn_ref, *, dk, ns):
    row = lax.broadcasted_iota(jnp.int32, (SUBLANES, dk), 0)
    units = [(j, h) for j in range(ns) for h in range(H_C)]
    rows8 = lambda x: jnp.broadcast_to(x, (SUBLANES, dk))
    q = [q_ref[j, :, h * dk:(h + 1) * dk] for j, h in units]
    k = [k_ref[j, :, h * dk:(h + 1) * dk] for j, h in units]
    v = [v_ref[j, :, h * dk:(h + 1) * dk] for j, h in units]
    beta = [gb_ref[j, :, h:h + 1] for j, h in units]
    e_g = [jnp.exp(gb_ref[j, :, H_C + h:H_C + h + 1]) for j, h in units]
    s = [s_ref[j, h] for j, h in units]
    n = range(len(units))
    kq = [jnp.where(row == 0, rows8(k[u]), jnp.where(row == 1, rows8(q[u] * e_g[u]), 0.0)) for u in n]
    ks = [_bdot(kq[u], s[u]) for u in n]
    v_new = [beta[u] * (v[u] - e_g[u] * ks[u][0:1]) for u in n]
    o = [ks[u][1:2] + jnp.sum(q[u] * k[u], axis=-1, keepdims=True) * v_new[u] for u in n]
    k_hi, k_lo = zip(*[_split_bf16(jnp.where(row == 0, rows8(k[u]), 0.0)) for u in n])
    v_hi, v_lo = zip(*[_split_bf16(jnp.where(row == 0, rows8(v_new[u]), 0.0)) for u in n])
    outer = [_bdot_tn(k_hi[u], v_hi[u]) + _bdot_tn(k_hi[u], v_lo[u]) + _bdot_tn(k_lo[u], v_hi[u]) for u in n]
    for u, (j, h) in enumerate(units):
        sn_ref[j, h] = s[u] * e_g[u] + outer[u]
    for j in range(ns):
        zt = z_ref[j].astype(F32)
        o_ref[j] = jnp.concatenate(
            [_rms(o[j * H_C + h], og_ref[...]) * _silu(zt[:, h * dk:(h + 1) * dk]) for h in range(H_C)],
            axis=1).astype(BF16)


def _delta_step(qkv, buf, z, gb, s0, cw, og):
    n, nq = qkv.shape
    nz = z.shape[1]
    dk = nz // H_C
    args = (qkv, buf[:, 0], buf[:, 1], buf[:, 2], cw)
    q, k, v = pl.pallas_call(
        functools.partial(_delta_prep_step_body, dk=dk),
        grid=(1,),
        in_specs=[_full(a.shape) for a in args],
        out_specs=[_full((n, nz))] * 3,
        out_shape=[jax.ShapeDtypeStruct((n, nz), F32)] * 3,
        compiler_params=_cparams("arbitrary"),
        name="delta_prep_step",
    )(*args)
    ns = DELTA_STEP_SEQS if n % DELTA_STEP_SEQS == 0 else 1
    tok = lambda width: pl.BlockSpec((ns, 1, width), lambda i: (i, 0, 0))
    st = pl.BlockSpec((ns, H_C, dk, dk), lambda i: (i, 0, 0, 0))
    o, s_new = pl.pallas_call(
        functools.partial(_delta_step_body, dk=dk, ns=ns),
        grid=(n // ns,),
        in_specs=[tok(nz), tok(nz), tok(nz), tok(LANES), tok(nz), _full((1, dk)), st],
        out_specs=[tok(nz), st],
        out_shape=[jax.ShapeDtypeStruct((n, 1, nz), BF16), jax.ShapeDtypeStruct(s0.shape, F32)],
        compiler_params=_cparams("parallel"),
        name="delta_step",
    )(q.reshape(n, 1, nz), k.reshape(n, 1, nz), v.reshape(n, 1, nz), gb.reshape(n, 1, LANES),
      z.reshape(n, 1, nz), og.reshape(1, dk), s0)
    return o.reshape(n, nz), s_new


ROUTER_ROWS = 1024
SCATTER_ROWS = 512


def _router_body(x_ref, a_ref, wo_ref, ln_ref, rw_ref, xo_ref, hb_ref, comb_ref, rank_ref, rank_t_ref, cnt_ref,
                 *, n_exp, tm):
    tb = min(ROUTER_ROWS, tm)
    lane = lax.broadcasted_iota(jnp.int32, (tb, LANES), 1).astype(F32)
    ri = lax.broadcasted_iota(jnp.int32, (tb, tb), 0)
    ci = lax.broadcasted_iota(jnp.int32, (tb, tb), 1)
    before = jnp.where(ri > ci, 1.0, 0.0).astype(BF16)
    running = jnp.zeros((1, LANES), F32)
    for t0 in range(0, tm, tb):
        rows = slice(t0, t0 + tb)
        x = x_ref[rows, :] + jnp.dot(a_ref[rows, :], wo_ref[...], preferred_element_type=F32)
        xo_ref[rows, :] = x
        h = _rms(x, ln_ref[...])
        hb_ref[rows, :] = h.astype(BF16)
        h_hi, h_lo = _split_bf16(h)
        w_hi, w_lo = _split_bf16(rw_ref[...])
        logits = _bdot(h_hi, w_hi) + _bdot(h_lo, w_hi) + _bdot(h_hi, w_lo)
        logits = jnp.where(lane < n_exp, logits, -jnp.inf)
        m1 = jnp.max(logits, axis=-1, keepdims=True)
        i1 = jnp.min(jnp.where(logits == m1, lane, float(LANES)), axis=-1, keepdims=True)
        rest = jnp.where(lane == i1, -jnp.inf, logits)
        m2 = jnp.max(rest, axis=-1, keepdims=True)
        i2 = jnp.min(jnp.where(rest == m2, lane, float(LANES)), axis=-1, keepdims=True)
        e = jnp.exp(m2 - m1)
        g1 = 1.0 / (1.0 + e)
        g2 = e / (1.0 + e)
        comb = jnp.where(lane == i1, g1, 0.0) + jnp.where(lane == i2, g2, 0.0)
        comb_ref[rows, :] = comb
        sel = comb != 0.0
        self_f = jnp.where(sel, 1.0, 0.0)
        local = jnp.dot(before, self_f.astype(BF16), preferred_element_type=F32)
        rank = jnp.where(sel, local + running, -1.0)
        rank_ref[rows, :] = rank
        rank_t_ref[:, rows] = rank.T[:SUBLANES, :]
        running = running + jnp.sum(self_f, axis=0, keepdims=True)
    cnt_ref[0] = running.astype(jnp.int32)


def _router(x, a, wo, ln, rw, *, tm):
    m, d = x.shape
    n_exp = rw.shape[1]
    assert n_exp <= SUBLANES
    rwp = jnp.zeros((d, LANES), F32).at[:, :n_exp].set(rw)
    row = lambda i: (i, 0)
    return pl.pallas_call(
        functools.partial(_router_body, n_exp=n_exp, tm=tm),
        grid=(m // tm,),
        in_specs=[pl.BlockSpec((tm, d), row), pl.BlockSpec((tm, a.shape[1]), row), _full(wo.shape),
                  _full((1, d)), _full((d, LANES))],
        out_specs=[pl.BlockSpec((tm, d), row), pl.BlockSpec((tm, d), row), pl.BlockSpec((tm, LANES), row),
                   pl.BlockSpec((tm, LANES), row), pl.BlockSpec((SUBLANES, tm), lambda i: (0, i)),
                   pl.BlockSpec((1, 1, LANES), lambda i: (i, 0, 0))],
        out_shape=[jax.ShapeDtypeStruct((m, d), F32), jax.ShapeDtypeStruct((m, d), BF16),
                   jax.ShapeDtypeStruct((m, LANES), F32), jax.ShapeDtypeStruct((m, LANES), F32),
                   jax.ShapeDtypeStruct((SUBLANES, m), F32),
                   jax.ShapeDtypeStruct((m // tm, 1, LANES), jnp.int32)],
        compiler_params=_cparams("parallel"),
        name="router",
    )(x, a, wo, ln.reshape(1, d), rwp)


def _moe_body(cnt_ref, x_hbm, hb_ref, comb_ref, rank_ref, rank_t_ref, wg_ref, wu_ref, wd_ref, o_ref,
              xs_s, ys_s, x_sem, *, tm):
    i, e, f = pl.program_id(0), pl.program_id(1), pl.program_id(2)
    n_f = pl.num_programs(2)
    count = cnt_ref[i, e]
    r = MOE_ROWS

    @pl.when(jnp.logical_and(e == 0, f == 0))
    def _():
        rows = pl.ds(pl.multiple_of(i * tm, tm), tm)
        copy = pltpu.make_async_copy(x_hbm.at[rows, :], o_ref, x_sem)
        copy.start()
        copy.wait()

    def group(lo, n):
        rows = pl.ds(lo, n)

        @pl.when(f == 0)
        def _():
            want = (lax.broadcasted_iota(jnp.int32, (n, tm), 0) + lo).astype(F32)
            onehot = jnp.where(rank_t_ref[pl.ds(e, 1), :] == want, 1.0, 0.0).astype(BF16)
            xs_s[rows, :] = jnp.dot(onehot, hb_ref[...], preferred_element_type=F32).astype(BF16)

        xs = xs_s[rows, :]
        g = jnp.dot(xs, wg_ref[0], preferred_element_type=F32)
        u = jnp.dot(xs, wu_ref[0], preferred_element_type=F32)
        y = jnp.dot((_silu(g) * u).astype(BF16), wd_ref[0], preferred_element_type=F32)

        @pl.when(f == 0)
        def _():
            ys_s[rows, :] = y

        @pl.when(f > 0)
        def _():
            ys_s[rows, :] += y

        @pl.when(f == n_f - 1)
        def _():
            tb = min(SCATTER_ROWS, tm)
            pick = lax.broadcasted_iota(jnp.int32, (tb, LANES), 1) == e
            want = (lax.broadcasted_iota(jnp.int32, (tb, n), 1) + lo).astype(F32)
            yb = ys_s[rows, :].astype(BF16)
            for t0 in range(0, tm, tb):
                rank_col = jnp.sum(jnp.where(pick, rank_ref[t0:t0 + tb, :], 0.0), axis=-1, keepdims=True)
                gate_col = jnp.sum(jnp.where(pick, comb_ref[t0:t0 + tb, :], 0.0), axis=-1, keepdims=True)
                onehot = jnp.where(rank_col == want, 1.0, 0.0).astype(BF16)
                o_ref[t0:t0 + tb, :] += gate_col * jnp.dot(onehot, yb, preferred_element_type=F32)

    sizes = [n for n in MOE_HEAD_SIZES if n <= tm]
    head = sizes[-1] if sizes else 0
    head_count = jnp.minimum(count, head)
    for below, n in zip([0] + sizes[:-1], sizes):
        @pl.when(jnp.logical_and(head_count > below, head_count <= n))
        def _(n=n):
            group(0, n)

    def rest(j, carry):
        group(pl.multiple_of(head + j * r, r), r)
        return carry

    lax.fori_loop(0, (jnp.maximum(count - head, 0) + r - 1) // r, rest, 0)


def _moe(x, hb, comb, rank, rank_t, cnt, wg, wu, wd, *, tm, tf):
    m, d = x.shape
    n_exp, _, f = wg.shape
    assert tm % MOE_ROWS == 0
    tok = lambda width: pl.BlockSpec((tm, width), lambda i, e, j, cnt: (i, 0))
    return pl.pallas_call(
        functools.partial(_moe_body, tm=tm),
        grid_spec=pltpu.PrefetchScalarGridSpec(
            num_scalar_prefetch=1,
            grid=(m // tm, n_exp, f // tf),
            in_specs=[pl.BlockSpec(memory_space=pl.ANY), tok(d), tok(LANES), tok(LANES),
                      pl.BlockSpec((SUBLANES, tm), lambda i, e, j, cnt: (0, i)),
                      pl.BlockSpec((1, d, tf), lambda i, e, j, cnt: (e, 0, j)),
                      pl.BlockSpec((1, d, tf), lambda i, e, j, cnt: (e, 0, j)),
                      pl.BlockSpec((1, tf, d), lambda i, e, j, cnt: (e, j, 0))],
            out_specs=tok(d),
            scratch_shapes=[pltpu.VMEM((tm, d), BF16), pltpu.VMEM((tm, d), F32), pltpu.SemaphoreType.DMA(())],
        ),
        out_shape=jax.ShapeDtypeStruct((m, d), F32),
        compiler_params=_cparams("parallel", "arbitrary", "arbitrary", vmem_limit_bytes=MOE_VMEM_LIMIT_BYTES),
        name="moe",
    )(cnt, x, hb, comb, rank, rank_t, wg, wu, wd)


def _moe_layer(x, a, wo, ln, rw, wg, wu, wd, *, tm, tf):
    x, hb, comb, rank, rank_t, cnt = _router(x, a, wo, ln, rw, tm=tm)
    return _moe(x, hb, comb, rank, rank_t, cnt.reshape(cnt.shape[0], LANES), wg, wu, wd, tm=tm, tf=tf)


def _block_diag(w):
    n, c, d = w.shape
    eye = jnp.eye(n, dtype=w.dtype)
    return (eye[:, None, :, None] * w[:, :, None, :]).reshape(n * c, n * d)


def _tile_rows(m, pref):
    return pref if m % pref == 0 else m


def kernel(x_prompt, x_sample, cache_k, cache_v, state_lru_conv, state_lru_h, state_delta_conv, state_delta_s, page_table, ln_mix_e, w_in_e, conv_lru_w, conv_lru_b, lru_wx, lru_bx, lru_wa, lru_ba, lru_lambda, qk_gain_q, qk_gain_k, lam_q1, lam_k1, lam_q2, lam_k2, subln_g, w_out_e, ln_ffn_e, ffn_wg, ffn_wu, ffn_wd, ln_mix_o, w_in_o, conv_delta_w, a_log, dt_bias, onorm_g, w_out_o, ln_ffn_o, router_w, moe_wg, moe_wu, moe_wd):
    bp, tp, d = x_prompt.shape
    bs = x_sample.shape[0]
    n_even, n_odd = w_in_e.shape[0], w_in_o.shape[0]
    lru_w = conv_lru_w.shape[-1]
    att_w = cache_k.shape[-1] * cache_k.shape[-2]
    qkv_c = conv_delta_w.shape[-1]
    dk = onorm_g.shape[-1]
    assert x_sample.shape[1] == 1 and lru_w == att_w and w_in_e.shape[-1] == 5 * lru_w

    xp = x_prompt.reshape(bp * tp, d)
    xs = x_sample.reshape(bs, d)
    mp = bp * tp
    tmp = _tile_rows(mp, 512)
    outs = {k: [] for k in ("kp", "vp", "cp", "hp", "dcp", "dsp", "ks", "vs", "cs", "hs", "dcs", "dss")}
    bf = lambda a: a.astype(BF16)
    vec = lambda a: a.reshape(1, -1)

    for layer in range(n_even + n_odd):
        i = layer // 2
        if layer % 2 == 0:
            lam_init = 0.8 - 0.6 * math.exp(-0.3 * layer)
            w_in, w_out = bf(w_in_e[i]), bf(w_out_e[i])
            lru_args = (conv_lru_w[i], vec(conv_lru_b[i]), bf(_block_diag(lru_wx[i])), vec(lru_bx[i]),
                        bf(_block_diag(lru_wa[i])), vec(lru_ba[i]), vec(lru_lambda[i]))
            lam_args = (lam_q1[i], lam_k1[i], lam_q2[i], lam_k2[i], subln_g[i])
            wg, wu, wd = bf(ffn_wg[i]), bf(ffn_wu[i]), bf(ffn_wd[i])

            gg, xr, qn, k_rows, v_rows, kb, vb = _inproj_even(xp, ln_mix_e[i], w_in, qk_gain_q[i],
                                                              qk_gain_k[i], tm=tmp)
            seq = lambda a: a.reshape(bp, tp, -1)
            ya, h_last = _lru_prompt(seq(xr), seq(gg), *lru_args, tt=_tile_rows(tp, 512))
            o = _attn_prompt(seq(qn), seq(kb), seq(vb), *lam_args, qk_gain_q[i], qk_gain_k[i],
                             lam_init=lam_init, tq=_tile_rows(tp, 256))
            xp = _ffn(xp, [(ya.reshape(mp, lru_w), w_out[:lru_w]), (o.reshape(mp, att_w), w_out[lru_w:])],
                      ln_ffn_e[i], wg, wu, wd, tm=tmp)
            outs["kp"].append(k_rows.reshape(bp, tp, H_B, att_w // H_B))
            outs["vp"].append(v_rows.reshape(bp, tp, H_B, att_w // H_B))
            outs["cp"].append(seq(xr)[:, tp - (CONV_W - 1):])
            outs["hp"].append(h_last.reshape(bp, lru_w))

            gg, xr, qn, k_rows, v_rows, _, _ = _inproj_even(xs, ln_mix_e[i], w_in, qk_gain_q[i],
                                                            qk_gain_k[i], tm=bs)
            ya, h_new = _lru_step(xr, gg, state_lru_conv[i], state_lru_h[i], *lru_args)
            o = _attn_decode(qn, k_rows.reshape(bs, att_w), v_rows.reshape(bs, att_w), cache_k[i], cache_v[i],
                             page_table, *lam_args,
                             lam_init=lam_init)
            xs = _ffn(xs, [(ya, w_out[:lru_w]), (o, w_out[lru_w:])], ln_ffn_e[i], wg, wu, wd, tm=bs)
            outs["ks"].append(k_rows.reshape(bs, 1, H_B, att_w // H_B))
            outs["vs"].append(v_rows.reshape(bs, 1, H_B, att_w // H_B))
            outs["cs"].append(jnp.concatenate([state_lru_conv[i][:, 1:], xr[:, None]], axis=1))
            outs["hs"].append(h_new)
        else:
            w_in, w_out = bf(w_in_o[i]), bf(w_out_o[i])
            wg, wu, wd = bf(moe_wg[i]), bf(moe_wu[i]), bf(moe_wd[i])
            tf = MOE_FF_TILE

            qkv, z, gb = _inproj_odd(xp, ln_mix_o[i], w_in, a_log[i], dt_bias[i], tm=tmp)
            seq = lambda a: a.reshape(bp, tp, -1)
            o, s_fin = _delta_prompt(seq(qkv), seq(z), seq(gb), conv_delta_w[i], onorm_g[i],
                                     tc=_tile_rows(tp, 256))
            xp = _moe_layer(xp, o.reshape(mp, -1), w_out, ln_ffn_o[i], router_w[i], wg, wu, wd,
                            tm=_tile_rows(mp, MOE_TOKENS), tf=tf)
            outs["dcp"].append(seq(qkv)[:, tp - (CONV_W - 1):])
            outs["dsp"].append(s_fin)

            qkv, z, gb = _inproj_odd(xs, ln_mix_o[i], w_in, a_log[i], dt_bias[i], tm=bs)
            o, s_new = _delta_step(qkv, state_delta_conv[i], z, gb, state_delta_s[i], conv_delta_w[i],
                                   onorm_g[i])
            xs = _moe_layer(xs, o, w_out, ln_ffn_o[i], router_w[i], wg, wu, wd, tm=bs, tf=tf)
            outs["dcs"].append(jnp.concatenate([state_delta_conv[i][:, 1:], qkv[:, None]], axis=1))
            outs["dss"].append(s_new)

    st = lambda k: jnp.stack(outs[k])
    return (xp.reshape(bp, tp, d), xs.reshape(bs, 1, d),
            st("kp"), st("vp"), st("cp"), st("hp"), st("dcp"), st("dsp"),
            st("ks"), st("vs"), st("cs"), st("hs"), st("dcs"), st("dss"))
```

```python
import functools
import math

import jax
import jax.numpy as jnp
from jax import lax
from jax.experimental import pallas as pl
from jax.experimental.pallas import tpu as pltpu

F32 = jnp.float32
BF16 = jnp.bfloat16
HIGHEST = lax.Precision.HIGHEST

CONV_W = 4
RG_C = 8.0
H_B = 4
H_C = 8
TOP_K = 2
EPS = 1e-6
NEG_INF = -1e30

LANES = 128
SUBLANES = 8
VMEM_LIMIT_BYTES = 52 * 1024 * 1024
MOE_VMEM_LIMIT_BYTES = 60 * 1024 * 1024

LRU_SEQS_PER_STEP = 4
DELTA_CHUNK = 64
DELTA_STEP_SEQS = 4
ATTN_ROWS = 512
ATTN_HEADS_PER_STEP = 4
ATTN_FIXED_SHIFT_MAX = 40.0
MOE_ROWS = 128
MOE_HEAD_SIZES = (128, 256, 288, 320, 384)
MOE_TOKENS = 1024
MOE_FF_TILE = 1792
FFN_CHUNK = 512


def _cparams(*sem, vmem_limit_bytes=VMEM_LIMIT_BYTES):
    return pltpu.CompilerParams(dimension_semantics=sem, vmem_limit_bytes=vmem_limit_bytes)


def _full(shape):
    nd = len(shape)
    return pl.BlockSpec(shape, lambda *_: (0,) * nd)


def _rms(x, g):
    return x * lax.rsqrt(jnp.mean(x * x, axis=-1, keepdims=True) + EPS) * g


def _softplus(x):
    return jnp.maximum(x, 0.0) + jnp.log1p(jnp.exp(-jnp.abs(x)))


def _silu(x):
    return x * jax.nn.sigmoid(x)


def _bdot(a, b):
    return jnp.dot(a.astype(BF16), b.astype(BF16), preferred_element_type=F32)


def _bdot_nt(a, b):
    return lax.dot_general(a.astype(BF16), b.astype(BF16), (((1,), (1,)), ((), ())),
                           preferred_element_type=F32)


def _bdot_tn(a, b):
    return lax.dot_general(a.astype(BF16), b.astype(BF16), (((0,), (0,)), ((), ())),
                           preferred_element_type=F32)


def _split_bf16(x):
    hi = x.astype(BF16)
    return hi, (x - hi.astype(F32)).astype(BF16)


def _shifted_rows(prev8, x):
    full = jnp.concatenate([prev8, x], axis=0)
    return [pltpu.roll(full, s, axis=0)[SUBLANES:] for s in (1, 2, 3)]


def _inproj_even_body(x_ref, ln_ref, w_ref, gq_ref, gk_ref, seg_ref,
                      gg_ref, xr_ref, q_ref, k_ref, v_ref, kb_ref, vb_ref, *, width, dh):
    hb = _rms(x_ref[...], ln_ref[...]).astype(BF16)

    def proj(j):
        return jnp.dot(hb, w_ref[:, j * width:(j + 1) * width], preferred_element_type=F32)

    def qk_norm(t, g):
        ms = _bdot(t * t, seg_ref[...]) * (1.0 / dh)
        return t * lax.rsqrt(ms + EPS) * g

    gg_ref[...] = jax.nn.gelu(proj(0)).astype(BF16)
    xr_ref[...] = proj(1)
    q_ref[...] = qk_norm(proj(2), gq_ref[...]).astype(BF16)
    k = qk_norm(proj(3), gk_ref[...])
    kb_ref[...] = k.astype(BF16)
    v = proj(4)
    vb_ref[...] = v.astype(BF16)
    hw = width // H_B
    for h in range(H_B):
        k_ref[:, h, :] = k[:, h * hw:(h + 1) * hw]
        v_ref[:, h, :] = v[:, h * hw:(h + 1) * hw]


def _inproj_even(x, ln, w, gq, gk, *, tm):
    m, d = x.shape
    width = w.shape[1] // 5
    dh = gq.shape[-1]
    reps = width // dh
    seg = jnp.kron(jnp.eye(reps, dtype=F32), jnp.ones((dh, dh), F32)).astype(BF16)
    gq_t = jnp.tile(gq.reshape(1, dh), (1, reps))
    gk_t = jnp.tile(gk.reshape(1, dh), (1, reps))
    row = lambda i: (i, 0)
    blk = pl.BlockSpec((tm, width), row)
    f32o = jax.ShapeDtypeStruct((m, width), F32)
    b16o = jax.ShapeDtypeStruct((m, width), BF16)
    kvblk = pl.BlockSpec((tm, H_B, width // H_B), lambda i: (i, 0, 0))
    kvo = jax.ShapeDtypeStruct((m, H_B, width // H_B), F32)
    return pl.pallas_call(
        functools.partial(_inproj_even_body, width=width, dh=dh),
        grid=(m // tm,),
        in_specs=[pl.BlockSpec((tm, d), row), _full((1, d)), _full(w.shape),
                  _full((1, width)), _full((1, width)), _full((width, width))],
        out_specs=[blk, blk, blk, kvblk, kvblk, blk, blk],
        out_shape=[b16o, f32o, b16o, kvo, kvo, b16o, b16o],
        compiler_params=_cparams("parallel"),
        name="inproj_even",
    )(x, ln.reshape(1, d), w, gq_t, gk_t, seg)


def _lru_coeffs(xc, wx_ref, bx_ref, wa_ref, ba_ref, lam_ref):
    xb = xc.astype(BF16)
    gate_x = jax.nn.sigmoid(jnp.dot(xb, wx_ref[...], preferred_element_type=F32) + bx_ref[...])
    gate_a = jax.nn.sigmoid(jnp.dot(xb, wa_ref[...], preferred_element_type=F32) + ba_ref[...])
    log_a = -RG_C * gate_a * _softplus(-lam_ref[...])
    a = jnp.exp(log_a)
    th = jnp.tanh(log_a)
    one_minus_a2 = -2.0 * th / (1.0 - th)
    b = jnp.sqrt(one_minus_a2) * (gate_x * xc)
    return a, b


def _lru_prompt_body(xr_ref, gg_ref, cw_ref, cb_ref, wx_ref, bx_ref, wa_ref, ba_ref, lam_ref,
                     ya_ref, hlast_ref, prev_s, h_s, a_s, b_s, *, tt, nb):
    t = pl.program_id(1)

    @pl.when(t == 0)
    def _():
        prev_s[...] = jnp.zeros_like(prev_s)
        h_s[...] = jnp.zeros_like(h_s)

    for b in range(nb):
        xr = xr_ref[b]
        x1, x2, x3 = _shifted_rows(prev_s[b], xr)
        xc = (x3 * cw_ref[0:1, :] + x2 * cw_ref[1:2, :] + x1 * cw_ref[2:3, :] + xr * cw_ref[3:4, :]
              + cb_ref[...])
        prev_s[b] = xr[tt - SUBLANES:, :]
        a, bb = _lru_coeffs(xc, wx_ref, bx_ref, wa_ref, ba_ref, lam_ref)
        a_s[b] = a
        b_s[b] = bb

    def step(i, hs):
        out = []
        for b in range(nb):
            h = a_s[b, pl.ds(i, 1), :] * hs[b] + b_s[b, pl.ds(i, 1), :]
            b_s[b, pl.ds(i, 1), :] = h
            out.append(h)
        return tuple(out)

    hs = lax.fori_loop(0, tt, step, tuple(h_s[b] for b in range(nb)), unroll=8)
    for b in range(nb):
        h_s[b] = hs[b]
        ya_ref[b] = (gg_ref[b].astype(F32) * b_s[b]).astype(BF16)
        hlast_ref[b] = hs[b]


def _lru_prompt(xr, gg, cw, cb, wx, bx, wa, ba, lam, *, tt):
    b, t, w = xr.shape
    nb = LRU_SEQS_PER_STEP if b % LRU_SEQS_PER_STEP == 0 else 1
    seq = lambda i, j: (i, j, 0)
    vec = _full((1, w))
    return pl.pallas_call(
        functools.partial(_lru_prompt_body, tt=tt, nb=nb),
        grid=(b // nb, t // tt),
        in_specs=[pl.BlockSpec((nb, tt, w), seq), pl.BlockSpec((nb, tt, w), seq), _full((CONV_W, w)), vec,
                  _full((w, w)), vec, _full((w, w)), vec, vec],
        out_specs=[pl.BlockSpec((nb, tt, w), seq), pl.BlockSpec((nb, 1, w), lambda i, j: (i, 0, 0))],
        out_shape=[jax.ShapeDtypeStruct((b, t, w), BF16), jax.ShapeDtypeStruct((b, 1, w), F32)],
        scratch_shapes=[pltpu.VMEM((nb, SUBLANES, w), F32), pltpu.VMEM((nb, 1, w), F32),
                        pltpu.VMEM((nb, tt, w), F32), pltpu.VMEM((nb, tt, w), F32)],
        compiler_params=_cparams("parallel", "arbitrary"),
        name="lru_prompt",
    )(xr, gg, cw, cb, wx, bx, wa, ba, lam)


def _lru_step_body(xr_ref, gg_ref, b0_ref, b1_ref, b2_ref, h0_ref, cw_ref, cb_ref,
                   wx_ref, bx_ref, wa_ref, ba_ref, lam_ref, ya_ref, h_ref):
    xr = xr_ref[...]
    xc = (b0_ref[...] * cw_ref[0:1, :] + b1_ref[...] * cw_ref[1:2, :] + b2_ref[...] * cw_ref[2:3, :]
          + xr * cw_ref[3:4, :] + cb_ref[...])
    a, b = _lru_coeffs(xc, wx_ref, bx_ref, wa_ref, ba_ref, lam_ref)
    h = a * h0_ref[...] + b
    h_ref[...] = h
    ya_ref[...] = (gg_ref[...].astype(F32) * h).astype(BF16)


def _lru_step(xr, gg, buf, h0, cw, cb, wx, bx, wa, ba, lam):
    n, w = xr.shape
    args = (xr, gg, buf[:, 0], buf[:, 1], buf[:, 2], h0, cw, cb, wx, bx, wa, ba, lam)
    return pl.pallas_call(
        _lru_step_body,
        grid=(1,),
        in_specs=[_full(a.shape) for a in args],
        out_specs=[_full((n, w)), _full((n, w))],
        out_shape=[jax.ShapeDtypeStruct((n, w), BF16), jax.ShapeDtypeStruct((n, w), F32)],
        compiler_params=_cparams("arbitrary"),
        name="lru_step",
    )(*args)


def _diff_lambda(lq1_ref, lk1_ref, lq2_ref, lk2_ref, lam_init):
    s1 = jnp.sum(lq1_ref[...] * lk1_ref[...], axis=-1, keepdims=True)
    s2 = jnp.sum(lq2_ref[...] * lk2_ref[...], axis=-1, keepdims=True)
    return jnp.exp(s1) - jnp.exp(s2) + lam_init


def _attn_prompt_body(lq1_ref, lk1_ref, lq2_ref, lk2_ref, sg_ref, gq_ref, gk_ref, q_ref, k_ref, v_ref, o_ref,
                      *, tq, dh, nh, lam_init):
    i = pl.program_id(2)
    hw = 2 * dh
    heads = range(nh)
    lam = _diff_lambda(lq1_ref, lk1_ref, lq2_ref, lk2_ref, lam_init)
    lane = lax.broadcasted_iota(jnp.int32, (tq, hw), 1)
    qq = []
    for h in heads:
        q = q_ref[0, :, h * hw:(h + 1) * hw] * (dh ** -0.5)
        zero = jnp.zeros_like(q)
        qq.append(jnp.concatenate([jnp.where(lane < dh, q, zero), jnp.where(lane >= dh, q, zero)], axis=0))
    rowi = lax.broadcasted_iota(jnp.int32, (2 * tq, tq), 0)
    coli = lax.broadcasted_iota(jnp.int32, (2 * tq, tq), 1)
    causal = coli <= jnp.where(rowi >= tq, rowi - tq, rowi)

    def scores(j, masked):
        start = pl.multiple_of(j * tq, tq)
        kj = k_ref[0, pl.ds(start, tq), :]
        vj = v_ref[0, pl.ds(start, tq), :]
        s = [_bdot_nt(qq[h], kj[:, h * hw:(h + 1) * hw]) for h in heads]
        if masked:
            s = [jnp.where(causal, s[h], NEG_INF) for h in heads]
        return s, [vj[:, h * hw:(h + 1) * hw] for h in heads]

    def finish(l, acc):
        out = []
        for h in heads:
            o = acc[h] / l[h]
            od = o[:tq] - lam * o[tq:]
            out.append(_rms(od, sg_ref[...]) * (1.0 - lam_init))
        o_ref[0] = jnp.concatenate(out, axis=1).astype(BF16)

    bound = (dh ** 0.5) * 1.05 * jnp.max(jnp.abs(gq_ref[...])) * jnp.max(jnp.abs(gk_ref[...]))
    fixed_ok = bound <= ATTN_FIXED_SHIFT_MAX

    @pl.when(fixed_ok)
    def _():
        def block(j, carry, masked):
            s, vj = scores(j, masked)
            p = [jnp.exp(s[h] - bound) for h in heads]
            l = [carry[h][0] + sum(p[h][:, c0:c0 + LANES] for c0 in range(0, tq, LANES)) for h in heads]
            acc = [carry[h][1] + _bdot(p[h], vj[h]) for h in heads]
            return tuple((l[h], acc[h]) for h in heads)

        init = tuple((jnp.zeros((2 * tq, LANES), F32), jnp.zeros((2 * tq, hw), F32)) for _ in heads)
        carry = lax.fori_loop(0, i, lambda j, c: block(j, c, False), init)
        carry = block(i, carry, True)
        finish([jnp.sum(carry[h][0], axis=-1, keepdims=True) for h in heads], [carry[h][1] for h in heads])

    @pl.when(jnp.logical_not(fixed_ok))
    def _():
        def block(j, carry, masked):
            s, vj = scores(j, masked)
            m_new = [jnp.maximum(carry[h][0], jnp.max(s[h], axis=-1, keepdims=True)) for h in heads]
            alpha = [jnp.exp(carry[h][0] - m_new[h]) for h in heads]
            p = [jnp.exp(s[h] - m_new[h]) for h in heads]
            l = [alpha[h] * carry[h][1] + jnp.sum(p[h], axis=-1, keepdims=True) for h in heads]
            acc = [alpha[h] * carry[h][2] + _bdot(p[h], vj[h]) for h in heads]
            return tuple((m_new[h], l[h], acc[h]) for h in heads)

        init = tuple((jnp.full((2 * tq, 1), NEG_INF, F32), jnp.zeros((2 * tq, 1), F32),
                      jnp.zeros((2 * tq, hw), F32)) for _ in heads)
        carry = lax.fori_loop(0, i, lambda j, c: block(j, c, False), init)
        carry = block(i, carry, True)
        finish([carry[h][1] for h in heads], [carry[h][2] for h in heads])


def _attn_prompt(q, k, v, lq1, lk1, lq2, lk2, sub_g, gq, gk, *, lam_init, tq):
    b, t, w = q.shape
    hw = w // H_B
    dh = hw // 2
    vec = _full((1, dh))
    nh = ATTN_HEADS_PER_STEP
    assert tq % LANES == 0
    qblk = pl.BlockSpec((1, tq, nh * hw), lambda bi, hg, i: (bi, i, hg))
    kvblk = pl.BlockSpec((1, t, nh * hw), lambda bi, hg, i: (bi, 0, hg))
    return pl.pallas_call(
        functools.partial(_attn_prompt_body, tq=tq, dh=dh, nh=nh, lam_init=lam_init),
        grid=(b, H_B // nh, t // tq),
        in_specs=[vec, vec, vec, vec, _full((1, hw)), vec, vec, qblk, kvblk, kvblk],
        out_specs=qblk,
        out_shape=jax.ShapeDtypeStruct((b, t, w), BF16),
        compiler_params=_cparams("parallel", "parallel", "arbitrary"),
        name="attn_prompt",
    )(lq1.reshape(1, dh), lk1.reshape(1, dh), lq2.reshape(1, dh), lk2.reshape(1, dh),
      sub_g.reshape(1, hw), gq.reshape(1, dh), gk.reshape(1, dh), q, k, v)


def _attn_decode_body(pt_ref, lq1_ref, lk1_ref, lq2_ref, lk2_ref, sg_ref, q_ref, kn_ref, vn_ref, *refs,
                      n_pages, dh, lam_init):
    del pt_ref
    k_refs = refs[:n_pages]
    v_refs = refs[n_pages:2 * n_pages]
    o_ref = refs[2 * n_pages]
    hw = 2 * dh
    rows = 2 * H_B
    lam = _diff_lambda(lq1_ref, lk1_ref, lq2_ref, lk2_ref, lam_init)

    row = lax.broadcasted_iota(jnp.int32, (rows, hw), 0)
    lane = lax.broadcasted_iota(jnp.int32, (rows, hw), 1)

    def by_head(vec_ref):
        out = jnp.zeros((rows, hw), F32)
        for h in range(H_B):
            piece = vec_ref[0, :, h * hw:(h + 1) * hw].astype(F32)
            out = jnp.where(row // 2 == h, jnp.broadcast_to(piece, (rows, hw)), out)
        return out

    half = (lane >= dh) == (row % 2 == 1)
    wq = jnp.where(half, by_head(q_ref), 0.0) * (dh ** -0.5)
    kn = by_head(kn_ref)
    vn = by_head(vn_ref)
    s_new = jnp.sum(wq * kn, axis=-1, keepdims=True)

    n_rows = k_refs[0].shape[1]
    col_head = lax.broadcasted_iota(jnp.int32, (rows, n_rows), 1) % H_B
    own = col_head == lax.broadcasted_iota(jnp.int32, (rows, n_rows), 0) // 2
    wq_b = wq.astype(BF16)
    scores = [jnp.where(own, _bdot_nt(wq_b, k_refs[p][0]), NEG_INF) for p in range(n_pages)]
    m = s_new
    for s in scores:
        m = jnp.maximum(m, jnp.max(s, axis=-1, keepdims=True))
    p_new = jnp.exp(s_new - m)
    l = p_new
    acc = p_new * vn
    for p in range(n_pages):
        pr = jnp.exp(scores[p] - m)
        l = l + jnp.sum(pr, axis=-1, keepdims=True)
        acc = acc + _bdot(pr, v_refs[p][0])
    o = acc / l
    r_out = lax.broadcasted_iota(jnp.int32, (SUBLANES, rows), 0)
    r_in = lax.broadcasted_iota(jnp.int32, (SUBLANES, rows), 1)
    sel = (jnp.where(r_in == 2 * r_out, 1.0, 0.0) - lam * jnp.where(r_in == 2 * r_out + 1, 1.0, 0.0))
    od = jnp.dot(sel, o, preferred_element_type=F32, precision=HIGHEST)[:H_B]
    o_ref[0] = _rms(od, sg_ref[...]) * (1.0 - lam_init)


def _attn_decode(q, k_new, v_new, cache_k, cache_v, page_table, lq1, lk1, lq2, lk2, sub_g, *, lam_init):
    n, w = q.shape
    hw = w // H_B
    dh = hw // 2
    n_pages = page_table.shape[1]
    n_pool, page = cache_k.shape[0], cache_k.shape[1]
    ck = cache_k.reshape(n_pool, page * H_B, hw)
    cv = cache_v.reshape(n_pool, page * H_B, hw)
    vec = pl.BlockSpec((1, dh), lambda i, pt: (0, 0))
    tok = pl.BlockSpec((1, 1, w), lambda i, pt: (i, 0, 0))

    def page_spec(p):
        return pl.BlockSpec((1, page * H_B, hw), lambda i, pt: (pt[i, p], 0, 0))

    pages = [page_spec(p) for p in range(n_pages)]
    out = pl.pallas_call(
        functools.partial(_attn_decode_body, n_pages=n_pages, dh=dh, lam_init=lam_init),
        grid_spec=pltpu.PrefetchScalarGridSpec(
            num_scalar_prefetch=1,
            grid=(n,),
            in_specs=[vec, vec, vec, vec, pl.BlockSpec((1, hw), lambda i, pt: (0, 0)), tok, tok, tok]
            + pages + pages,
            out_specs=pl.BlockSpec((1, H_B, hw), lambda i, pt: (i, 0, 0)),
        ),
        out_shape=jax.ShapeDtypeStruct((n, H_B, hw), F32),
        compiler_params=_cparams("arbitrary"),
        name="attn_decode",
    )(page_table, lq1.reshape(1, dh), lk1.reshape(1, dh), lq2.reshape(1, dh), lk2.reshape(1, dh),
      sub_g.reshape(1, hw), q.reshape(n, 1, w), k_new.reshape(n, 1, w), v_new.reshape(n, 1, w),
      *([ck] * n_pages), *([cv] * n_pages))
    return out.reshape(n, w)


def _resident(shape):
    nd = len(shape)
    return pl.BlockSpec(shape, lambda *_: (0,) * nd, pipeline_mode=pl.Buffered(1))


def _mixer_residual(x, pair_refs):
    y = None
    for a_ref, w_ref in zip(pair_refs[0::2], pair_refs[1::2]):
        part = jnp.dot(a_ref[...].astype(BF16), w_ref[...], preferred_element_type=F32)
        y = part if y is None else y + part
    return x if y is None else x + y


def _ffn_body(*refs, chunks, n_pairs):
    x_ref, ln_ref = refs[:2]
    wg_ref, wu_ref, wd_ref, o_ref = refs[2 + 2 * n_pairs:]
    x = _mixer_residual(x_ref[...], refs[2:2 + 2 * n_pairs])
    hb = _rms(x, ln_ref[...]).astype(BF16)
    acc = x
    for lo, n in chunks:
        g = jnp.dot(hb, wg_ref[:, lo:lo + n], preferred_element_type=F32)
        u = jnp.dot(hb, wu_ref[:, lo:lo + n], preferred_element_type=F32)
        acc = acc + jnp.dot((_silu(g) * u).astype(BF16), wd_ref[lo:lo + n, :], preferred_element_type=F32)
    o_ref[...] = acc


def _ffn(x, pairs, ln, wg, wu, wd, *, tm):
    m, d = x.shape
    f = wg.shape[1]
    chunks = [(lo, min(FFN_CHUNK, f - lo)) for lo in range(0, f, FFN_CHUNK)]
    row = lambda i: (i, 0)
    pair_specs, pair_args = [], []
    for a, w in pairs:
        pair_specs += [pl.BlockSpec((tm, a.shape[1]), row), _full(w.shape)]
        pair_args += [a, w]
    return pl.pallas_call(
        functools.partial(_ffn_body, chunks=chunks, n_pairs=len(pairs)),
        grid=(m // tm,),
        in_specs=[pl.BlockSpec((tm, d), row), _full((1, d))] + pair_specs
        + [_resident(wg.shape), _resident(wu.shape), _resident(wd.shape)],
        out_specs=pl.BlockSpec((tm, d), row),
        out_shape=jax.ShapeDtypeStruct((m, d), F32),
        compiler_params=_cparams("parallel"),
        name="ffn",
    )(x, ln.reshape(1, d), *pair_args, wg, wu, wd)


def _inproj_odd_body(x_ref, ln_ref, w_ref, wba_ref, alog_ref, dtb_ref, qkv_ref, z_ref, gb_ref,
                     *, n_qkv, n_z, tn):
    hb = _rms(x_ref[...], ln_ref[...]).astype(BF16)
    for c in range(n_qkv // tn):
        qkv_ref[:, c * tn:(c + 1) * tn] = jnp.dot(hb, w_ref[:, c * tn:(c + 1) * tn],
                                                  preferred_element_type=F32)
    for c in range(n_z // tn):
        z_ref[:, c * tn:(c + 1) * tn] = jnp.dot(
            hb, w_ref[:, n_qkv + c * tn:n_qkv + (c + 1) * tn], preferred_element_type=F32).astype(BF16)
    ba = jnp.dot(hb, wba_ref[...], preferred_element_type=F32)
    lane = lax.broadcasted_iota(jnp.int32, ba.shape, 1)
    beta = jax.nn.sigmoid(ba)
    g = -jnp.exp(alog_ref[...]) * _softplus(ba + dtb_ref[...])
    gb_ref[...] = jnp.where(lane < H_C, beta, g)


def _inproj_odd(x, ln, w, a_log, dt_bias, *, tm):
    m, d = x.shape
    n_z = H_C * (d // H_C)
    n_qkv = w.shape[1] - n_z - 2 * H_C
    wba = jnp.zeros((d, LANES), BF16).at[:, :2 * H_C].set(w[:, n_qkv + n_z:])
    alog = jnp.zeros((1, LANES), F32).at[0, H_C:2 * H_C].set(a_log)
    dtb = jnp.zeros((1, LANES), F32).at[0, H_C:2 * H_C].set(dt_bias)
    row = lambda i: (i, 0)
    return pl.pallas_call(
        functools.partial(_inproj_odd_body, n_qkv=n_qkv, n_z=n_z, tn=512),
        grid=(m // tm,),
        in_specs=[pl.BlockSpec((tm, d), row), _full((1, d)), _full(w.shape), _full(wba.shape),
                  _full((1, LANES)), _full((1, LANES))],
        out_specs=[pl.BlockSpec((tm, n_qkv), row), pl.BlockSpec((tm, n_z), row),
                   pl.BlockSpec((tm, LANES), row)],
        out_shape=[jax.ShapeDtypeStruct((m, n_qkv), F32), jax.ShapeDtypeStruct((m, n_z), BF16),
                   jax.ShapeDtypeStruct((m, LANES), F32)],
        compiler_params=_cparams("parallel"),
        name="inproj_odd",
    )(x, ln.reshape(1, d), w, wba, alog, dtb)


def _l2n(x):
    return x * lax.rsqrt(jnp.sum(x * x, axis=-1, keepdims=True) + EPS)


def _delta_prompt_body(qkv_ref, z_ref, gb_ref, cw_ref, og_ref, o_ref, s_ref, prev_s, xc_s, *, tc, dk):
    t = pl.program_id(1)
    c = DELTA_CHUNK
    nk = H_C * dk

    @pl.when(t == 0)
    def _():
        prev_s[...] = jnp.zeros_like(prev_s)
        s_ref[...] = jnp.zeros_like(s_ref)

    x = qkv_ref[0]
    x1, x2, x3 = _shifted_rows(prev_s[...], x)
    xc_s[...] = _silu(x3 * cw_ref[0:1, :] + x2 * cw_ref[1:2, :] + x1 * cw_ref[2:3, :] + x * cw_ref[3:4, :])
    prev_s[...] = x[tc - SUBLANES:, :]

    ri = lax.broadcasted_iota(jnp.int32, (c, c), 0)
    ci = lax.broadcasted_iota(jnp.int32, (c, c), 1)
    incl = ri >= ci
    strict = ri > ci
    tri = jnp.where(incl, 1.0, 0.0)
    eye = jnp.where(ri == ci, 1.0, 0.0)
    heads = range(H_C)
    n_chunks = tc // c

    q, k, v, beta, e_g, decay, g_end, k_dec = ([] for _ in range(8))
    for ic in range(n_chunks):
        rows = slice(ic * c, (ic + 1) * c)
        gbc = gb_ref[0, rows, :]
        g_all = jnp.dot(tri, gbc, preferred_element_type=F32, precision=HIGHEST)
        g_t = g_all.T
        for h in heads:
            g_col = g_all[:, H_C + h:H_C + h + 1]
            g_row = g_t[H_C + h:H_C + h + 1, :]
            g_last = g_all[c - 1:c, H_C + h:H_C + h + 1]
            kn = _l2n(xc_s[rows, nk + h * dk:nk + (h + 1) * dk])
            q.append(_l2n(xc_s[rows, h * dk:(h + 1) * dk]) * (dk ** -0.5))
            k.append(kn)
            v.append(xc_s[rows, 2 * nk + h * dk:2 * nk + (h + 1) * dk])
            beta.append(gbc[:, h:h + 1])
            e_g.append(jnp.exp(g_col))
            decay.append(jnp.where(incl, jnp.exp(jnp.where(incl, g_col - g_row, 0.0)), 0.0))
            g_end.append(jnp.exp(g_last))
            k_dec.append(kn * jnp.exp(g_last - g_col))
    pairs = range(n_chunks * H_C)
    kb = [k[i] * beta[i] for i in pairs]
    kk = [_bdot_nt(jnp.concatenate([kb[i], q[i]], axis=0), k[i]) for i in pairs]
    qk = [kk[i][c:] * decay[i] for i in pairs]
    pw = [-jnp.where(strict, kk[i][:c] * decay[i], 0.0) for i in pairs]
    inv = [eye + pw[i] for i in pairs]
    for _ in range(int(math.log2(c)) - 1):
        pw = [_bdot(pw[i], pw[i]) for i in pairs]
        inv = [inv[i] + _bdot(inv[i], pw[i]) for i in pairs]
    sol = [_bdot(inv[i], jnp.concatenate([v[i] * beta[i], kb[i] * e_g[i]], axis=1)) for i in pairs]
    wq = [jnp.concatenate([sol[i][:, dk:], q[i] * e_g[i]], axis=0).astype(BF16) for i in pairs]

    s = [s_ref[0, h] for h in heads]
    for ic in range(n_chunks):
        rows = slice(ic * c, (ic + 1) * c)
        idx = [ic * H_C + h for h in heads]
        ws = [_bdot(wq[i], s[h]) for h, i in zip(heads, idx)]
        v_new = [sol[i][:, :dk] - ws[h][:c] for h, i in zip(heads, idx)]
        o = [ws[h][c:] + _bdot(qk[i], v_new[h]) for h, i in zip(heads, idx)]
        s = [s[h] * g_end[i] + _bdot_tn(k_dec[i], v_new[h]) for h, i in zip(heads, idx)]
        zt = z_ref[0, rows, :].astype(F32)
        o_ref[0, rows, :] = jnp.concatenate(
            [_rms(o[h], og_ref[...]) * _silu(zt[:, h * dk:(h + 1) * dk]) for h in heads], axis=1).astype(BF16)
    for h in heads:
        s_ref[0, h] = s[h]


def _delta_prompt(qkv, z, gb, cw, og, *, tc):
    b, t, nq = qkv.shape
    nz = z.shape[2]
    dk = nz // H_C
    seq = lambda i, j: (i, j, 0)
    return pl.pallas_call(
        functools.partial(_delta_prompt_body, tc=tc, dk=dk),
        grid=(b, t // tc),
        in_specs=[pl.BlockSpec((1, tc, nq), seq), pl.BlockSpec((1, tc, nz), seq),
                  pl.BlockSpec((1, tc, LANES), seq), _full((CONV_W, nq)), _full((1, dk))],
        out_specs=[pl.BlockSpec((1, tc, nz), seq),
                   pl.BlockSpec((1, H_C, dk, dk), lambda i, j: (i, 0, 0, 0))],
        out_shape=[jax.ShapeDtypeStruct((b, t, nz), BF16), jax.ShapeDtypeStruct((b, H_C, dk, dk), F32)],
        scratch_shapes=[pltpu.VMEM((SUBLANES, nq), F32), pltpu.VMEM((tc, nq), F32)],
        compiler_params=_cparams("parallel", "arbitrary"),
        name="delta_prompt",
    )(qkv, z, gb, cw, og.reshape(1, dk))


def _delta_prep_step_body(x_ref, b0_ref, b1_ref, b2_ref, cw_ref, q_ref, k_ref, v_ref, *, dk):
    xc = _silu(b0_ref[...] * cw_ref[0:1, :] + b1_ref[...] * cw_ref[1:2, :] + b2_ref[...] * cw_ref[2:3, :]
               + x_ref[...] * cw_ref[3:4, :])
    nk = H_C * dk
    for h in range(H_C):
        sl = slice(h * dk, (h + 1) * dk)
        q_ref[:, sl] = _l2n(xc[:, h * dk:(h + 1) * dk]) * (dk ** -0.5)
        k_ref[:, sl] = _l2n(xc[:, nk + h * dk:nk + (h + 1) * dk])
    v_ref[...] = xc[:, 2 * nk:]


def _delta_step_body(q_ref, k_ref, v_ref, gb_ref, z_ref, og_ref, s_ref, o_ref, sn_ref, *, dk, ns):
    row = lax.broadcasted_iota(jnp.int32, (SUBLANES, dk), 0)
    units = [(j, h) for j in range(ns) for h in range(H_C)]
    rows8 = lambda x: jnp.broadcast_to(x, (SUBLANES, dk))
    q = [q_ref[j, :, h * dk:(h + 1) * dk] for j, h in units]
    k = [k_ref[j, :, h * dk:(h + 1) * dk] for j, h in units]
    v = [v_ref[j, :, h * dk:(h + 1) * dk] for j, h in units]
    beta = [gb_ref[j, :, h:h + 1] for j, h in units]
    e_g = [jnp.exp(gb_ref[j, :, H_C + h:H_C + h + 1]) for j, h in units]
    s = [s_ref[j, h] for j, h in units]
    n = range(len(units))
    kq = [jnp.where(row == 0, rows8(k[u]), jnp.where(row == 1, rows8(q[u] * e_g[u]), 0.0)) for u in n]
    ks = [_bdot(kq[u], s[u]) for u in n]
    v_new = [beta[u] * (v[u] - e_g[u] * ks[u][0:1]) for u in n]
    o = [ks[u][1:2] + jnp.sum(q[u] * k[u], axis=-1, keepdims=True) * v_new[u] for u in n]
    k_hi, k_lo = zip(*[_split_bf16(jnp.where(row == 0, rows8(k[u]), 0.0)) for u in n])
    v_hi, v_lo = zip(*[_split_bf16(jnp.where(row == 0, rows8(v_new[u]), 0.0)) for u in n])
    outer = [_bdot_tn(k_hi[u], v_hi[u]) + _bdot_tn(k_hi[u], v_lo[u]) + _bdot_tn(k_lo[u], v_hi[u]) for u in n]
    for u, (j, h) in enumerate(units):
        sn_ref[j, h] = s[u] * e_g[u] + outer[u]
    for j in range(ns):
        zt = z_ref[j].astype(F32)
        o_ref[j] = jnp.concatenate(
            [_rms(o[j * H_C + h], og_ref[...]) * _silu(zt[:, h * dk:(h + 1) * dk]) for h in range(H_C)],
            axis=1).astype(BF16)


def _delta_step(qkv, buf, z, gb, s0, cw, og):
    n, nq = qkv.shape
    nz = z.shape[1]
    dk = nz // H_C
    args = (qkv, buf[:, 0], buf[:, 1], buf[:, 2], cw)
    q, k, v = pl.pallas_call(
        functools.partial(_delta_prep_step_body, dk=dk),
        grid=(1,),
        in_specs=[_full(a.shape) for a in args],
        out_specs=[_full((n, nz))] * 3,
        out_shape=[jax.ShapeDtypeStruct((n, nz), F32)] * 3,
        compiler_params=_cparams("arbitrary"),
        name="delta_prep_step",
    )(*args)
    ns = DELTA_STEP_SEQS if n % DELTA_STEP_SEQS == 0 else 1
    tok = lambda width: pl.BlockSpec((ns, 1, width), lambda i: (i, 0, 0))
    st = pl.BlockSpec((ns, H_C, dk, dk), lambda i: (i, 0, 0, 0))
    o, s_new = pl.pallas_call(
        functools.partial(_delta_step_body, dk=dk, ns=ns),
        grid=(n // ns,),
        in_specs=[tok(nz), tok(nz), tok(nz), tok(LANES), tok(nz), _full((1, dk)), st],
        out_specs=[tok(nz), st],
        out_shape=[jax.ShapeDtypeStruct((n, 1, nz), BF16), jax.ShapeDtypeStruct(s0.shape, F32)],
        compiler_params=_cparams("parallel"),
        name="delta_step",
    )(q.reshape(n, 1, nz), k.reshape(n, 1, nz), v.reshape(n, 1, nz), gb.reshape(n, 1, LANES),
      z.reshape(n, 1, nz), og.reshape(1, dk), s0)
    return o.reshape(n, nz), s_new


ROUTER_ROWS = 1024
SCATTER_ROWS = 512


def _router_body(x_ref, a_ref, wo_ref, ln_ref, rw_ref, xo_ref, hb_ref, comb_ref, rank_ref, rank_t_ref, cnt_ref,
                 *, n_exp, tm):
    tb = min(ROUTER_ROWS, tm)
    lane = lax.broadcasted_iota(jnp.int32, (tb, LANES), 1).astype(F32)
    ri = lax.broadcasted_iota(jnp.int32, (tb, tb), 0)
    ci = lax.broadcasted_iota(jnp.int32, (tb, tb), 1)
    before = jnp.where(ri > ci, 1.0, 0.0).astype(BF16)
    running = jnp.zeros((1, LANES), F32)
    for t0 in range(0, tm, tb):
        rows = slice(t0, t0 + tb)
        x = x_ref[rows, :] + jnp.dot(a_ref[rows, :], wo_ref[...], preferred_element_type=F32)
        xo_ref[rows, :] = x
        h = _rms(x, ln_ref[...])
        hb_ref[rows, :] = h.astype(BF16)
        h_hi, h_lo = _split_bf16(h)
        w_hi, w_lo = _split_bf16(rw_ref[...])
        logits = _bdot(h_hi, w_hi) + _bdot(h_lo, w_hi) + _bdot(h_hi, w_lo)
        logits = jnp.where(lane < n_exp, logits, -jnp.inf)
        m1 = jnp.max(logits, axis=-1, keepdims=True)
        i1 = jnp.min(jnp.where(logits == m1, lane, float(LANES)), axis=-1, keepdims=True)
        rest = jnp.where(lane == i1, -jnp.inf, logits)
        m2 = jnp.max(rest, axis=-1, keepdims=True)
        i2 = jnp.min(jnp.where(rest == m2, lane, float(LANES)), axis=-1, keepdims=True)
        e = jnp.exp(m2 - m1)
        g1 = 1.0 / (1.0 + e)
        g2 = e / (1.0 + e)
        comb = jnp.where(lane == i1, g1, 0.0) + jnp.where(lane == i2, g2, 0.0)
        comb_ref[rows, :] = comb
        sel = comb != 0.0
        self_f = jnp.where(sel, 1.0, 0.0)
        local = jnp.dot(before, self_f.astype(BF16), preferred_element_type=F32)
        rank = jnp.where(sel, local + running, -1.0)
        rank_ref[rows, :] = rank
        rank_t_ref[:, rows] = rank.T[:SUBLANES, :]
        running = running + jnp.sum(self_f, axis=0, keepdims=True)
    cnt_ref[0] = running.astype(jnp.int32)


def _router(x, a, wo, ln, rw, *, tm):
    m, d = x.shape
    n_exp = rw.shape[1]
    assert n_exp <= SUBLANES
    rwp = jnp.zeros((d, LANES), F32).at[:, :n_exp].set(rw)
    row = lambda i: (i, 0)
    return pl.pallas_call(
        functools.partial(_router_body, n_exp=n_exp, tm=tm),
        grid=(m // tm,),
        in_specs=[pl.BlockSpec((tm, d), row), pl.BlockSpec((tm, a.shape[1]), row), _full(wo.shape),
                  _full((1, d)), _full((d, LANES))],
        out_specs=[pl.BlockSpec((tm, d), row), pl.BlockSpec((tm, d), row), pl.BlockSpec((tm, LANES), row),
                   pl.BlockSpec((tm, LANES), row), pl.BlockSpec((SUBLANES, tm), lambda i: (0, i)),
                   pl.BlockSpec((1, 1, LANES), lambda i: (i, 0, 0))],
        out_shape=[jax.ShapeDtypeStruct((m, d), F32), jax.ShapeDtypeStruct((m, d), BF16),
                   jax.ShapeDtypeStruct((m, LANES), F32), jax.ShapeDtypeStruct((m, LANES), F32),
                   jax.ShapeDtypeStruct((SUBLANES, m), F32),
                   jax.ShapeDtypeStruct((m // tm, 1, LANES), jnp.int32)],
        compiler_params=_cparams("parallel"),
        name="router",
    )(x, a, wo, ln.reshape(1, d), rwp)


def _moe_body(cnt_ref, x_hbm, hb_ref, comb_ref, rank_ref, rank_t_ref, wg_ref, wu_ref, wd_ref, o_ref,
              xs_s, ys_s, x_sem, *, tm):
    i, e, f = pl.program_id(0), pl.program_id(1), pl.program_id(2)
    n_f = pl.num_programs(2)
    count = cnt_ref[i, e]
    r = MOE_ROWS

    @pl.when(jnp.logical_and(e == 0, f == 0))
    def _():
        rows = pl.ds(pl.multiple_of(i * tm, tm), tm)
        copy = pltpu.make_async_copy(x_hbm.at[rows, :], o_ref, x_sem)
        copy.start()
        copy.wait()

    def group(lo, n):
        rows = pl.ds(lo, n)

        @pl.when(f == 0)
        def _():
            want = (lax.broadcasted_iota(jnp.int32, (n, tm), 0) + lo).astype(F32)
            onehot = jnp.where(rank_t_ref[pl.ds(e, 1), :] == want, 1.0, 0.0).astype(BF16)
            xs_s[rows, :] = jnp.dot(onehot, hb_ref[...], preferred_element_type=F32).astype(BF16)

        xs = xs_s[rows, :]
        g = jnp.dot(xs, wg_ref[0], preferred_element_type=F32)
        u = jnp.dot(xs, wu_ref[0], preferred_element_type=F32)
        y = jnp.dot((_silu(g) * u).astype(BF16), wd_ref[0], preferred_element_type=F32)

        @pl.when(f == 0)
        def _():
            ys_s[rows, :] = y

        @pl.when(f > 0)
        def _():
            ys_s[rows, :] += y

        @pl.when(f == n_f - 1)
        def _():
            tb = min(SCATTER_ROWS, tm)
            pick = lax.broadcasted_iota(jnp.int32, (tb, LANES), 1) == e
            want = (lax.broadcasted_iota(jnp.int32, (tb, n), 1) + lo).astype(F32)
            yb = ys_s[rows, :].astype(BF16)
            for t0 in range(0, tm, tb):
                rank_col = jnp.sum(jnp.where(pick, rank_ref[t0:t0 + tb, :], 0.0), axis=-1, keepdims=True)
                gate_col = jnp.sum(jnp.where(pick, comb_ref[t0:t0 + tb, :], 0.0), axis=-1, keepdims=True)
                onehot = jnp.where(rank_col == want, 1.0, 0.0).astype(BF16)
                o_ref[t0:t0 + tb, :] += gate_col * jnp.dot(onehot, yb, preferred_element_type=F32)

    sizes = [n for n in MOE_HEAD_SIZES if n <= tm]
    head = sizes[-1] if sizes else 0
    head_count = jnp.minimum(count, head)
    for below, n in zip([0] + sizes[:-1], sizes):
        @pl.when(jnp.logical_and(head_count > below, head_count <= n))
        def _(n=n):
            group(0, n)

    def rest(j, carry):
        group(pl.multiple_of(head + j * r, r), r)
        return carry

    lax.fori_loop(0, (jnp.maximum(count - head, 0) + r - 1) // r, rest, 0)


def _moe(x, hb, comb, rank, rank_t, cnt, wg, wu, wd, *, tm, tf):
    m, d = x.shape
    n_exp, _, f = wg.shape
    assert tm % MOE_ROWS == 0
    tok = lambda width: pl.BlockSpec((tm, width), lambda i, e, j, cnt: (i, 0))
    return pl.pallas_call(
        functools.partial(_moe_body, tm=tm),
        grid_spec=pltpu.PrefetchScalarGridSpec(
            num_scalar_prefetch=1,
            grid=(m // tm, n_exp, f // tf),
            in_specs=[pl.BlockSpec(memory_space=pl.ANY), tok(d), tok(LANES), tok(LANES),
                      pl.BlockSpec((SUBLANES, tm), lambda i, e, j, cnt: (0, i)),
                      pl.BlockSpec((1, d, tf), lambda i, e, j, cnt: (e, 0, j)),
                      pl.BlockSpec((1, d, tf), lambda i, e, j, cnt: (e, 0, j)),
                      pl.BlockSpec((1, tf, d), lambda i, e, j, cnt: (e, j, 0))],
            out_specs=tok(d),
            scratch_shapes=[pltpu.VMEM((tm, d), BF16), pltpu.VMEM((tm, d), F32), pltpu.SemaphoreType.DMA(())],
        ),
        out_shape=jax.ShapeDtypeStruct((m, d), F32),
        compiler_params=_cparams("parallel", "arbitrary", "arbitrary", vmem_limit_bytes=MOE_VMEM_LIMIT_BYTES),
        name="moe",
    )(cnt, x, hb, comb, rank, rank_t, wg, wu, wd)


def _moe_layer(x, a, wo, ln, rw, wg, wu, wd, *, tm, tf):
    x, hb, comb, rank, rank_t, cnt = _router(x, a, wo, ln, rw, tm=tm)
    return _moe(x, hb, comb, rank, rank_t, cnt.reshape(cnt.shape[0], LANES), wg, wu, wd, tm=tm, tf=tf)


def _block_diag(w):
    n, c, d = w.shape
    eye = jnp.eye(n, dtype=w.dtype)
    return (eye[:, None, :, None] * w[:, :, None, :]).reshape(n * c, n * d)


def _tile_rows(m, pref):
    return pref if m % pref == 0 else m


def kernel(x_prompt, x_sample, cache_k, cache_v, state_lru_conv, state_lru_h, state_delta_conv, state_delta_s, page_table, ln_mix_e, w_in_e, conv_lru_w, conv_lru_b, lru_wx, lru_bx, lru_wa, lru_ba, lru_lambda, qk_gain_q, qk_gain_k, lam_q1, lam_k1, lam_q2, lam_k2, subln_g, w_out_e, ln_ffn_e, ffn_wg, ffn_wu, ffn_wd, ln_mix_o, w_in_o, conv_delta_w, a_log, dt_bias, onorm_g, w_out_o, ln_ffn_o, router_w, moe_wg, moe_wu, moe_wd):
    bp, tp, d = x_prompt.shape
    bs = x_sample.shape[0]
    n_even, n_odd = w_in_e.shape[0], w_in_o.shape[0]
    lru_w = conv_lru_w.shape[-1]
    att_w = cache_k.shape[-1] * cache_k.shape[-2]
    qkv_c = conv_delta_w.shape[-1]
    dk = onorm_g.shape[-1]
    assert x_sample.shape[1] == 1 and lru_w == att_w and w_in_e.shape[-1] == 5 * lru_w

    xp = x_prompt.reshape(bp * tp, d)
    xs = x_sample.reshape(bs, d)
    mp = bp * tp
    tmp = _tile_rows(mp, 512)
    outs = {k: [] for k in ("kp", "vp", "cp", "hp", "dcp", "dsp", "ks", "vs", "cs", "hs", "dcs", "dss")}
    bf = lambda a: a.astype(BF16)
    vec = lambda a: a.reshape(1, -1)

    for layer in range(n_even + n_odd):
        i = layer // 2
        if layer % 2 == 0:
            lam_init = 0.8 - 0.6 * math.exp(-0.3 * layer)
            w_in, w_out = bf(w_in_e[i]), bf(w_out_e[i])
            lru_args = (conv_lru_w[i], vec(conv_lru_b[i]), bf(_block_diag(lru_wx[i])), vec(lru_bx[i]),
                        bf(_block_diag(lru_wa[i])), vec(lru_ba[i]), vec(lru_lambda[i]))
            lam_args = (lam_q1[i], lam_k1[i], lam_q2[i], lam_k2[i], subln_g[i])
            wg, wu, wd = bf(ffn_wg[i]), bf(ffn_wu[i]), bf(ffn_wd[i])

            gg, xr, qn, k_rows, v_rows, kb, vb = _inproj_even(xp, ln_mix_e[i], w_in, qk_gain_q[i],
                                                              qk_gain_k[i], tm=tmp)
            seq = lambda a: a.reshape(bp, tp, -1)
            ya, h_last = _lru_prompt(seq(xr), seq(gg), *lru_args, tt=_tile_rows(tp, 512))
            o = _attn_prompt(seq(qn), seq(kb), seq(vb), *lam_args, qk_gain_q[i], qk_gain_k[i],
                             lam_init=lam_init, tq=_tile_rows(tp, ATTN_ROWS))
            xp = _ffn(xp, [(ya.reshape(mp, lru_w), w_out[:lru_w]), (o.reshape(mp, att_w), w_out[lru_w:])],
                      ln_ffn_e[i], wg, wu, wd, tm=tmp)
            outs["kp"].append(k_rows.reshape(bp, tp, H_B, att_w // H_B))
            outs["vp"].append(v_rows.reshape(bp, tp, H_B, att_w // H_B))
            outs["cp"].append(seq(xr)[:, tp - (CONV_W - 1):])
            outs["hp"].append(h_last.reshape(bp, lru_w))

            gg, xr, qn, k_rows, v_rows, _, _ = _inproj_even(xs, ln_mix_e[i], w_in, qk_gain_q[i],
                                                            qk_gain_k[i], tm=bs)
            ya, h_new = _lru_step(xr, gg, state_lru_conv[i], state_lru_h[i], *lru_args)
            o = _attn_decode(qn, k_rows.reshape(bs, att_w), v_rows.reshape(bs, att_w), cache_k[i], cache_v[i],
                             page_table, *lam_args,
                             lam_init=lam_init)
            xs = _ffn(xs, [(ya, w_out[:lru_w]), (o, w_out[lru_w:])], ln_ffn_e[i], wg, wu, wd, tm=bs)
            outs["ks"].append(k_rows.reshape(bs, 1, H_B, att_w // H_B))
            outs["vs"].append(v_rows.reshape(bs, 1, H_B, att_w // H_B))
            outs["cs"].append(jnp.concatenate([state_lru_conv[i][:, 1:], xr[:, None]], axis=1))
            outs["hs"].append(h_new)
        else:
            w_in, w_out = bf(w_in_o[i]), bf(w_out_o[i])
            wg, wu, wd = bf(moe_wg[i]), bf(moe_wu[i]), bf(moe_wd[i])
            tf = MOE_FF_TILE

            qkv, z, gb = _inproj_odd(xp, ln_mix_o[i], w_in, a_log[i], dt_bias[i], tm=tmp)
            seq = lambda a: a.reshape(bp, tp, -1)
            o, s_fin = _delta_prompt(seq(qkv), seq(z), seq(gb), conv_delta_w[i], onorm_g[i],
                                     tc=_tile_rows(tp, 256))
            xp = _moe_layer(xp, o.reshape(mp, -1), w_out, ln_ffn_o[i], router_w[i], wg, wu, wd,
                            tm=_tile_rows(mp, MOE_TOKENS), tf=tf)
            outs["dcp"].append(seq(qkv)[:, tp - (CONV_W - 1):])
            outs["dsp"].append(s_fin)

            qkv, z, gb = _inproj_odd(xs, ln_mix_o[i], w_in, a_log[i], dt_bias[i], tm=bs)
            o, s_new = _delta_step(qkv, state_delta_conv[i], z, gb, state_delta_s[i], conv_delta_w[i],
                                   onorm_g[i])
            xs = _moe_layer(xs, o, w_out, ln_ffn_o[i], router_w[i], wg, wu, wd, tm=bs, tf=tf)
            outs["dcs"].append(jnp.concatenate([state_delta_conv[i][:, 1:], qkv[:, None]], axis=1))
            outs["dss"].append(s_new)

    st = lambda k: jnp.stack(outs[k])
    return (xp.reshape(bp, tp, d), xs.reshape(bs, 1, d),
            st("kp"), st("vp"), st("cp"), st("hp"), st("dcp"), st("dsp"),
            st("ks"), st("vs"), st("cs"), st("hs"), st("dcs"), st("dss"))
```
